```python
import math
import jax, jax.numpy as jnp
from jax import lax
import numpy as np

D_MODEL = 1024
BATCH = 4
SEQ = 4096
DEPTH = 4

N_MIXERS = 2
N_HEADS = 8
HEAD_DIM = D_MODEL // N_HEADS
MOBA_BLOCK = 256
MOBA_TOPK = 3
MOBA_QCHUNK = 128
NUM_BUCKETS = 32
REL_MAX_DISTANCE = 128
LRU_WIDTH = D_MODEL
LRU_BLOCKS = 4
LRU_BLOCK_WIDTH = LRU_WIDTH // LRU_BLOCKS
CONV_WIDTH = 4
LRU_C = 8.0
N_EXPERTS = 32
EXPERT_TOPK = 4
D_FF = D_MODEL
SWIGLU_LIMIT = 7.0
SWIGLU_ALPHA = 1.702
EXPERT_CHUNK = 128
LN_EPS = 1e-5
DEEPNORM_ALPHA = (2 * DEPTH) ** 0.25
DEEPNORM_BETA = (8 * DEPTH) ** -0.25
N_MOBA_LAYERS = (DEPTH + N_MIXERS - 1) // N_MIXERS
N_LRU_LAYERS = DEPTH // N_MIXERS

kernel_name = 'hybrid_moba_rglru_moe_deepnorm'


def _layer_norm(x, g, b):
    xf = x.astype(jnp.float32)
    mu = jnp.mean(xf, axis=-1, keepdims=True)
    var = jnp.mean(jnp.square(xf - mu), axis=-1, keepdims=True)
    y = (xf - mu) * lax.rsqrt(var + LN_EPS)
    return (y * g.astype(jnp.float32) + b.astype(jnp.float32)).astype(x.dtype)


def _t5_bucket(dist):
    max_exact = NUM_BUCKETS // 2
    n = dist.astype(jnp.float32)
    large = max_exact + (jnp.log(jnp.maximum(n, 1.0) / max_exact)
                         / math.log(REL_MAX_DISTANCE / max_exact)
                         * (NUM_BUCKETS - max_exact)).astype(jnp.int32)
    large = jnp.minimum(large, NUM_BUCKETS - 1)
    return jnp.where(dist < max_exact, dist, large)


def _grouped_layout(ids, n_groups, chunk):
    n = ids.shape[0]
    n_rows = n + n_groups * chunk
    counts = jax.ops.segment_sum(jnp.ones_like(ids), ids, num_segments=n_groups + 1)[:n_groups]
    padded = (counts + chunk - 1) // chunk * chunk
    pad_end = jnp.cumsum(padded)
    pad_start = pad_end - padded
    raw_start = jnp.cumsum(counts) - counts
    order = jnp.argsort(ids)
    sorted_ids = ids[order]
    grp = jnp.minimum(sorted_ids, n_groups - 1)
    rank = jnp.arange(n, dtype=jnp.int32) - raw_start[grp]
    dest_sorted = jnp.where(sorted_ids < n_groups, pad_start[grp] + rank, n_rows).astype(jnp.int32)
    dest = jnp.zeros_like(ids).at[order].set(dest_sorted)
    starts = jnp.arange(n_rows // chunk, dtype=jnp.int32) * chunk
    chunk_group = jnp.sum((pad_end[None, :] <= starts[:, None]).astype(jnp.int32), axis=1)
    chunk_group = jnp.minimum(chunk_group, n_groups - 1)
    return dest, chunk_group, n_rows


def _moba_head(q, k, v, bias_by_dist):
    tp, dh = q.shape
    nblk = tp // MOBA_BLOCK
    k_sel = min(MOBA_TOPK, nblk)
    q = q.astype(jnp.float32)
    k = k.astype(jnp.float32)
    v = v.astype(jnp.float32)
    q_blk = jnp.arange(tp, dtype=jnp.int32) // MOBA_BLOCK
    kb = k.reshape(nblk, MOBA_BLOCK, dh)
    vb = v.reshape(nblk, MOBA_BLOCK, dh)

    qb = q.reshape(nblk, MOBA_BLOCK, dh)
    rel = jnp.arange(MOBA_BLOCK)[:, None] - jnp.arange(MOBA_BLOCK)[None, :]
    s_self = jnp.einsum('nqd,nkd->nqk', qb, kb) + bias_by_dist[jnp.maximum(rel, 0)]
    s_self = jnp.where(rel >= 0, s_self, -jnp.inf)
    m_self = jnp.max(s_self, axis=-1, keepdims=True)
    p_self = jnp.exp(s_self - m_self)
    l_self = jnp.sum(p_self, axis=-1, keepdims=True)
    o_self = (jnp.einsum('nqk,nkd->nqd', p_self, vb) / l_self).reshape(tp, dh)
    lse_self = (m_self + jnp.log(l_self)).reshape(tp)

    k_mean = jnp.mean(kb, axis=1)
    gate = q @ k_mean.T
    past = jnp.arange(nblk)[None, :] < q_blk[:, None]
    _, sel = lax.top_k(jnp.where(past, gate, -jnp.inf), k_sel)
    valid = jnp.arange(k_sel)[None, :] < q_blk[:, None]
    ids = jnp.where(valid, sel, nblk).reshape(-1).astype(jnp.int32)
    n_assign = tp * k_sel

    dest, chunk_blk, n_rows = _grouped_layout(ids, nblk, MOBA_QCHUNK)
    n_chunks = n_rows // MOBA_QCHUNK
    src = jnp.full((n_rows,), n_assign, jnp.int32).at[dest].set(
        jnp.arange(n_assign, dtype=jnp.int32), mode='drop')
    tq = (src // k_sel).reshape(n_chunks, MOBA_QCHUNK)
    q_ext = jnp.concatenate([q, jnp.zeros((1, dh), q.dtype)], axis=0)
    qg = q_ext[tq]
    kg = kb[chunk_blk]
    vg = vb[chunk_blk]
    k_pos = chunk_blk[:, None] * MOBA_BLOCK + jnp.arange(MOBA_BLOCK)[None, :]
    dist = jnp.clip(tq[:, :, None] - k_pos[:, None, :], 0, tp)
    s = jnp.einsum('cqd,ckd->cqk', qg, kg) + bias_by_dist[dist]
    m = jnp.max(s, axis=-1, keepdims=True)
    p = jnp.exp(s - m)
    l = jnp.sum(p, axis=-1, keepdims=True)
    o_g = (jnp.einsum('cqk,ckd->cqd', p, vg) / l).reshape(n_rows, dh)
    lse_g = (m + jnp.log(l)).reshape(n_rows)

    row = jnp.minimum(dest, n_rows - 1)
    o_sel = o_g[row].reshape(tp, k_sel, dh)
    lse_sel = jnp.where(valid, lse_g[row].reshape(tp, k_sel), -jnp.inf)
    w = jax.nn.softmax(jnp.concatenate([lse_self[:, None], lse_sel], axis=1), axis=-1)
    return w[:, :1] * o_self + jnp.einsum('tk,tkd->td', w[:, 1:], o_sel)


def _moba_mixer(x, w_qkv, w_o, rel_bias_table):
    b, t, d = x.shape
    tp = -(-t // MOBA_BLOCK) * MOBA_BLOCK
    qkv = x @ w_qkv
    q, k, v = jnp.split(qkv, 3, axis=-1)
    def heads(z):
        z = z.reshape(b, t, N_HEADS, HEAD_DIM).transpose(0, 2, 1, 3)
        return jnp.pad(z, ((0, 0), (0, 0), (0, tp - t), (0, 0)))
    q = heads(q) * (HEAD_DIM ** -0.5)
    k = heads(k)
    v = heads(v)
    bias_by_dist = rel_bias_table.astype(jnp.float32)[_t5_bucket(jnp.arange(tp + 1, dtype=jnp.int32))].T
    head_fn = jax.vmap(_moba_head, in_axes=(0, 0, 0, 0))
    o = lax.map(lambda a: head_fn(a[0], a[1], a[2], bias_by_dist), (q, k, v))
    o = o[:, :, :t].transpose(0, 2, 1, 3).reshape(b, t, d).astype(x.dtype)
    return o @ w_o


def _rglru_mixer(x, w_in, conv_w, conv_b, w_gx, b_gx, w_ga, b_ga, a_param, w_out):
    b, t, d = x.shape
    yz = x @ w_in
    y_br, r = jnp.split(yz, 2, axis=-1)
    y_br = jax.nn.gelu(y_br, approximate=True)
    r = lax.conv_general_dilated(r, conv_w[:, None, :], window_strides=(1,),
                                 padding=[(CONV_WIDTH - 1, 0)],
                                 dimension_numbers=('NWC', 'WIO', 'NWC'),
                                 feature_group_count=LRU_WIDTH) + conv_b
    rb = r.reshape(b, t, LRU_BLOCKS, LRU_BLOCK_WIDTH)
    gate_x = jax.nn.sigmoid(jnp.einsum('btnd,nde->btne', rb, w_gx).reshape(b, t, LRU_WIDTH) + b_gx)
    gate_a = jax.nn.sigmoid(jnp.einsum('btnd,nde->btne', rb, w_ga).reshape(b, t, LRU_WIDTH) + b_ga)
    log_a = -LRU_C * gate_a.astype(jnp.float32) * jax.nn.softplus(-a_param.astype(jnp.float32))
    a = jnp.exp(log_a)
    u = jnp.sqrt(-jnp.expm1(2.0 * log_a)) * (gate_x * r).astype(jnp.float32)
    def combine(e1, e2):
        a1, b1 = e1
        a2, b2 = e2
        return a1 * a2, a2 * b1 + b2
    _, h = lax.associative_scan(combine, (a, u), axis=1)
    return (h.astype(x.dtype) * y_br) @ w_out


def _clamped_swiglu(g, u):
    g = jnp.minimum(g, SWIGLU_LIMIT)
    u = jnp.clip(u, -SWIGLU_LIMIT, SWIGLU_LIMIT)
    return g * jax.nn.sigmoid(SWIGLU_ALPHA * g) * (u + 1.0)


def _moe_ffn(x, w_router, b_router, w_gate, b_gate, w_up, b_up, w_down, b_down):
    b, t, d = x.shape
    n_tok = b * t
    xf = x.reshape(n_tok, d)
    logits = (xf @ w_router + b_router).astype(jnp.float32)
    top_logit, top_idx = lax.top_k(logits, EXPERT_TOPK)
    gates = jax.nn.softmax(top_logit, axis=-1).astype(x.dtype)
    n_assign = n_tok * EXPERT_TOPK
    ids = top_idx.reshape(n_assign).astype(jnp.int32)
    dest, chunk_expert, n_rows = _grouped_layout(ids, N_EXPERTS, EXPERT_CHUNK)
    src = jnp.full((n_rows,), n_assign, jnp.int32).at[dest].set(
        jnp.arange(n_assign, dtype=jnp.int32), mode='drop')
    x_ext = jnp.concatenate([xf, jnp.zeros((1, d), x.dtype)], axis=0)
    x_rows = x_ext[src // EXPERT_TOPK].reshape(n_rows // EXPERT_CHUNK, EXPERT_CHUNK, d)
    def expert_chunk(args):
        xc, e = args
        g = xc @ w_gate[e] + b_gate[e]
        u = xc @ w_up[e] + b_up[e]
        return _clamped_swiglu(g, u) @ w_down[e] + b_down[e]
    y_rows = lax.map(expert_chunk, (x_rows, chunk_expert)).reshape(n_rows, d)
    y_sel = y_rows[dest].reshape(n_tok, EXPERT_TOPK, d)
    return jnp.einsum('nk,nkd->nd', gates, y_sel).reshape(b, t, d)


def setup_inputs(seed: int = 0) -> dict:
    key = jax.random.key(seed)
    ks = jax.random.split(key, 25)
    f32 = jnp.float32
    nrm = lambda k, s, sc: jax.random.normal(k, s, f32) * sc
    a_u = jax.random.uniform(ks[11], (N_LRU_LAYERS, LRU_WIDTH), f32, minval=0.9, maxval=0.999)
    return {
        'x': jax.random.normal(ks[0], (BATCH, SEQ, D_MODEL), f32),
        'rel_bias_table': nrm(ks[1], (NUM_BUCKETS, N_HEADS), 0.5),
        'w_qkv': nrm(ks[2], (N_MOBA_LAYERS, D_MODEL, 3 * D_MODEL), D_MODEL ** -0.5),
        'w_attn_out': nrm(ks[3], (N_MOBA_LAYERS, D_MODEL, D_MODEL), D_MODEL ** -0.5 * DEEPNORM_BETA),
        'w_lru_in': nrm(ks[4], (N_LRU_LAYERS, D_MODEL, 2 * LRU_WIDTH), D_MODEL ** -0.5),
        'lru_conv_w': nrm(ks[5], (N_LRU_LAYERS, CONV_WIDTH, LRU_WIDTH), CONV_WIDTH ** -0.5),
        'lru_conv_b': nrm(ks[6], (N_LRU_LAYERS, LRU_WIDTH), 0.01),
        'w_lru_gate_x': nrm(ks[7], (N_LRU_LAYERS, LRU_BLOCKS, LRU_BLOCK_WIDTH, LRU_BLOCK_WIDTH), LRU_BLOCK_WIDTH ** -0.5),
        'b_lru_gate_x': nrm(ks[8], (N_LRU_LAYERS, LRU_WIDTH), 0.01),
        'w_lru_gate_a': nrm(ks[9], (N_LRU_LAYERS, LRU_BLOCKS, LRU_BLOCK_WIDTH, LRU_BLOCK_WIDTH), LRU_BLOCK_WIDTH ** -0.5),
        'b_lru_gate_a': nrm(ks[10], (N_LRU_LAYERS, LRU_WIDTH), 0.01),
        'lru_a_param': jnp.log(a_u) - jnp.log1p(-a_u),
        'w_lru_out': nrm(ks[12], (N_LRU_LAYERS, LRU_WIDTH, D_MODEL), LRU_WIDTH ** -0.5 * DEEPNORM_BETA),
        'w_router': nrm(ks[13], (DEPTH, D_MODEL, N_EXPERTS), D_MODEL ** -0.5),
        'b_router': nrm(ks[14], (DEPTH, N_EXPERTS), 0.01),
        'w_exp_gate': nrm(ks[15], (DEPTH, N_EXPERTS, D_MODEL, D_FF), D_MODEL ** -0.5),
        'b_exp_gate': nrm(ks[16], (DEPTH, N_EXPERTS, D_FF), 0.01),
        'w_exp_up': nrm(ks[17], (DEPTH, N_EXPERTS, D_MODEL, D_FF), D_MODEL ** -0.5),
        'b_exp_up': nrm(ks[18], (DEPTH, N_EXPERTS, D_FF), 0.01),
        'w_exp_down': nrm(ks[19], (DEPTH, N_EXPERTS, D_FF, D_MODEL), D_FF ** -0.5 * DEEPNORM_BETA),
        'b_exp_down': nrm(ks[20], (DEPTH, N_EXPERTS, D_MODEL), 0.01),
        'ln_mix_g': 1.0 + nrm(ks[21], (DEPTH, D_MODEL), 0.01),
        'ln_mix_b': nrm(ks[22], (DEPTH, D_MODEL), 0.01),
        'ln_ffn_g': 1.0 + nrm(ks[23], (DEPTH, D_MODEL), 0.01),
        'ln_ffn_b': nrm(ks[24], (DEPTH, D_MODEL), 0.01),
    }


def reference(x, rel_bias_table, w_qkv, w_attn_out, w_lru_in, lru_conv_w, lru_conv_b,
              w_lru_gate_x, b_lru_gate_x, w_lru_gate_a, b_lru_gate_a, lru_a_param, w_lru_out,
              w_router, b_router, w_exp_gate, b_exp_gate, w_exp_up, b_exp_up,
              w_exp_down, b_exp_down, ln_mix_g, ln_mix_b, ln_ffn_g, ln_ffn_b):
    for i in range(DEPTH):
        j = i // N_MIXERS
        if i % N_MIXERS == 0:
            mix = _moba_mixer(x, w_qkv[j], w_attn_out[j], rel_bias_table)
        else:
            mix = _rglru_mixer(x, w_lru_in[j], lru_conv_w[j], lru_conv_b[j],
                               w_lru_gate_x[j], b_lru_gate_x[j], w_lru_gate_a[j], b_lru_gate_a[j],
                               lru_a_param[j], w_lru_out[j])
        x = _layer_norm(DEEPNORM_ALPHA * x + mix, ln_mix_g[i], ln_mix_b[i])
        ffn = _moe_ffn(x, w_router[i], b_router[i], w_exp_gate[i], b_exp_gate[i],
                       w_exp_up[i], b_exp_up[i], w_exp_down[i], b_exp_down[i])
        x = _layer_norm(DEEPNORM_ALPHA * x + ffn, ln_ffn_g[i], ln_ffn_b[i])
    return x
```

```python
import functools
import math

import numpy as np
import jax
import jax.numpy as jnp
from jax import lax
from jax.experimental import pallas as pl
from jax.experimental.pallas import tpu as pltpu

F32 = jnp.float32
BF16 = jnp.bfloat16
I32 = jnp.int32

N_HEADS = 8
MOBA_BLOCK = 256
MOBA_TOPK = 3
NUM_BUCKETS = 32
REL_MAX_DISTANCE = 128
LRU_C = 8.0
EXPERT_TOPK = 4
SWIGLU_LIMIT = 7.0
SWIGLU_ALPHA = 1.702
LN_EPS = 1e-5

EXPERT_ROWS = 256
TOKEN_TILE = 512
SCAN_ROWS = 8
V7X_VMEM_LIMIT = 56 * 1024 * 1024

_NT = (((1,), (1,)), ((), ()))


def _params(n_axes, vmem=V7X_VMEM_LIMIT):
    return pltpu.CompilerParams(
        dimension_semantics=("arbitrary",) * n_axes, vmem_limit_bytes=vmem)


def _sigmoid(x):
    return 1.0 / (1.0 + jnp.exp(-x))


def _layer_norm(z, g, b):
    mu = jnp.mean(z, axis=-1, keepdims=True)
    zc = z - mu
    var = jnp.mean(zc * zc, axis=-1, keepdims=True)
    return zc * lax.rsqrt(var + LN_EPS) * g + b


def _matmul_kernel(x_ref, w_ref, o_ref):
    o_ref[...] = jnp.dot(x_ref[...].astype(BF16), w_ref[...],
                         preferred_element_type=F32).astype(o_ref.dtype)


def _matmul(x, w, out_dtype):
    m, k = x.shape
    n = w.shape[1]
    tm = min(TOKEN_TILE, m)
    return pl.pallas_call(
        _matmul_kernel,
        grid=(m // tm,),
        in_specs=[pl.BlockSpec((tm, k), lambda i: (i, 0)),
                  pl.BlockSpec((k, n), lambda i: (0, 0))],
        out_specs=pl.BlockSpec((tm, n), lambda i: (i, 0)),
        out_shape=jax.ShapeDtypeStruct((m, n), out_dtype),
        compiler_params=_params(1),
        name="dense_proj",
    )(x, w)


def _attn_kernel(q_ref, k_ref, v_ref, bself_ref, badj_ref, bfar_ref, o_ref,
                 kmean_ref, vt_ref, sel_ref, m_ref, l_ref, acc_ref, *, nblk, scale):
    blk = MOBA_BLOCK
    i = pl.program_id(1)

    @pl.when(i == 0)
    def _():
        for j in range(nblk):
            kj = k_ref[j * blk:(j + 1) * blk, :].astype(F32)
            kmean_ref[j:j + 1, :] = jnp.sum(kj, axis=0, keepdims=True) * (1.0 / blk)
            vt_ref[j] = v_ref[j * blk:(j + 1) * blk, :].astype(F32).T.astype(BF16)

    q = q_ref[...]

    km = kmean_ref[...]
    km_hi = km.astype(BF16)
    km_lo = (km - km_hi.astype(F32)).astype(BF16)
    gate = (lax.dot_general(km_hi, q, _NT, preferred_element_type=F32)
            + lax.dot_general(km_lo, q, _NT, preferred_element_type=F32))
    jj = lax.broadcasted_iota(I32, gate.shape, 0)
    past = jj < i
    g = jnp.where(past, gate, -jnp.inf)
    sel = jnp.zeros(gate.shape, F32)
    for _ in range(min(MOBA_TOPK, nblk)):
        mx = jnp.max(g, axis=0, keepdims=True)
        jm = jnp.min(jnp.where(g == mx, jj, nblk), axis=0, keepdims=True)
        pick = jj == jm
        sel = jnp.where(pick, 1.0, sel)
        g = jnp.where(pick, -jnp.inf, g)
    sel_ref[...] = jnp.where(past, sel, 0.0)

    def scores(j):
        kb = k_ref[pl.ds(pl.multiple_of(j * blk, blk), blk), :]
        return lax.dot_general(kb, q, _NT, preferred_element_type=F32) * scale

    s = scores(i) + bself_ref[...]
    m0 = jnp.max(s, axis=0, keepdims=True)
    p = jnp.exp(s - m0)
    m_ref[...] = m0
    l_ref[...] = jnp.sum(p, axis=0, keepdims=True)
    acc_ref[...] = jnp.dot(vt_ref[i], p.astype(BF16), preferred_element_type=F32)

    def visit(j, bias):
        s = scores(j) + bias
        chosen = sel_ref[pl.ds(j, 1), :] > 0.5
        s = jnp.where(chosen, s, -jnp.inf)
        m_old = m_ref[...]
        m_new = jnp.maximum(m_old, jnp.max(s, axis=0, keepdims=True))
        alpha = jnp.exp(m_old - m_new)
        p = jnp.exp(s - m_new)
        l_ref[...] = alpha * l_ref[...] + jnp.sum(p, axis=0, keepdims=True)
        acc_ref[...] = alpha * acc_ref[...] + jnp.dot(
            vt_ref[j], p.astype(BF16), preferred_element_type=F32)
        m_ref[...] = m_new

    @pl.when(i >= 1)
    def _():
        visit(i - 1, badj_ref[...])

    bfar = bfar_ref[0:1, 0:1]

    def far(j, carry):
        visit(j, bfar)
        return carry

    lax.fori_loop(0, jnp.maximum(i - 1, 0), far, 0)

    o = acc_ref[...] / l_ref[...]
    o_ref[...] = o.T.astype(o_ref.dtype)


def _t5_bucket_np(dist):
    max_exact = NUM_BUCKETS // 2
    n = dist.astype(np.float32)
    large = max_exact + (np.log(np.maximum(n, np.float32(1.0)) / np.float32(max_exact))
                         / np.float32(math.log(REL_MAX_DISTANCE / max_exact))
                         * np.float32(NUM_BUCKETS - max_exact)).astype(np.int32)
    large = np.minimum(large, NUM_BUCKETS - 1)
    return np.where(dist < max_exact, dist, large)


def _t5_bucket(dist):
    max_exact = NUM_BUCKETS // 2
    n = dist.astype(F32)
    large = max_exact + (jnp.log(jnp.maximum(n, 1.0) / max_exact)
                         / math.log(REL_MAX_DISTANCE / max_exact)
                         * (NUM_BUCKETS - max_exact)).astype(I32)
    large = jnp.minimum(large, NUM_BUCKETS - 1)
    return jnp.where(dist < max_exact, dist, large)


def _attn_biases(rel_bias_table, t):
    blk = MOBA_BLOCK
    far_buckets = _t5_bucket_np(np.arange(blk + 1, max(t, 2 * blk) + 1, dtype=np.int32))
    assert (far_buckets == NUM_BUCKETS - 1).all()
    bbd = rel_bias_table.astype(F32)[_t5_bucket(jnp.arange(2 * blk + 1, dtype=I32))].T
    kc = jnp.arange(blk, dtype=I32)[:, None]
    qr = jnp.arange(blk, dtype=I32)[None, :]
    bself = jnp.where(qr >= kc, bbd[:, jnp.maximum(qr - kc, 0)], -jnp.inf)
    badj = bbd[:, blk + qr - kc]
    bfar = jnp.broadcast_to(bbd[:, 2 * blk][:, None, None], (bbd.shape[0], 8, 128))
    return bself, badj, bfar


def _moba_attention(qkv, biases):
    t, d3 = qkv.shape
    d = d3 // 3
    dh = d // N_HEADS
    blk = MOBA_BLOCK
    nblk = t // blk
    bself, badj, bfar = biases
    kernel = functools.partial(_attn_kernel, nblk=nblk, scale=dh ** -0.5)
    return pl.pallas_call(
        kernel,
        grid=(N_HEADS, nblk),
        in_specs=[pl.BlockSpec((blk, dh), lambda h, i: (i, h)),
                  pl.BlockSpec((t, dh), lambda h, i: (0, N_HEADS + h)),
                  pl.BlockSpec((t, dh), lambda h, i: (0, 2 * N_HEADS + h)),
                  pl.BlockSpec((None, blk, blk), lambda h, i: (h, 0, 0)),
                  pl.BlockSpec((None, blk, blk), lambda h, i: (h, 0, 0)),
                  pl.BlockSpec((None, 8, 128), lambda h, i: (h, 0, 0))],
        out_specs=pl.BlockSpec((blk, dh), lambda h, i: (i, h)),
        out_shape=jax.ShapeDtypeStruct((t, d), BF16),
        scratch_shapes=[pltpu.VMEM((nblk, dh), F32),
                        pltpu.VMEM((nblk, dh, blk), BF16),
                        pltpu.VMEM((nblk, blk), F32),
                        pltpu.VMEM((1, blk), F32),
                        pltpu.VMEM((1, blk), F32),
                        pltpu.VMEM((dh, blk), F32)],
        compiler_params=_params(2),
        name="moba_attention",
    )(qkv, qkv, qkv, bself, badj, bfar)


def _proj_ln_router_kernel(a_ref, w_ref, x_ref, g_ref, b_ref, wrh_ref, wrl_ref, br_ref,
                           xn_ref, ids_ref, gates_ref, rank_ref, counts_ref, base_ref,
                           *, alpha, n_exp):
    t = pl.program_id(0)
    tm = a_ref.shape[0]

    @pl.when(t == 0)
    def _():
        base_ref[...] = jnp.zeros_like(base_ref)

    mix = jnp.dot(a_ref[...].astype(BF16), w_ref[...], preferred_element_type=F32)
    xn = _layer_norm(alpha * x_ref[...] + mix, g_ref[...], b_ref[...])
    xn_ref[...] = xn

    x_hi = xn.astype(BF16)
    x_lo = (xn - x_hi.astype(F32)).astype(BF16)
    wr_hi = wrh_ref[...]
    logits = (lax.dot_general(wr_hi, x_hi, _NT, preferred_element_type=F32)
              + lax.dot_general(wr_hi, x_lo, _NT, preferred_element_type=F32)
              + lax.dot_general(wrl_ref[...], x_hi, _NT, preferred_element_type=F32)
              + br_ref[...])

    e_iota = lax.broadcasted_iota(I32, logits.shape, 0)
    work = logits
    tops, picks = [], []
    for k in range(EXPERT_TOPK):
        mx = jnp.max(work, axis=0, keepdims=True)
        em = jnp.min(jnp.where(work == mx, e_iota, n_exp), axis=0, keepdims=True)
        pick = e_iota == em
        ids_ref[k:k + 1, :] = em
        tops.append(mx)
        picks.append(pick)
        work = jnp.where(pick, -jnp.inf, work)

    ex = [jnp.exp(tk - tops[0]) for tk in tops]
    den = ex[0]
    for e in ex[1:]:
        den = den + e
    for k in range(EXPERT_TOPK):
        gates_ref[k:k + 1, :] = ex[k] / den

    onehot = jnp.zeros(logits.shape, F32)
    for pick in picks:
        onehot = onehot + pick.astype(F32)
    earlier = (lax.broadcasted_iota(I32, (tm, tm), 0)
               < lax.broadcasted_iota(I32, (tm, tm), 1)).astype(BF16)
    pos = jnp.dot(onehot.astype(BF16), earlier, preferred_element_type=F32) + base_ref[:, 0:1]
    for k in range(EXPERT_TOPK):
        rank_ref[k:k + 1, :] = jnp.sum(jnp.where(picks[k], pos, 0.0), axis=0,
                                       keepdims=True).astype(I32)
    total = base_ref[...] + jnp.sum(onehot, axis=1, keepdims=True)
    base_ref[...] = total
    counts_ref[...] = total


def _proj_ln_router(a, w, x, ln_g, ln_b, wr_hi, wr_lo, br, alpha):
    t, d = x.shape
    n_exp = wr_hi.shape[0]
    tm = min(TOKEN_TILE, t)
    kernel = functools.partial(_proj_ln_router_kernel, alpha=alpha, n_exp=n_exp)
    row = lambda i: (i, 0)
    fixed = lambda i: (0, 0)
    col = lambda i: (0, i)
    return pl.pallas_call(
        kernel,
        grid=(t // tm,),
        in_specs=[pl.BlockSpec((tm, d), row),
                  pl.BlockSpec((d, d), fixed),
                  pl.BlockSpec((tm, d), row),
                  pl.BlockSpec((1, d), fixed),
                  pl.BlockSpec((1, d), fixed),
                  pl.BlockSpec((n_exp, d), fixed),
                  pl.BlockSpec((n_exp, d), fixed),
                  pl.BlockSpec((n_exp, 1), fixed)],
        out_specs=[pl.BlockSpec((tm, d), row),
                   pl.BlockSpec((EXPERT_TOPK, tm), col),
                   pl.BlockSpec((EXPERT_TOPK, tm), col),
                   pl.BlockSpec((EXPERT_TOPK, tm), col),
                   pl.BlockSpec((n_exp, 128), fixed)],
        out_shape=[jax.ShapeDtypeStruct((t, d), F32),
                   jax.ShapeDtypeStruct((EXPERT_TOPK, t), I32),
                   jax.ShapeDtypeStruct((EXPERT_TOPK, t), F32),
                   jax.ShapeDtypeStruct((EXPERT_TOPK, t), I32),
                   jax.ShapeDtypeStruct((n_exp, 128), F32)],
        scratch_shapes=[pltpu.VMEM((n_exp, 128), F32)],
        compiler_params=_params(1),
        name="proj_ln_router",
    )(a, w, x, ln_g, ln_b, wr_hi, wr_lo, br)


def _dispatch_kernel(dest_ref, gate_ref, start_ref, count_ref, end_ref, nused_ref,
                     x3_ref, rows_ref, src_ref, grow_ref, xg_ref, *, n_tok, n_exp):
    c = pl.program_id(0)
    tm = xg_ref.shape[0]
    n_rows = src_ref.shape[0]

    @pl.when(c == 0)
    def _():
        def fill(r, carry):
            src_ref[r] = n_tok
            grow_ref[r] = 0.0
            return carry

        for e in range(n_exp):
            lax.fori_loop(start_ref[e] + count_ref[e], end_ref[e], fill, 0)
        lax.fori_loop(end_ref[n_exp - 1], n_rows, fill, 0)

        def scatter(a, carry):
            for k in range(EXPERT_TOPK):
                d = dest_ref[k * n_tok + a]
                src_ref[d] = a
                grow_ref[d] = gate_ref[k * n_tok + a]
            return carry

        lax.fori_loop(0, n_tok, scatter, 0, unroll=4)

    @pl.when(c < nused_ref[0])
    def _():
        def gather(r8, carry):
            for k in range(8):
                r = r8 * 8 + k
                tok = jnp.minimum(src_ref[c * tm + r], n_tok - 1)
                xg_ref[r] = x3_ref[tok]
            return carry

        lax.fori_loop(0, tm // 8, gather, 0)
        rows_ref[...] = xg_ref[...].reshape(tm, xg_ref.shape[2]).astype(rows_ref.dtype)

    @pl.when(c >= nused_ref[0])
    def _():
        rows_ref[...] = jnp.zeros_like(rows_ref)


def _dispatch(x, dest, gates, start, count, end, nused, n_rows):
    t, d = x.shape
    tm = EXPERT_ROWS
    n_exp = start.shape[0]
    kernel = functools.partial(_dispatch_kernel, n_tok=t, n_exp=n_exp)
    smem = pl.BlockSpec(memory_space=pltpu.SMEM)
    grid_spec = pltpu.PrefetchScalarGridSpec(
        num_scalar_prefetch=6,
        grid=(n_rows // tm,),
        in_specs=[pl.BlockSpec((t, 1, d), lambda c, *_: (0, 0, 0))],
        out_specs=[pl.BlockSpec((tm, d), lambda c, *_: (c, 0)), smem, smem],
        scratch_shapes=[pltpu.VMEM((tm, 1, d), F32)],
    )
    return pl.pallas_call(
        kernel,
        grid_spec=grid_spec,
        out_shape=[jax.ShapeDtypeStruct((n_rows, d), BF16),
                   jax.ShapeDtypeStruct((n_rows,), I32),
                   jax.ShapeDtypeStruct((n_rows,), F32)],
        compiler_params=_params(1),
        name="moe_dispatch",
    )(dest, gates, start, count, end, nused, x.reshape(t, 1, d))


def _expert_kernel(ce_ref, nused_ref, src_ref, grow_ref,
                   x_ref, wg_ref, wu_ref, wd_ref, bg_ref, bu_ref, bd_ref,
                   out_ref, y3_ref):
    c = pl.program_id(0)
    tm, d = x_ref.shape

    @pl.when(c == 0)
    def _():
        out_ref[...] = jnp.zeros_like(out_ref)

    @pl.when(c < nused_ref[0])
    def _():
        x = x_ref[...]
        g = jnp.dot(x, wg_ref[...], preferred_element_type=F32) + bg_ref[...]
        u = jnp.dot(x, wu_ref[...], preferred_element_type=F32) + bu_ref[...]
        g = jnp.minimum(g, SWIGLU_LIMIT)
        u = jnp.clip(u, -SWIGLU_LIMIT, SWIGLU_LIMIT)
        h = g * _sigmoid(SWIGLU_ALPHA * g) * (u + 1.0)
        y = jnp.dot(h.astype(BF16), wd_ref[...], preferred_element_type=F32) + bd_ref[...]
        y3_ref[...] = y.reshape(tm, 1, d)

        def combine(r8, carry):
            base = c * tm + r8 * 8
            toks = [src_ref[base + k] for k in range(8)]
            gs = [grow_ref[base + k] for k in range(8)]
            olds = [out_ref[toks[k]] for k in range(8)]
            for k in range(8):
                out_ref[toks[k]] = olds[k] + gs[k] * y3_ref[r8 * 8 + k]
            return carry

        lax.fori_loop(0, tm // 8, combine, 0)


def _experts(rows, ce, nused, src, grow, layer, wg, wu, wd, bg, bu, bd, n_tok):
    n_rows, d = rows.shape
    tm = EXPERT_ROWS
    dff = wg.shape[3]

    def row_map(c, ce_ref, nused_ref, *_):
        return (jnp.minimum(c, nused_ref[0] - 1), 0)

    def w_map(c, ce_ref, *_):
        return (layer, ce_ref[c], 0, 0)

    grid_spec = pltpu.PrefetchScalarGridSpec(
        num_scalar_prefetch=4,
        grid=(n_rows // tm,),
        in_specs=[pl.BlockSpec((tm, d), row_map),
                  pl.BlockSpec((None, None, d, dff), w_map),
                  pl.BlockSpec((None, None, d, dff), w_map),
                  pl.BlockSpec((None, None, dff, d), w_map),
                  pl.BlockSpec((None, None, 1, dff), w_map),
                  pl.BlockSpec((None, None, 1, dff), w_map),
                  pl.BlockSpec((None, None, 1, d), w_map)],
        out_specs=pl.BlockSpec((n_tok + 8, 1, d), lambda c, *_: (0, 0, 0)),
        scratch_shapes=[pltpu.VMEM((tm, 1, d), F32)],
    )
    return pl.pallas_call(
        _expert_kernel,
        grid_spec=grid_spec,
        out_shape=jax.ShapeDtypeStruct((n_tok + 8, 1, d), F32),
        compiler_params=_params(1),
        name="moe_experts",
    )(ce, nused, src, grow, rows, wg, wu, wd, bg, bu, bd)


def _add_ln_kernel(x_ref, f3_ref, g_ref, b_ref, o_ref, f2_ref, *, alpha):
    tm, d = x_ref.shape
    f2_ref[...] = f3_ref[...].reshape(tm, d)
    o_ref[...] = _layer_norm(alpha * x_ref[...] + f2_ref[...], g_ref[...], b_ref[...])


def _add_ln(x, ffn3, ln_g, ln_b, alpha):
    t, d = x.shape
    tm = min(TOKEN_TILE, t)
    return pl.pallas_call(
        functools.partial(_add_ln_kernel, alpha=alpha),
        grid=(t // tm,),
        in_specs=[pl.BlockSpec((tm, d), lambda i: (i, 0)),
                  pl.BlockSpec((tm, 1, d), lambda i: (i, 0, 0)),
                  pl.BlockSpec((1, d), lambda i: (0, 0)),
                  pl.BlockSpec((1, d), lambda i: (0, 0))],
        out_specs=pl.BlockSpec((tm, d), lambda i: (i, 0)),
        out_shape=jax.ShapeDtypeStruct((t, d), F32),
        scratch_shapes=[pltpu.VMEM((tm, d), F32)],
        compiler_params=_params(1),
        name="add_ln",
    )(x, ffn3, ln_g, ln_b)


def _lru_in_kernel(x_ref, w_ref, y_ref, r_ref):
    d = y_ref.shape[1]
    yz = jnp.dot(x_ref[...].astype(BF16), w_ref[...], preferred_element_type=F32)
    y = yz[:, :d]
    y_ref[...] = 0.5 * y * (1.0 + jnp.tanh(math.sqrt(2.0 / math.pi) * (y + 0.044715 * (y * y * y))))
    r_ref[...] = yz[:, d:]


def _lru_in(x, w_in):
    t, d = x.shape
    n = w_in.shape[1] // 2
    tm = min(TOKEN_TILE, t)
    return pl.pallas_call(
        _lru_in_kernel,
        grid=(t // tm,),
        in_specs=[pl.BlockSpec((tm, d), lambda i: (i, 0)),
                  pl.BlockSpec((d, 2 * n), lambda i: (0, 0))],
        out_specs=[pl.BlockSpec((tm, n), lambda i: (i, 0)),
                   pl.BlockSpec((tm, n), lambda i: (i, 0))],
        out_shape=[jax.ShapeDtypeStruct((t, n), F32), jax.ShapeDtypeStruct((t, n), F32)],
        compiler_params=_params(1),
        name="lru_in",
    )(x, w_in)


def _lru_core_kernel(r_ref, y_ref, cw_ref, cb_ref, wgx_ref, wga_ref, bgx_ref, bga_ref, ap_ref,
                     hy_ref, rprev_ref, carry_ref, a_ref, u_ref):
    t = pl.program_id(0)
    tt, d = r_ref.shape
    n_blk, bw, _ = wgx_ref.shape
    cwid = cw_ref.shape[0]

    @pl.when(t == 0)
    def _():
        rprev_ref[...] = jnp.zeros_like(rprev_ref)
        carry_ref[...] = jnp.zeros_like(carry_ref)

    r = r_ref[...]
    rext = jnp.concatenate([rprev_ref[...], r], axis=0)
    rc = jnp.broadcast_to(cb_ref[...], (tt, d))
    for w in range(cwid):
        off = 8 - (cwid - 1) + w
        rc = rc + cw_ref[w:w + 1, :] * rext[off:off + tt, :]
    rprev_ref[...] = r[tt - 8:, :]

    rcb = rc.astype(BF16)
    gx = jnp.concatenate(
        [jnp.dot(rcb[:, n * bw:(n + 1) * bw], wgx_ref[n], preferred_element_type=F32)
         for n in range(n_blk)], axis=1)
    ga = jnp.concatenate(
        [jnp.dot(rcb[:, n * bw:(n + 1) * bw], wga_ref[n], preferred_element_type=F32)
         for n in range(n_blk)], axis=1)
    gate_x = _sigmoid(gx + bgx_ref[...])
    gate_a = _sigmoid(ga + bga_ref[...])
    z = -ap_ref[...]
    softplus = jnp.maximum(z, 0.0) + jnp.log(1.0 + jnp.exp(-jnp.abs(z)))
    log_a = -LRU_C * gate_a * softplus
    a_ref[...] = jnp.exp(log_a)
    u_ref[...] = jnp.sqrt(1.0 - jnp.exp(2.0 * log_a)) * (gate_x * rc)

    row = lax.broadcasted_iota(I32, (SCAN_ROWS, d), 0)

    def slab(s, carry):
        off = pl.multiple_of(s * SCAN_ROWS, SCAN_ROWS)
        a = a_ref[pl.ds(off, SCAN_ROWS), :]
        b = u_ref[pl.ds(off, SCAN_ROWS), :]
        step = 1
        while step < SCAN_ROWS:
            a_sh = pltpu.roll(a, step, 0)
            b_sh = pltpu.roll(b, step, 0)
            live = row >= step
            b = jnp.where(live, a * b_sh + b, b)
            a = jnp.where(live, a * a_sh, a)
            step *= 2
        h = b + a * carry
        u_ref[pl.ds(off, SCAN_ROWS), :] = h
        return h[SCAN_ROWS - 1:SCAN_ROWS, :]

    carry_ref[...] = lax.fori_loop(0, tt // SCAN_ROWS, slab, carry_ref[...])
    hy_ref[...] = (u_ref[...] * y_ref[...]).astype(hy_ref.dtype)


def _lru_core(r, y, conv_w, conv_b, wgx, wga, bgx, bga, a_param):
    t, d = r.shape
    tt = min(TOKEN_TILE, t)
    row = lambda i: (i, 0)
    fixed2 = lambda i: (0, 0)
    fixed3 = lambda i: (0, 0, 0)
    return pl.pallas_call(
        _lru_core_kernel,
        grid=(t // tt,),
        in_specs=[pl.BlockSpec((tt, d), row),
                  pl.BlockSpec((tt, d), row),
                  pl.BlockSpec(conv_w.shape, fixed2),
                  pl.BlockSpec((1, d), fixed2),
                  pl.BlockSpec(wgx.shape, fixed3),
                  pl.BlockSpec(wga.shape, fixed3),
                  pl.BlockSpec((1, d), fixed2),
                  pl.BlockSpec((1, d), fixed2),
                  pl.BlockSpec((1, d), fixed2)],
        out_specs=pl.BlockSpec((tt, d), row),
        out_shape=jax.ShapeDtypeStruct((t, d), BF16),
        scratch_shapes=[pltpu.VMEM((8, d), F32),
                        pltpu.VMEM((1, d), F32),
                        pltpu.VMEM((tt, d), F32),
                        pltpu.VMEM((tt, d), F32)],
        compiler_params=_params(1),
        name="lru_core",
    )(r, y, conv_w, conv_b, wgx, wga, bgx, bga, a_param)


def _moe_layout(counts, ids, rank, n_rows):
    tm = EXPERT_ROWS
    n_exp = counts.shape[0]
    padded = (counts + tm - 1) // tm * tm
    end = jnp.cumsum(padded).astype(I32)
    start = end - padded
    nused = (end[n_exp - 1] // tm).reshape(1)
    chunk_start = jnp.arange(n_rows // tm, dtype=I32) * tm
    chunk_start = jnp.minimum(chunk_start, end[n_exp - 1] - tm)
    ce = jnp.sum((end[None, :] <= chunk_start[:, None]).astype(I32), axis=1)
    ce = jnp.minimum(ce, n_exp - 1).astype(I32)
    dest = (start[ids] + rank).astype(I32).reshape(-1)
    return start, end, nused, ce, dest


def _moe(xn, ids, gates, rank, counts128, layer, wg, wu, wd, bg, bu, bd):
    t, d = xn.shape
    n_exp = wg.shape[1]
    n_rows = t * EXPERT_TOPK + n_exp * EXPERT_ROWS
    counts = counts128[:, 0].astype(I32)
    start, end, nused, ce, dest = _moe_layout(counts, ids, rank, n_rows)
    rows, src, grow = _dispatch(xn, dest, gates.reshape(-1), start, counts, end, nused, n_rows)
    return _experts(rows, ce, nused, src, grow, layer, wg, wu, wd, bg, bu, bd, t)


def kernel(x, rel_bias_table, w_qkv, w_attn_out, w_lru_in, lru_conv_w, lru_conv_b, w_lru_gate_x, b_lru_gate_x, w_lru_gate_a, b_lru_gate_a, lru_a_param, w_lru_out, w_router, b_router, w_exp_gate, b_exp_gate, w_exp_up, b_exp_up, w_exp_down, b_exp_down, ln_mix_g, ln_mix_b, ln_ffn_g, ln_ffn_b):
    bsz, t, d = x.shape
    depth = w_router.shape[0]
    n_mixers = 2
    alpha = (2 * depth) ** 0.25
    assert t % MOBA_BLOCK == 0 and t % TOKEN_TILE == 0

    biases = _attn_biases(rel_bias_table, t)
    w_qkv_b = w_qkv.astype(BF16)
    w_attn_out_b = w_attn_out.astype(BF16)
    w_lru_in_b = w_lru_in.astype(BF16)
    w_lru_out_b = w_lru_out.astype(BF16)
    wgx_b = w_lru_gate_x.astype(BF16)
    wga_b = w_lru_gate_a.astype(BF16)
    wg_b = w_exp_gate.astype(BF16)
    wu_b = w_exp_up.astype(BF16)
    wd_b = w_exp_down.astype(BF16)
    wr_t = jnp.swapaxes(w_router, 1, 2)
    wr_hi = wr_t.astype(BF16)
    wr_lo = (wr_t - wr_hi.astype(F32)).astype(BF16)

    bg4 = b_exp_gate[:, :, None, :]
    bu4 = b_exp_up[:, :, None, :]
    bd4 = b_exp_down[:, :, None, :]

    outs = []
    for b in range(bsz):
        xs = x[b]
        for i in range(depth):
            j = i // n_mixers
            if i % n_mixers == 0:
                qkv = _matmul(xs, w_qkv_b[j], BF16)
                a = _moba_attention(qkv, biases)
                w_out = w_attn_out_b[j]
            else:
                y, r = _lru_in(xs, w_lru_in_b[j])
                a = _lru_core(r, y, lru_conv_w[j], lru_conv_b[j][None], wgx_b[j], wga_b[j],
                              b_lru_gate_x[j][None], b_lru_gate_a[j][None], lru_a_param[j][None])
                w_out = w_lru_out_b[j]
            xn, ids, gates, rank, counts = _proj_ln_router(
                a, w_out, xs, ln_mix_g[i][None], ln_mix_b[i][None],
                wr_hi[i], wr_lo[i], b_router[i][:, None], alpha)
            ffn3 = _moe(xn, ids, gates, rank, counts, i, wg_b, wu_b, wd_b, bg4, bu4, bd4)
            xs = _add_ln(xn, ffn3, ln_ffn_g[i][None], ln_ffn_b[i][None], alpha)
        outs.append(xs)
    return jnp.stack(outs, axis=0)
```

```python
import functools
import math

import numpy as np
import jax
import jax.numpy as jnp
from jax import lax
from jax.experimental import pallas as pl
from jax.experimental.pallas import tpu as pltpu

F32 = jnp.float32
BF16 = jnp.bfloat16
I32 = jnp.int32

N_HEADS = 8
MOBA_BLOCK = 256
MOBA_TOPK = 3
NUM_BUCKETS = 32
REL_MAX_DISTANCE = 128
LRU_C = 8.0
EXPERT_TOPK = 4
SWIGLU_LIMIT = 7.0
SWIGLU_ALPHA = 1.702
LN_EPS = 1e-5

EXPERT_ROWS = 256
TOKEN_TILE = 512
SCAN_ROWS = 8
LANES = 128
HEAD_GROUP = 4
V7X_VMEM_LIMIT = 56 * 1024 * 1024

_NT = (((1,), (1,)), ((), ()))


def _params(n_axes, vmem=V7X_VMEM_LIMIT):
    return pltpu.CompilerParams(
        dimension_semantics=("arbitrary",) * n_axes, vmem_limit_bytes=vmem)


def _sigmoid(x):
    return 1.0 / (1.0 + jnp.exp(-x))


def _layer_norm(z, g, b):
    mu = jnp.mean(z, axis=-1, keepdims=True)
    zc = z - mu
    var = jnp.mean(zc * zc, axis=-1, keepdims=True)
    return zc * lax.rsqrt(var + LN_EPS) * g + b


def _matmul_kernel(x_ref, w_ref, o_ref):
    o_ref[...] = jnp.dot(x_ref[...].astype(BF16), w_ref[...],
                         preferred_element_type=F32).astype(o_ref.dtype)


def _matmul(x, w, out_dtype):
    m, k = x.shape
    n = w.shape[1]
    tm = min(TOKEN_TILE, m)
    return pl.pallas_call(
        _matmul_kernel,
        grid=(m // tm,),
        in_specs=[pl.BlockSpec((tm, k), lambda i: (i, 0)),
                  pl.BlockSpec((k, n), lambda i: (0, 0))],
        out_specs=pl.BlockSpec((tm, n), lambda i: (i, 0)),
        out_shape=jax.ShapeDtypeStruct((m, n), out_dtype),
        compiler_params=_params(1),
        name="dense_proj",
    )(x, w)


def _attn_kernel(q_ref, k_ref, v_ref, bself_ref, badj_ref, bfar_ref, o_ref,
                 kmean_ref, vt_ref, pen_ref, m_ref, l_ref, acc_ref, *, nblk, scale):
    blk = MOBA_BLOCK
    hg = bself_ref.shape[0]
    dh = q_ref.shape[1] // hg
    i = pl.program_id(1)

    def head(ref, h):
        return ref.at[:, h * dh:(h + 1) * dh]

    @pl.when(i == 0)
    def _():
        for h in range(hg):
            for j in range(nblk):
                kj = head(k_ref, h)[j * blk:(j + 1) * blk, :].astype(F32)
                kmean_ref[h, j:j + 1, :] = jnp.sum(kj, axis=0, keepdims=True) * (1.0 / blk)
                vt_ref[h, j] = head(v_ref, h)[j * blk:(j + 1) * blk, :].astype(F32).T.astype(BF16)

    qs = []
    for h in range(hg):
        q = head(q_ref, h)[...]

        km = kmean_ref[h]
        km_hi = km.astype(BF16)
        km_lo = (km - km_hi.astype(F32)).astype(BF16)
        gate = (lax.dot_general(km_hi, q, _NT, preferred_element_type=F32)
                + lax.dot_general(km_lo, q, _NT, preferred_element_type=F32))
        jj = lax.broadcasted_iota(I32, gate.shape, 0)
        past = jj < i
        g = jnp.where(past, gate, -jnp.inf)
        sel = jnp.zeros(gate.shape, jnp.bool_)
        for _ in range(min(MOBA_TOPK, nblk)):
            mx = jnp.max(g, axis=0, keepdims=True)
            jm = jnp.min(jnp.where(g == mx, jj, nblk), axis=0, keepdims=True)
            pick = jj == jm
            sel = jnp.logical_or(sel, pick)
            g = jnp.where(pick, -jnp.inf, g)
        pen_ref[h] = jnp.where(jnp.logical_and(sel, past), 0.0, -jnp.inf)
        qs.append((q.astype(F32) * scale).astype(BF16))

    def scores(h, j):
        kb = head(k_ref, h)[pl.ds(pl.multiple_of(j * blk, blk), blk), :]
        return lax.dot_general(kb, qs[h], _NT, preferred_element_type=F32)

    def fold(x, op):
        return op(x.reshape(blk // 8, 8, blk), axis=0)

    def adj_bias(h):
        return badj_ref[h] + pen_ref[h, pl.ds(jnp.maximum(i - 1, 0), 1), :]

    def far_bias(h, j):
        return bfar_ref[h, 0:1, 0:1] + pen_ref[h, pl.ds(j, 1), :]

    n_far = jnp.maximum(i - 1, 0)

    for h in range(hg):
        m_ref[h] = fold(scores(h, i) + bself_ref[h], jnp.max)

    @pl.when(i >= 1)
    def _():
        for h in range(hg):
            m_ref[h] = jnp.maximum(m_ref[h], fold(scores(h, i - 1) + adj_bias(h), jnp.max))

    def far_max(j, ms):
        return tuple(jnp.maximum(ms[h], fold(scores(h, j) + far_bias(h, j), jnp.max))
                     for h in range(hg))

    ms = lax.fori_loop(0, n_far, far_max, tuple(m_ref[h] for h in range(hg)))
    ms = [jnp.max(m, axis=0, keepdims=True) for m in ms]

    def accumulate(h, j, s):
        p = jnp.exp(s - ms[h])
        acc_ref[h] += jnp.dot(vt_ref[h, j], p.astype(BF16), preferred_element_type=F32)
        return fold(p, jnp.sum)

    for h in range(hg):
        acc_ref[h] = jnp.zeros(acc_ref.shape[1:], F32)
    ss = [scores(h, i) + bself_ref[h] for h in range(hg)]
    for h in range(hg):
        l_ref[h] = accumulate(h, i, ss[h])

    @pl.when(i >= 1)
    def _():
        ss = [scores(h, i - 1) + adj_bias(h) for h in range(hg)]
        for h in range(hg):
            l_ref[h] += accumulate(h, i - 1, ss[h])

    def far_acc(j, ls):
        ss = [scores(h, j) + far_bias(h, j) for h in range(hg)]
        return tuple(ls[h] + accumulate(h, j, ss[h]) for h in range(hg))

    ls = lax.fori_loop(0, n_far, far_acc, tuple(l_ref[h] for h in range(hg)))

    for h in range(hg):
        o = acc_ref[h] / jnp.sum(ls[h], axis=0, keepdims=True)
        head(o_ref, h)[...] = o.T.astype(o_ref.dtype)


def _t5_bucket_np(dist):
    max_exact = NUM_BUCKETS // 2
    n = dist.astype(np.float32)
    large = max_exact + (np.log(np.maximum(n, np.float32(1.0)) / np.float32(max_exact))
                         / np.float32(math.log(REL_MAX_DISTANCE / max_exact))
                         * np.float32(NUM_BUCKETS - max_exact)).astype(np.int32)
    large = np.minimum(large, NUM_BUCKETS - 1)
    return np.where(dist < max_exact, dist, large)


def _t5_bucket(dist):
    max_exact = NUM_BUCKETS // 2
    n = dist.astype(F32)
    large = max_exact + (jnp.log(jnp.maximum(n, 1.0) / max_exact)
                         / math.log(REL_MAX_DISTANCE / max_exact)
                         * (NUM_BUCKETS - max_exact)).astype(I32)
    large = jnp.minimum(large, NUM_BUCKETS - 1)
    return jnp.where(dist < max_exact, dist, large)


def _attn_biases(rel_bias_table, t):
    blk = MOBA_BLOCK
    far_buckets = _t5_bucket_np(np.arange(blk + 1, max(t, 2 * blk) + 1, dtype=np.int32))
    assert (far_buckets == NUM_BUCKETS - 1).all()
    bbd = rel_bias_table.astype(F32)[_t5_bucket(jnp.arange(2 * blk + 1, dtype=I32))].T
    kc = jnp.arange(blk, dtype=I32)[:, None]
    qr = jnp.arange(blk, dtype=I32)[None, :]
    bself = jnp.where(qr >= kc, bbd[:, jnp.maximum(qr - kc, 0)], -jnp.inf)
    badj = bbd[:, blk + qr - kc]
    bfar = jnp.broadcast_to(bbd[:, 2 * blk][:, None, None], (bbd.shape[0], 8, 128))
    return bself, badj, bfar


def _moba_attention(qkv, biases):
    t, d3 = qkv.shape
    d = d3 // 3
    dh = d // N_HEADS
    blk = MOBA_BLOCK
    nblk = t // blk
    bself, badj, bfar = biases
    hg = HEAD_GROUP
    n_grp = N_HEADS // hg
    kernel = functools.partial(_attn_kernel, nblk=nblk, scale=dh ** -0.5)
    return pl.pallas_call(
        kernel,
        grid=(n_grp, nblk),
        in_specs=[pl.BlockSpec((blk, hg * dh), lambda g, i: (i, g)),
                  pl.BlockSpec((t, hg * dh), lambda g, i: (0, n_grp + g)),
                  pl.BlockSpec((t, hg * dh), lambda g, i: (0, 2 * n_grp + g)),
                  pl.BlockSpec((hg, blk, blk), lambda g, i: (g, 0, 0)),
                  pl.BlockSpec((hg, blk, blk), lambda g, i: (g, 0, 0)),
                  pl.BlockSpec((hg, 8, LANES), lambda g, i: (g, 0, 0))],
        out_specs=pl.BlockSpec((blk, hg * dh), lambda g, i: (i, g)),
        out_shape=jax.ShapeDtypeStruct((t, d), BF16),
        scratch_shapes=[pltpu.VMEM((hg, nblk, dh), F32),
                        pltpu.VMEM((hg, nblk, dh, blk), BF16),
                        pltpu.VMEM((hg, nblk, blk), F32),
                        pltpu.VMEM((hg, 8, blk), F32),
                        pltpu.VMEM((hg, 8, blk), F32),
                        pltpu.VMEM((hg, dh, blk), F32)],
        compiler_params=_params(2),
        name="moba_attention",
    )(qkv, qkv, qkv, bself, badj, bfar)


def _proj_ln_router_kernel(a_ref, w_ref, x_ref, g_ref, b_ref, wrh_ref, wrl_ref, br_ref,
                           xn_ref, ids_ref, gates_ref, rank_ref, counts_ref, base_ref,
                           *, alpha, n_exp):
    t = pl.program_id(0)
    tm = a_ref.shape[0]

    @pl.when(t == 0)
    def _():
        base_ref[...] = jnp.zeros_like(base_ref)

    mix = jnp.dot(a_ref[...].astype(BF16), w_ref[...], preferred_element_type=F32)
    xn = _layer_norm(alpha * x_ref[...] + mix, g_ref[...], b_ref[...])
    xn_ref[...] = xn

    x_hi = xn.astype(BF16)
    x_lo = (xn - x_hi.astype(F32)).astype(BF16)
    wr_hi = wrh_ref[...]
    logits = (lax.dot_general(wr_hi, x_hi, _NT, preferred_element_type=F32)
              + lax.dot_general(wr_hi, x_lo, _NT, preferred_element_type=F32)
              + lax.dot_general(wrl_ref[...], x_hi, _NT, preferred_element_type=F32)
              + br_ref[...])

    e_iota = lax.broadcasted_iota(I32, logits.shape, 0)
    work = logits
    tops, picks = [], []
    for k in range(EXPERT_TOPK):
        mx = jnp.max(work, axis=0, keepdims=True)
        em = jnp.min(jnp.where(work == mx, e_iota, n_exp), axis=0, keepdims=True)
        pick = e_iota == em
        ids_ref[k:k + 1, :] = em
        tops.append(mx)
        picks.append(pick)
        work = jnp.where(pick, -jnp.inf, work)

    ex = [jnp.exp(tk - tops[0]) for tk in tops]
    den = ex[0]
    for e in ex[1:]:
        den = den + e
    for k in range(EXPERT_TOPK):
        gates_ref[k:k + 1, :] = ex[k] / den

    onehot = jnp.zeros(logits.shape, F32)
    for pick in picks:
        onehot = onehot + pick.astype(F32)
    earlier = (lax.broadcasted_iota(I32, (tm, tm), 0)
               < lax.broadcasted_iota(I32, (tm, tm), 1)).astype(BF16)
    pos = jnp.dot(onehot.astype(BF16), earlier, preferred_element_type=F32) + base_ref[:, 0:1]
    for k in range(EXPERT_TOPK):
        rank_ref[k:k + 1, :] = jnp.sum(jnp.where(picks[k], pos, 0.0), axis=0,
                                       keepdims=True).astype(I32)
    total = base_ref[...] + jnp.sum(onehot, axis=1, keepdims=True)
    base_ref[...] = total
    counts_ref[...] = total


def _proj_ln_router(a, w, x, ln_g, ln_b, wr_hi, wr_lo, br, alpha):
    t, d = x.shape
    n_exp = wr_hi.shape[0]
    tm = min(TOKEN_TILE, t)
    kernel = functools.partial(_proj_ln_router_kernel, alpha=alpha, n_exp=n_exp)
    row = lambda i: (i, 0)
    fixed = lambda i: (0, 0)
    col = lambda i: (0, i)
    return pl.pallas_call(
        kernel,
        grid=(t // tm,),
        in_specs=[pl.BlockSpec((tm, d), row),
                  pl.BlockSpec((d, d), fixed),
                  pl.BlockSpec((tm, d), row),
                  pl.BlockSpec((1, d), fixed),
                  pl.BlockSpec((1, d), fixed),
                  pl.BlockSpec((n_exp, d), fixed),
                  pl.BlockSpec((n_exp, d), fixed),
                  pl.BlockSpec((n_exp, 1), fixed)],
        out_specs=[pl.BlockSpec((tm, d), row),
                   pl.BlockSpec((EXPERT_TOPK, tm), col),
                   pl.BlockSpec((EXPERT_TOPK, tm), col),
                   pl.BlockSpec((EXPERT_TOPK, tm), col),
                   pl.BlockSpec((n_exp, 128), fixed)],
        out_shape=[jax.ShapeDtypeStruct((t, d), F32),
                   jax.ShapeDtypeStruct((EXPERT_TOPK, t), I32),
                   jax.ShapeDtypeStruct((EXPERT_TOPK, t), F32),
                   jax.ShapeDtypeStruct((EXPERT_TOPK, t), I32),
                   jax.ShapeDtypeStruct((n_exp, 128), F32)],
        scratch_shapes=[pltpu.VMEM((n_exp, 128), F32)],
        compiler_params=_params(1),
        name="proj_ln_router",
    )(a, w, x, ln_g, ln_b, wr_hi, wr_lo, br)


def _dispatch_kernel(ids_ref, rank_ref, gate_ref, start_ref, count_ref, end_ref, nused_ref,
                     x2_ref, rows_ref, src_ref, grow_ref, xg_ref, *, n_tok, n_exp):
    c = pl.program_id(0)
    tm = rows_ref.shape[0]
    n_sub = rows_ref.shape[1] // LANES
    n_rows = src_ref.shape[0]

    @pl.when(c == 0)
    def _():
        def fill(r, carry):
            src_ref[r] = n_tok
            grow_ref[r] = 0.0
            return carry

        for e in range(n_exp):
            lax.fori_loop(start_ref[e] + count_ref[e], end_ref[e], fill, 0)
        lax.fori_loop(end_ref[n_exp - 1], n_rows, fill, 0)

        def scatter(a, carry):
            for k in range(EXPERT_TOPK):
                idx = k * n_tok + a
                d = start_ref[ids_ref[idx]] + rank_ref[idx]
                src_ref[d] = a
                grow_ref[d] = gate_ref[idx]
            return carry

        lax.fori_loop(0, n_tok, scatter, 0, unroll=4)

    @pl.when(c < nused_ref[0])
    def _():
        def gather(r16, carry):
            for k in range(16):
                r = r16 * 16 + k
                tok = src_ref[c * tm + r]
                xg_ref[pl.ds(pl.multiple_of(r * n_sub, n_sub), n_sub), :] = (
                    x2_ref[pl.ds(pl.multiple_of(tok * n_sub, n_sub), n_sub), :])
            return carry

        lax.fori_loop(0, tm // 16, gather, 0)
        for s in range(n_sub):
            rows_ref[:, s * LANES:(s + 1) * LANES] = (
                xg_ref[pl.ds(s, tm, stride=n_sub), :].astype(rows_ref.dtype))

    @pl.when(c >= nused_ref[0])
    def _():
        rows_ref[...] = jnp.zeros_like(rows_ref)


def _dispatch(x, ids, rank, gates, start, count, end, nused, n_rows):
    t, d = x.shape
    tm = EXPERT_ROWS
    n_sub = d // LANES
    n_exp = start.shape[0]
    kernel = functools.partial(_dispatch_kernel, n_tok=t, n_exp=n_exp)
    smem = pl.BlockSpec(memory_space=pltpu.SMEM)
    grid_spec = pltpu.PrefetchScalarGridSpec(
        num_scalar_prefetch=7,
        grid=(n_rows // tm,),
        in_specs=[pl.BlockSpec(((t + 1) * n_sub, LANES), lambda c, *_: (0, 0))],
        out_specs=[pl.BlockSpec((tm, d), lambda c, *_: (c, 0)), smem, smem],
        scratch_shapes=[pltpu.VMEM((tm * n_sub, LANES), F32)],
    )
    return pl.pallas_call(
        kernel,
        grid_spec=grid_spec,
        out_shape=[jax.ShapeDtypeStruct((n_rows, d), BF16),
                   jax.ShapeDtypeStruct((n_rows,), I32),
                   jax.ShapeDtypeStruct((n_rows,), F32)],
        compiler_params=_params(1),
        name="moe_dispatch",
    )(ids, rank, gates, start, count, end, nused,
      jnp.pad(x, ((0, 1), (0, 0))).reshape((t + 1) * n_sub, LANES))


def _expert_kernel(ce_ref, nused_ref, src_ref, grow_ref,
                   x_ref, wg_ref, wu_ref, wd_ref, bg_ref, bu_ref, bd_ref,
                   out_ref, yf_ref):
    c = pl.program_id(0)
    tm, d = x_ref.shape
    n_sub = d // LANES

    def tile(ref, idx):
        return ref.at[pl.ds(pl.multiple_of(idx * n_sub, n_sub), n_sub), :]

    @pl.when(c == 0)
    def _():
        out_ref[...] = jnp.zeros_like(out_ref)

    @pl.when(c < nused_ref[0])
    def _():
        x = x_ref[...]
        g = jnp.dot(x, wg_ref[...], preferred_element_type=F32) + bg_ref[...]
        u = jnp.dot(x, wu_ref[...], preferred_element_type=F32) + bu_ref[...]
        g = jnp.minimum(g, SWIGLU_LIMIT)
        u = jnp.clip(u, -SWIGLU_LIMIT, SWIGLU_LIMIT)
        h = g * _sigmoid(SWIGLU_ALPHA * g) * (u + 1.0)
        y = jnp.dot(h.astype(BF16), wd_ref[...], preferred_element_type=F32) + bd_ref[...]
        for s in range(n_sub):
            yf_ref[pl.ds(s, tm, stride=n_sub), :] = y[:, s * LANES:(s + 1) * LANES]

        def combine(r8, carry):
            base = c * tm + r8 * 8
            toks = [src_ref[base + k] for k in range(8)]
            gs = [grow_ref[base + k] for k in range(8)]
            olds = [tile(out_ref, toks[k])[...] for k in range(8)]
            for k in range(8):
                tile(out_ref, toks[k])[...] = olds[k] + gs[k] * tile(yf_ref, r8 * 8 + k)[...]
            return carry

        lax.fori_loop(0, tm // 8, combine, 0)


def _experts(rows, ce, nused, src, grow, layer, wg, wu, wd, bg, bu, bd, n_tok):
    n_rows, d = rows.shape
    tm = EXPERT_ROWS
    n_sub = d // LANES
    dff = wg.shape[3]

    def row_map(c, ce_ref, nused_ref, *_):
        return (jnp.minimum(c, nused_ref[0] - 1), 0)

    def w_map(c, ce_ref, *_):
        return (layer, ce_ref[c], 0, 0)

    grid_spec = pltpu.PrefetchScalarGridSpec(
        num_scalar_prefetch=4,
        grid=(n_rows // tm,),
        in_specs=[pl.BlockSpec((tm, d), row_map),
                  pl.BlockSpec((None, None, d, dff), w_map),
                  pl.BlockSpec((None, None, d, dff), w_map),
                  pl.BlockSpec((None, None, dff, d), w_map),
                  pl.BlockSpec((None, None, 1, dff), w_map),
                  pl.BlockSpec((None, None, 1, dff), w_map),
                  pl.BlockSpec((None, None, 1, d), w_map)],
        out_specs=pl.BlockSpec(((n_tok + 1) * n_sub, LANES), lambda c, *_: (0, 0)),
        scratch_shapes=[pltpu.VMEM((tm * n_sub, LANES), F32)],
    )
    return pl.pallas_call(
        _expert_kernel,
        grid_spec=grid_spec,
        out_shape=jax.ShapeDtypeStruct(((n_tok + 1) * n_sub, LANES), F32),
        compiler_params=_params(1),
        name="moe_experts",
    )(ce, nused, src, grow, rows, wg, wu, wd, bg, bu, bd)


def _add_ln_kernel(x_ref, f_ref, g_ref, b_ref, o_ref, *, alpha):
    tm, d = x_ref.shape
    n_sub = d // LANES
    ffn = jnp.concatenate([f_ref[pl.ds(s, tm, stride=n_sub), :] for s in range(n_sub)], axis=1)
    o_ref[...] = _layer_norm(alpha * x_ref[...] + ffn, g_ref[...], b_ref[...])


def _add_ln(x, ffn_tiles, ln_g, ln_b, alpha):
    t, d = x.shape
    n_sub = d // LANES
    tm = min(TOKEN_TILE, t)
    return pl.pallas_call(
        functools.partial(_add_ln_kernel, alpha=alpha),
        grid=(t // tm,),
        in_specs=[pl.BlockSpec((tm, d), lambda i: (i, 0)),
                  pl.BlockSpec((tm * n_sub, LANES), lambda i: (i, 0)),
                  pl.BlockSpec((1, d), lambda i: (0, 0)),
                  pl.BlockSpec((1, d), lambda i: (0, 0))],
        out_specs=pl.BlockSpec((tm, d), lambda i: (i, 0)),
        out_shape=jax.ShapeDtypeStruct((t, d), F32),
        compiler_params=_params(1),
        name="add_ln",
    )(x, ffn_tiles, ln_g, ln_b)


def _lru_in_kernel(x_ref, w_ref, y_ref, r_ref):
    d = y_ref.shape[1]
    yz = jnp.dot(x_ref[...].astype(BF16), w_ref[...], preferred_element_type=F32)
    y = yz[:, :d]
    y_ref[...] = 0.5 * y * (1.0 + jnp.tanh(math.sqrt(2.0 / math.pi) * (y + 0.044715 * (y * y * y))))
    r_ref[...] = yz[:, d:]


def _lru_in(x, w_in):
    t, d = x.shape
    n = w_in.shape[1] // 2
    tm = min(TOKEN_TILE, t)
    return pl.pallas_call(
        _lru_in_kernel,
        grid=(t // tm,),
        in_specs=[pl.BlockSpec((tm, d), lambda i: (i, 0)),
                  pl.BlockSpec((d, 2 * n), lambda i: (0, 0))],
        out_specs=[pl.BlockSpec((tm, n), lambda i: (i, 0)),
                   pl.BlockSpec((tm, n), lambda i: (i, 0))],
        out_shape=[jax.ShapeDtypeStruct((t, n), F32), jax.ShapeDtypeStruct((t, n), F32)],
        compiler_params=_params(1),
        name="lru_in",
    )(x, w_in)


def _lru_core_kernel(r_ref, y_ref, cw_ref, cb_ref, wgx_ref, wga_ref, bgx_ref, bga_ref, ap_ref,
                     hy_ref, rprev_ref, carry_ref, a_ref, u_ref):
    t = pl.program_id(0)
    tt, d = r_ref.shape
    n_blk, bw, _ = wgx_ref.shape
    cwid = cw_ref.shape[0]

    @pl.when(t == 0)
    def _():
        rprev_ref[...] = jnp.zeros_like(rprev_ref)
        carry_ref[...] = jnp.zeros_like(carry_ref)

    r = r_ref[...]
    rext = jnp.concatenate([rprev_ref[...], r], axis=0)
    rc = jnp.broadcast_to(cb_ref[...], (tt, d))
    for w in range(cwid):
        off = 8 - (cwid - 1) + w
        rc = rc + cw_ref[w:w + 1, :] * rext[off:off + tt, :]
    rprev_ref[...] = r[tt - 8:, :]

    rcb = rc.astype(BF16)
    gx = jnp.concatenate(
        [jnp.dot(rcb[:, n * bw:(n + 1) * bw], wgx_ref[n], preferred_element_type=F32)
         for n in range(n_blk)], axis=1)
    ga = jnp.concatenate(
        [jnp.dot(rcb[:, n * bw:(n + 1) * bw], wga_ref[n], preferred_element_type=F32)
         for n in range(n_blk)], axis=1)
    gate_x = _sigmoid(gx + bgx_ref[...])
    gate_a = _sigmoid(ga + bga_ref[...])
    z = -ap_ref[...]
    softplus = jnp.maximum(z, 0.0) + jnp.log(1.0 + jnp.exp(-jnp.abs(z)))
    log_a = -LRU_C * gate_a * softplus
    a_ref[...] = jnp.exp(log_a)
    u_ref[...] = jnp.sqrt(1.0 - jnp.exp(2.0 * log_a)) * (gate_x * rc)

    row = lax.broadcasted_iota(I32, (SCAN_ROWS, d), 0)

    def slab(s, carry):
        off = pl.multiple_of(s * SCAN_ROWS, SCAN_ROWS)
        a = a_ref[pl.ds(off, SCAN_ROWS), :]
        b = u_ref[pl.ds(off, SCAN_ROWS), :]
        step = 1
        while step < SCAN_ROWS:
            a_sh = pltpu.roll(a, step, 0)
            b_sh = pltpu.roll(b, step, 0)
            live = row >= step
            b = jnp.where(live, a * b_sh + b, b)
            a = jnp.where(live, a * a_sh, a)
            step *= 2
        h = b + a * carry
        u_ref[pl.ds(off, SCAN_ROWS), :] = h
        return h[SCAN_ROWS - 1:SCAN_ROWS, :]

    carry_ref[...] = lax.fori_loop(0, tt // SCAN_ROWS, slab, carry_ref[...])
    hy_ref[...] = (u_ref[...] * y_ref[...]).astype(hy_ref.dtype)


def _lru_core(r, y, conv_w, conv_b, wgx, wga, bgx, bga, a_param):
    t, d = r.shape
    tt = min(TOKEN_TILE, t)
    row = lambda i: (i, 0)
    fixed2 = lambda i: (0, 0)
    fixed3 = lambda i: (0, 0, 0)
    return pl.pallas_call(
        _lru_core_kernel,
        grid=(t // tt,),
        in_specs=[pl.BlockSpec((tt, d), row),
                  pl.BlockSpec((tt, d), row),
                  pl.BlockSpec(conv_w.shape, fixed2),
                  pl.BlockSpec((1, d), fixed2),
                  pl.BlockSpec(wgx.shape, fixed3),
                  pl.BlockSpec(wga.shape, fixed3),
                  pl.BlockSpec((1, d), fixed2),
                  pl.BlockSpec((1, d), fixed2),
                  pl.BlockSpec((1, d), fixed2)],
        out_specs=pl.BlockSpec((tt, d), row),
        out_shape=jax.ShapeDtypeStruct((t, d), BF16),
        scratch_shapes=[pltpu.VMEM((8, d), F32),
                        pltpu.VMEM((1, d), F32),
                        pltpu.VMEM((tt, d), F32),
                        pltpu.VMEM((tt, d), F32)],
        compiler_params=_params(1),
        name="lru_core",
    )(r, y, conv_w, conv_b, wgx, wga, bgx, bga, a_param)


def _moe_layout(counts, n_rows):
    tm = EXPERT_ROWS
    n_exp = counts.shape[0]
    padded = (counts + tm - 1) // tm * tm
    end = jnp.cumsum(padded).astype(I32)
    start = end - padded
    nused = (end[n_exp - 1] // tm).reshape(1)
    chunk_start = jnp.arange(n_rows // tm, dtype=I32) * tm
    chunk_start = jnp.minimum(chunk_start, end[n_exp - 1] - tm)
    ce = jnp.sum((end[None, :] <= chunk_start[:, None]).astype(I32), axis=1)
    ce = jnp.minimum(ce, n_exp - 1).astype(I32)
    return start, end, nused, ce


def _moe(xn, ids, gates, rank, counts128, layer, wg, wu, wd, bg, bu, bd):
    t, d = xn.shape
    n_exp = wg.shape[1]
    n_rows = t * EXPERT_TOPK + n_exp * EXPERT_ROWS
    counts = counts128[:, 0].astype(I32)
    start, end, nused, ce = _moe_layout(counts, n_rows)
    rows, src, grow = _dispatch(xn, ids.reshape(-1), rank.reshape(-1), gates.reshape(-1),
                                start, counts, end, nused, n_rows)
    return _experts(rows, ce, nused, src, grow, layer, wg, wu, wd, bg, bu, bd, t)


def kernel(x, rel_bias_table, w_qkv, w_attn_out, w_lru_in, lru_conv_w, lru_conv_b, w_lru_gate_x, b_lru_gate_x, w_lru_gate_a, b_lru_gate_a, lru_a_param, w_lru_out, w_router, b_router, w_exp_gate, b_exp_gate, w_exp_up, b_exp_up, w_exp_down, b_exp_down, ln_mix_g, ln_mix_b, ln_ffn_g, ln_ffn_b):
    bsz, t, d = x.shape
    depth = w_router.shape[0]
    n_mixers = 2
    alpha = (2 * depth) ** 0.25
    assert t % MOBA_BLOCK == 0 and t % TOKEN_TILE == 0

    biases = _attn_biases(rel_bias_table, t)
    w_qkv_b = w_qkv.astype(BF16)
    w_attn_out_b = w_attn_out.astype(BF16)
    w_lru_in_b = w_lru_in.astype(BF16)
    w_lru_out_b = w_lru_out.astype(BF16)
    wgx_b = w_lru_gate_x.astype(BF16)
    wga_b = w_lru_gate_a.astype(BF16)
    wg_b = w_exp_gate.astype(BF16)
    wu_b = w_exp_up.astype(BF16)
    wd_b = w_exp_down.astype(BF16)
    wr_t = jnp.swapaxes(w_router, 1, 2)
    wr_hi = wr_t.astype(BF16)
    wr_lo = (wr_t - wr_hi.astype(F32)).astype(BF16)

    bg4 = b_exp_gate[:, :, None, :]
    bu4 = b_exp_up[:, :, None, :]
    bd4 = b_exp_down[:, :, None, :]

    outs = []
    for b in range(bsz):
        xs = x[b]
        for i in range(depth):
            j = i // n_mixers
            if i % n_mixers == 0:
                qkv = _matmul(xs, w_qkv_b[j], BF16)
                a = _moba_attention(qkv, biases)
                w_out = w_attn_out_b[j]
            else:
                y, r = _lru_in(xs, w_lru_in_b[j])
                a = _lru_core(r, y, lru_conv_w[j], lru_conv_b[j][None], wgx_b[j], wga_b[j],
                              b_lru_gate_x[j][None], b_lru_gate_a[j][None], lru_a_param[j][None])
                w_out = w_lru_out_b[j]
            xn, ids, gates, rank, counts = _proj_ln_router(
                a, w_out, xs, ln_mix_g[i][None], ln_mix_b[i][None],
                wr_hi[i], wr_lo[i], b_router[i][:, None], alpha)
            ffn3 = _moe(xn, ids, gates, rank, counts, i, wg_b, wu_b, wd_b, bg4, bu4, bd4)
            xs = _add_ln(xn, ffn3, ln_ffn_g[i][None], ln_ffn_b[i][None], alpha)
        outs.append(xs)
    return jnp.stack(outs, axis=0)
```

```python
import functools
import math

import numpy as np
import jax
import jax.numpy as jnp
from jax import lax
from jax.experimental import pallas as pl
from jax.experimental.pallas import tpu as pltpu

F32 = jnp.float32
BF16 = jnp.bfloat16
I32 = jnp.int32

N_HEADS = 8
MOBA_BLOCK = 256
MOBA_TOPK = 3
NUM_BUCKETS = 32
REL_MAX_DISTANCE = 128
LRU_C = 8.0
EXPERT_TOPK = 4
SWIGLU_LIMIT = 7.0
SWIGLU_ALPHA = 1.702
LN_EPS = 1e-5

EXPERT_ROWS = 256
TOKEN_TILE = 512
SCAN_ROWS = 8
LANES = 128
HEAD_GROUP = 8
V7X_VMEM_LIMIT = 56 * 1024 * 1024

_NT = (((1,), (1,)), ((), ()))


def _params(n_axes, vmem=V7X_VMEM_LIMIT):
    return pltpu.CompilerParams(
        dimension_semantics=("arbitrary",) * n_axes, vmem_limit_bytes=vmem)


def _sigmoid(x):
    return 1.0 / (1.0 + jnp.exp(-x))


def _layer_norm(z, g, b):
    mu = jnp.mean(z, axis=-1, keepdims=True)
    zc = z - mu
    var = jnp.mean(zc * zc, axis=-1, keepdims=True)
    return zc * lax.rsqrt(var + LN_EPS) * g + b


def _matmul_kernel(x_ref, w_ref, o_ref):
    o_ref[...] = jnp.dot(x_ref[...].astype(BF16), w_ref[...],
                         preferred_element_type=F32).astype(o_ref.dtype)


def _matmul(x, w, out_dtype):
    m, k = x.shape
    n = w.shape[1]
    tm = min(TOKEN_TILE, m)
    return pl.pallas_call(
        _matmul_kernel,
        grid=(m // tm,),
        in_specs=[pl.BlockSpec((tm, k), lambda i: (i, 0)),
                  pl.BlockSpec((k, n), lambda i: (0, 0))],
        out_specs=pl.BlockSpec((tm, n), lambda i: (i, 0)),
        out_shape=jax.ShapeDtypeStruct((m, n), out_dtype),
        compiler_params=_params(1),
        name="dense_proj",
    )(x, w)


def _attn_kernel(q_ref, k_ref, v_ref, bself_ref, badj_ref, bfar_ref, o_ref,
                 kmean_ref, vt_ref, pen_ref, m_ref, l_ref, acc_ref, *, nblk, scale):
    blk = MOBA_BLOCK
    hg = bself_ref.shape[0]
    dh = q_ref.shape[1] // hg
    i = pl.program_id(1)

    def head(ref, h):
        return ref.at[:, h * dh:(h + 1) * dh]

    @pl.when(i == 0)
    def _():
        for h in range(hg):
            for j in range(nblk):
                kj = head(k_ref, h)[j * blk:(j + 1) * blk, :].astype(F32)
                kmean_ref[h, j:j + 1, :] = jnp.sum(kj, axis=0, keepdims=True) * (1.0 / blk)
                vt_ref[h, j] = head(v_ref, h)[j * blk:(j + 1) * blk, :].astype(F32).T.astype(BF16)

    qs = []
    for h in range(hg):
        q = head(q_ref, h)[...]

        km = kmean_ref[h]
        km_hi = km.astype(BF16)
        km_lo = (km - km_hi.astype(F32)).astype(BF16)
        gate = (lax.dot_general(km_hi, q, _NT, preferred_element_type=F32)
                + lax.dot_general(km_lo, q, _NT, preferred_element_type=F32))
        jj = lax.broadcasted_iota(I32, gate.shape, 0)
        past = jj < i
        g = jnp.where(past, gate, -jnp.inf)
        sel = jnp.zeros(gate.shape, jnp.bool_)
        for _ in range(min(MOBA_TOPK, nblk)):
            mx = jnp.max(g, axis=0, keepdims=True)
            jm = jnp.min(jnp.where(g == mx, jj, nblk), axis=0, keepdims=True)
            pick = jj == jm
            sel = jnp.logical_or(sel, pick)
            g = jnp.where(pick, -jnp.inf, g)
        pen_ref[h] = jnp.where(jnp.logical_and(sel, past), 0.0, -jnp.inf)
        qs.append((q.astype(F32) * scale).astype(BF16))

    def scores(h, j):
        kb = head(k_ref, h)[pl.ds(pl.multiple_of(j * blk, blk), blk), :]
        return lax.dot_general(kb, qs[h], _NT, preferred_element_type=F32)

    def fold(x, op):
        return op(x.reshape(blk // 8, 8, blk), axis=0)

    def adj_bias(h):
        return badj_ref[h] + pen_ref[h, pl.ds(jnp.maximum(i - 1, 0), 1), :]

    def far_bias(h, j):
        return bfar_ref[h, 0:1, 0:1] + pen_ref[h, pl.ds(j, 1), :]

    n_far = jnp.maximum(i - 1, 0)

    for h in range(hg):
        m_ref[h] = fold(scores(h, i) + bself_ref[h], jnp.max)

    @pl.when(i >= 1)
    def _():
        for h in range(hg):
            m_ref[h] = jnp.maximum(m_ref[h], fold(scores(h, i - 1) + adj_bias(h), jnp.max))

    def far_max(j, ms):
        return tuple(jnp.maximum(ms[h], fold(scores(h, j) + far_bias(h, j), jnp.max))
                     for h in range(hg))

    ms = lax.fori_loop(0, n_far, far_max, tuple(m_ref[h] for h in range(hg)))
    ms = [jnp.max(m, axis=0, keepdims=True) for m in ms]

    def accumulate(h, j, s):
        p = jnp.exp(s - ms[h])
        acc_ref[h] += jnp.dot(vt_ref[h, j], p.astype(BF16), preferred_element_type=F32)
        return fold(p, jnp.sum)

    for h in range(hg):
        acc_ref[h] = jnp.zeros(acc_ref.shape[1:], F32)
    ss = [scores(h, i) + bself_ref[h] for h in range(hg)]
    for h in range(hg):
        l_ref[h] = accumulate(h, i, ss[h])

    @pl.when(i >= 1)
    def _():
        ss = [scores(h, i - 1) + adj_bias(h) for h in range(hg)]
        for h in range(hg):
            l_ref[h] += accumulate(h, i - 1, ss[h])

    def far_acc(j, ls):
        ss = [scores(h, j) + far_bias(h, j) for h in range(hg)]
        return tuple(ls[h] + accumulate(h, j, ss[h]) for h in range(hg))

    ls = lax.fori_loop(0, n_far, far_acc, tuple(l_ref[h] for h in range(hg)))

    for h in range(hg):
        o = acc_ref[h] / jnp.sum(ls[h], axis=0, keepdims=True)
        head(o_ref, h)[...] = o.T.astype(o_ref.dtype)


def _t5_bucket_np(dist):
    max_exact = NUM_BUCKETS // 2
    n = dist.astype(np.float32)
    large = max_exact + (np.log(np.maximum(n, np.float32(1.0)) / np.float32(max_exact))
                         / np.float32(math.log(REL_MAX_DISTANCE / max_exact))
                         * np.float32(NUM_BUCKETS - max_exact)).astype(np.int32)
    large = np.minimum(large, NUM_BUCKETS - 1)
    return np.where(dist < max_exact, dist, large)


def _t5_bucket(dist):
    max_exact = NUM_BUCKETS // 2
    n = dist.astype(F32)
    large = max_exact + (jnp.log(jnp.maximum(n, 1.0) / max_exact)
                         / math.log(REL_MAX_DISTANCE / max_exact)
                         * (NUM_BUCKETS - max_exact)).astype(I32)
    large = jnp.minimum(large, NUM_BUCKETS - 1)
    return jnp.where(dist < max_exact, dist, large)


def _attn_biases(rel_bias_table, t):
    blk = MOBA_BLOCK
    far_buckets = _t5_bucket_np(np.arange(blk + 1, max(t, 2 * blk) + 1, dtype=np.int32))
    assert (far_buckets == NUM_BUCKETS - 1).all()
    bbd = rel_bias_table.astype(F32)[_t5_bucket(jnp.arange(2 * blk + 1, dtype=I32))].T
    n_heads = bbd.shape[0]

    def toeplitz(w):
        rep = jnp.tile(w, (1, blk))[:, :blk * (2 * blk - 1)].reshape(n_heads, blk, 2 * blk - 1)
        return rep[:, :, blk - 1:]

    w_self = jnp.concatenate([jnp.full((n_heads, blk - 1), -jnp.inf, F32), bbd[:, :blk + 1]], axis=1)
    w_adj = jnp.concatenate([bbd[:, 1:2 * blk], bbd[:, :1]], axis=1)
    bself = toeplitz(w_self)
    badj = toeplitz(w_adj)
    bfar = jnp.broadcast_to(bbd[:, 2 * blk][:, None, None], (n_heads, 8, 128))
    return bself, badj, bfar


def _moba_attention(qkv, biases):
    t, d3 = qkv.shape
    d = d3 // 3
    dh = d // N_HEADS
    blk = MOBA_BLOCK
    nblk = t // blk
    bself, badj, bfar = biases
    hg = HEAD_GROUP
    n_grp = N_HEADS // hg
    kernel = functools.partial(_attn_kernel, nblk=nblk, scale=dh ** -0.5)
    return pl.pallas_call(
        kernel,
        grid=(n_grp, nblk),
        in_specs=[pl.BlockSpec((blk, hg * dh), lambda g, i: (i, g)),
                  pl.BlockSpec((t, hg * dh), lambda g, i: (0, n_grp + g)),
                  pl.BlockSpec((t, hg * dh), lambda g, i: (0, 2 * n_grp + g)),
                  pl.BlockSpec((hg, blk, blk), lambda g, i: (g, 0, 0)),
                  pl.BlockSpec((hg, blk, blk), lambda g, i: (g, 0, 0)),
                  pl.BlockSpec((hg, 8, LANES), lambda g, i: (g, 0, 0))],
        out_specs=pl.BlockSpec((blk, hg * dh), lambda g, i: (i, g)),
        out_shape=jax.ShapeDtypeStruct((t, d), BF16),
        scratch_shapes=[pltpu.VMEM((hg, nblk, dh), F32),
                        pltpu.VMEM((hg, nblk, dh, blk), BF16),
                        pltpu.VMEM((hg, nblk, blk), F32),
                        pltpu.VMEM((hg, 8, blk), F32),
                        pltpu.VMEM((hg, 8, blk), F32),
                        pltpu.VMEM((hg, dh, blk), F32)],
        compiler_params=_params(2),
        name="moba_attention",
    )(qkv, qkv, qkv, bself, badj, bfar)


def _proj_ln_router_kernel(a_ref, w_ref, x_ref, g_ref, b_ref, wrh_ref, wrl_ref, br_ref,
                           xn_ref, ids_ref, gates_ref, rank_ref, counts_ref, base_ref,
                           *, alpha, n_exp):
    t = pl.program_id(0)
    tm = a_ref.shape[0]

    @pl.when(t == 0)
    def _():
        base_ref[...] = jnp.zeros_like(base_ref)

    mix = jnp.dot(a_ref[...].astype(BF16), w_ref[...], preferred_element_type=F32)
    xn = _layer_norm(alpha * x_ref[...] + mix, g_ref[...], b_ref[...])
    n_sub = xn.shape[1] // LANES
    for s in range(n_sub):
        xn_ref[pl.ds(s, tm, stride=n_sub), :] = xn[:, s * LANES:(s + 1) * LANES]

    x_hi = xn.astype(BF16)
    x_lo = (xn - x_hi.astype(F32)).astype(BF16)
    wr_hi = wrh_ref[...]
    logits = (lax.dot_general(wr_hi, x_hi, _NT, preferred_element_type=F32)
              + lax.dot_general(wr_hi, x_lo, _NT, preferred_element_type=F32)
              + lax.dot_general(wrl_ref[...], x_hi, _NT, preferred_element_type=F32)
              + br_ref[...])

    e_iota = lax.broadcasted_iota(I32, logits.shape, 0)
    work = logits
    tops, picks = [], []
    for k in range(EXPERT_TOPK):
        mx = jnp.max(work, axis=0, keepdims=True)
        em = jnp.min(jnp.where(work == mx, e_iota, n_exp), axis=0, keepdims=True)
        pick = e_iota == em
        ids_ref[k:k + 1, :] = em
        tops.append(mx)
        picks.append(pick)
        work = jnp.where(pick, -jnp.inf, work)

    ex = [jnp.exp(tk - tops[0]) for tk in tops]
    den = ex[0]
    for e in ex[1:]:
        den = den + e
    for k in range(EXPERT_TOPK):
        gates_ref[k:k + 1, :] = ex[k] / den

    onehot = jnp.zeros(logits.shape, F32)
    for pick in picks:
        onehot = onehot + pick.astype(F32)
    earlier = (lax.broadcasted_iota(I32, (tm, tm), 0)
               < lax.broadcasted_iota(I32, (tm, tm), 1)).astype(BF16)
    pos = jnp.dot(onehot.astype(BF16), earlier, preferred_element_type=F32) + base_ref[:, 0:1]
    for k in range(EXPERT_TOPK):
        rank_ref[k:k + 1, :] = jnp.sum(jnp.where(picks[k], pos, 0.0), axis=0,
                                       keepdims=True).astype(I32)
    total = base_ref[...] + jnp.sum(onehot, axis=1, keepdims=True)
    base_ref[...] = total
    counts_ref[...] = total


def _proj_ln_router(a, w, x, ln_g, ln_b, wr_hi, wr_lo, br, alpha):
    t, d = x.shape
    n_exp = wr_hi.shape[0]
    tm = min(TOKEN_TILE, t)
    kernel = functools.partial(_proj_ln_router_kernel, alpha=alpha, n_exp=n_exp)
    row = lambda i: (i, 0)
    fixed = lambda i: (0, 0)
    col = lambda i: (0, i)
    return pl.pallas_call(
        kernel,
        grid=(t // tm,),
        in_specs=[pl.BlockSpec((tm, d), row),
                  pl.BlockSpec((d, d), fixed),
                  pl.BlockSpec((tm, d), row),
                  pl.BlockSpec((1, d), fixed),
                  pl.BlockSpec((1, d), fixed),
                  pl.BlockSpec((n_exp, d), fixed),
                  pl.BlockSpec((n_exp, d), fixed),
                  pl.BlockSpec((n_exp, 1), fixed)],
        out_specs=[pl.BlockSpec((tm * (d // LANES), LANES), row),
                   pl.BlockSpec((EXPERT_TOPK, tm), col),
                   pl.BlockSpec((EXPERT_TOPK, tm), col),
                   pl.BlockSpec((EXPERT_TOPK, tm), col),
                   pl.BlockSpec((n_exp, 128), fixed)],
        out_shape=[jax.ShapeDtypeStruct((t * (d // LANES), LANES), F32),
                   jax.ShapeDtypeStruct((EXPERT_TOPK, t), I32),
                   jax.ShapeDtypeStruct((EXPERT_TOPK, t), F32),
                   jax.ShapeDtypeStruct((EXPERT_TOPK, t), I32),
                   jax.ShapeDtypeStruct((n_exp, 128), F32)],
        scratch_shapes=[pltpu.VMEM((n_exp, 128), F32)],
        compiler_params=_params(1),
        name="proj_ln_router",
    )(a, w, x, ln_g, ln_b, wr_hi, wr_lo, br)


def _dispatch_kernel(dest_ref, gate_ref, start_ref, count_ref, end_ref, nused_ref,
                     x2_ref, rows_ref, src_ref, grow_ref, xg_ref, *, n_tok, n_exp):
    c = pl.program_id(0)
    tm = rows_ref.shape[0]
    n_sub = rows_ref.shape[1] // LANES
    n_rows = src_ref.shape[0]

    @pl.when(c == 0)
    def _():
        def fill(r, carry):
            src_ref[r] = 0
            grow_ref[r] = 0.0
            return carry

        for e in range(n_exp):
            lax.fori_loop(start_ref[e] + count_ref[e], end_ref[e], fill, 0)
        lax.fori_loop(end_ref[n_exp - 1], n_rows, fill, 0)

        def scatter(a, carry):
            for k in range(EXPERT_TOPK):
                d = dest_ref[k * n_tok + a]
                src_ref[d] = a
                grow_ref[d] = gate_ref[k * n_tok + a]
            return carry

        lax.fori_loop(0, n_tok, scatter, 0, unroll=8)

    @pl.when(c < nused_ref[0])
    def _():
        def gather(r16, carry):
            for k in range(16):
                r = r16 * 16 + k
                tok = src_ref[c * tm + r]
                xg_ref[pl.ds(pl.multiple_of(r * n_sub, n_sub), n_sub), :] = (
                    x2_ref[pl.ds(pl.multiple_of(tok * n_sub, n_sub), n_sub), :])
            return carry

        lax.fori_loop(0, tm // 16, gather, 0)
        for s in range(n_sub):
            rows_ref[:, s * LANES:(s + 1) * LANES] = (
                xg_ref[pl.ds(s, tm, stride=n_sub), :].astype(rows_ref.dtype))

    @pl.when(c >= nused_ref[0])
    def _():
        rows_ref[...] = jnp.zeros_like(rows_ref)


def _dispatch(x_tiles, d, dest, gates, start, count, end, nused, n_rows):
    n_sub = d // LANES
    t = x_tiles.shape[0] // n_sub
    tm = EXPERT_ROWS
    n_exp = start.shape[0]
    kernel = functools.partial(_dispatch_kernel, n_tok=t, n_exp=n_exp)
    smem = pl.BlockSpec(memory_space=pltpu.SMEM)
    grid_spec = pltpu.PrefetchScalarGridSpec(
        num_scalar_prefetch=6,
        grid=(n_rows // tm,),
        in_specs=[pl.BlockSpec((t * n_sub, LANES), lambda c, *_: (0, 0))],
        out_specs=[pl.BlockSpec((tm, d), lambda c, *_: (c, 0)), smem, smem],
        scratch_shapes=[pltpu.VMEM((tm * n_sub, LANES), F32)],
    )
    return pl.pallas_call(
        kernel,
        grid_spec=grid_spec,
        out_shape=[jax.ShapeDtypeStruct((n_rows, d), BF16),
                   jax.ShapeDtypeStruct((n_rows,), I32),
                   jax.ShapeDtypeStruct((n_rows,), F32)],
        compiler_params=_params(1),
        name="moe_dispatch",
    )(dest, gates, start, count, end, nused, x_tiles)


def _expert_kernel(ce_ref, nused_ref, valid_ref, src_ref, grow_ref,
                   x_ref, wg_ref, wu_ref, wd_ref, bg_ref, bu_ref, bd_ref,
                   out_ref, yf_ref):
    c = pl.program_id(0)
    tm, d = x_ref.shape
    n_sub = d // LANES

    def tile(ref, idx):
        return ref.at[pl.ds(pl.multiple_of(idx * n_sub, n_sub), n_sub), :]

    @pl.when(c == 0)
    def _():
        out_ref[...] = jnp.zeros_like(out_ref)

    @pl.when(c < nused_ref[0])
    def _():
        x = x_ref[...]
        g = jnp.dot(x, wg_ref[...], preferred_element_type=F32) + bg_ref[...]
        u = jnp.dot(x, wu_ref[...], preferred_element_type=F32) + bu_ref[...]
        g = jnp.minimum(g, SWIGLU_LIMIT)
        u = jnp.clip(u, -SWIGLU_LIMIT, SWIGLU_LIMIT)
        h = g * _sigmoid(SWIGLU_ALPHA * g) * (u + 1.0)
        y = jnp.dot(h.astype(BF16), wd_ref[...], preferred_element_type=F32) + bd_ref[...]
        for s in range(n_sub):
            yf_ref[pl.ds(s, tm, stride=n_sub), :] = y[:, s * LANES:(s + 1) * LANES]

        def combine(r8, carry):
            base = c * tm + r8 * 8
            toks = [src_ref[base + k] for k in range(8)]
            gs = [grow_ref[base + k] for k in range(8)]
            olds = [tile(out_ref, toks[k])[...] for k in range(8)]
            for k in range(8):
                tile(out_ref, toks[k])[...] = olds[k] + gs[k] * tile(yf_ref, r8 * 8 + k)[...]
            return carry

        def combine_one(r, carry):
            tok = src_ref[c * tm + r]
            tile(out_ref, tok)[...] += grow_ref[c * tm + r] * tile(yf_ref, r)[...]
            return carry

        n_real = valid_ref[c]
        lax.fori_loop(0, n_real // 8, combine, 0)
        lax.fori_loop(n_real // 8 * 8, n_real, combine_one, 0)


def _experts(rows, ce, nused, valid, src, grow, layer, wg, wu, wd, bg, bu, bd, n_tok):
    n_rows, d = rows.shape
    tm = EXPERT_ROWS
    n_sub = d // LANES
    dff = wg.shape[3]

    def row_map(c, ce_ref, nused_ref, *_):
        return (jnp.minimum(c, nused_ref[0] - 1), 0)

    def w_map(c, ce_ref, *_):
        return (layer, ce_ref[c], 0, 0)

    grid_spec = pltpu.PrefetchScalarGridSpec(
        num_scalar_prefetch=5,
        grid=(n_rows // tm,),
        in_specs=[pl.BlockSpec((tm, d), row_map),
                  pl.BlockSpec((None, None, d, dff), w_map),
                  pl.BlockSpec((None, None, d, dff), w_map),
                  pl.BlockSpec((None, None, dff, d), w_map),
                  pl.BlockSpec((None, None, 1, dff), w_map),
                  pl.BlockSpec((None, None, 1, dff), w_map),
                  pl.BlockSpec((None, None, 1, d), w_map)],
        out_specs=pl.BlockSpec((n_tok * n_sub, LANES), lambda c, *_: (0, 0)),
        scratch_shapes=[pltpu.VMEM((tm * n_sub, LANES), F32)],
    )
    return pl.pallas_call(
        _expert_kernel,
        grid_spec=grid_spec,
        out_shape=jax.ShapeDtypeStruct((n_tok * n_sub, LANES), F32),
        compiler_params=_params(1),
        name="moe_experts",
    )(ce, nused, valid, src, grow, rows, wg, wu, wd, bg, bu, bd)


def _add_ln_kernel(x_ref, f_ref, g_ref, b_ref, o_ref, *, alpha):
    tm, d = o_ref.shape
    n_sub = d // LANES
    z = jnp.concatenate(
        [alpha * x_ref[pl.ds(s, tm, stride=n_sub), :] + f_ref[pl.ds(s, tm, stride=n_sub), :]
         for s in range(n_sub)], axis=1)
    o_ref[...] = _layer_norm(z, g_ref[...], b_ref[...])


def _add_ln(x_tiles, ffn_tiles, t, d, ln_g, ln_b, alpha):
    n_sub = d // LANES
    tm = min(TOKEN_TILE, t)
    return pl.pallas_call(
        functools.partial(_add_ln_kernel, alpha=alpha),
        grid=(t // tm,),
        in_specs=[pl.BlockSpec((tm * n_sub, LANES), lambda i: (i, 0)),
                  pl.BlockSpec((tm * n_sub, LANES), lambda i: (i, 0)),
                  pl.BlockSpec((1, d), lambda i: (0, 0)),
                  pl.BlockSpec((1, d), lambda i: (0, 0))],
        out_specs=pl.BlockSpec((tm, d), lambda i: (i, 0)),
        out_shape=jax.ShapeDtypeStruct((t, d), F32),
        compiler_params=_params(1),
        name="add_ln",
    )(x_tiles, ffn_tiles, ln_g, ln_b)


def _lru_in_kernel(x_ref, w_ref, y_ref, r_ref):
    d = y_ref.shape[1]
    yz = jnp.dot(x_ref[...].astype(BF16), w_ref[...], preferred_element_type=F32)
    y = yz[:, :d]
    y_ref[...] = 0.5 * y * (1.0 + jnp.tanh(math.sqrt(2.0 / math.pi) * (y + 0.044715 * (y * y * y))))
    r_ref[...] = yz[:, d:]


def _lru_in(x, w_in):
    t, d = x.shape
    n = w_in.shape[1] // 2
    tm = min(TOKEN_TILE, t)
    return pl.pallas_call(
        _lru_in_kernel,
        grid=(t // tm,),
        in_specs=[pl.BlockSpec((tm, d), lambda i: (i, 0)),
                  pl.BlockSpec((d, 2 * n), lambda i: (0, 0))],
        out_specs=[pl.BlockSpec((tm, n), lambda i: (i, 0)),
                   pl.BlockSpec((tm, n), lambda i: (i, 0))],
        out_shape=[jax.ShapeDtypeStruct((t, n), F32), jax.ShapeDtypeStruct((t, n), F32)],
        compiler_params=_params(1),
        name="lru_in",
    )(x, w_in)


def _lru_core_kernel(r_ref, y_ref, cw_ref, cb_ref, wgx_ref, wga_ref, bgx_ref, bga_ref, ap_ref,
                     hy_ref, rprev_ref, carry_ref, a_ref, u_ref):
    t = pl.program_id(0)
    tt, d = r_ref.shape
    n_blk, bw, _ = wgx_ref.shape
    cwid = cw_ref.shape[0]

    @pl.when(t == 0)
    def _():
        rprev_ref[...] = jnp.zeros_like(rprev_ref)
        carry_ref[...] = jnp.zeros_like(carry_ref)

    r = r_ref[...]
    rext = jnp.concatenate([rprev_ref[...], r], axis=0)
    rc = jnp.broadcast_to(cb_ref[...], (tt, d))
    for w in range(cwid):
        off = 8 - (cwid - 1) + w
        rc = rc + cw_ref[w:w + 1, :] * rext[off:off + tt, :]
    rprev_ref[...] = r[tt - 8:, :]

    rcb = rc.astype(BF16)
    gx = jnp.concatenate(
        [jnp.dot(rcb[:, n * bw:(n + 1) * bw], wgx_ref[n], preferred_element_type=F32)
         for n in range(n_blk)], axis=1)
    ga = jnp.concatenate(
        [jnp.dot(rcb[:, n * bw:(n + 1) * bw], wga_ref[n], preferred_element_type=F32)
         for n in range(n_blk)], axis=1)
    gate_x = _sigmoid(gx + bgx_ref[...])
    gate_a = _sigmoid(ga + bga_ref[...])
    z = -ap_ref[...]
    softplus = jnp.maximum(z, 0.0) + jnp.log(1.0 + jnp.exp(-jnp.abs(z)))
    log_a = -LRU_C * gate_a * softplus
    a_ref[...] = jnp.exp(log_a)
    u_ref[...] = jnp.sqrt(1.0 - jnp.exp(2.0 * log_a)) * (gate_x * rc)

    row = lax.broadcasted_iota(I32, (SCAN_ROWS, d), 0)

    def slab(s, carry):
        off = pl.multiple_of(s * SCAN_ROWS, SCAN_ROWS)
        a = a_ref[pl.ds(off, SCAN_ROWS), :]
        b = u_ref[pl.ds(off, SCAN_ROWS), :]
        step = 1
        while step < SCAN_ROWS:
            a_sh = pltpu.roll(a, step, 0)
            b_sh = pltpu.roll(b, step, 0)
            live = row >= step
            b = jnp.where(live, a * b_sh + b, b)
            a = jnp.where(live, a * a_sh, a)
            step *= 2
        h = b + a * carry
        u_ref[pl.ds(off, SCAN_ROWS), :] = h
        return h[SCAN_ROWS - 1:SCAN_ROWS, :]

    carry_ref[...] = lax.fori_loop(0, tt // SCAN_ROWS, slab, carry_ref[...])
    hy_ref[...] = (u_ref[...] * y_ref[...]).astype(hy_ref.dtype)


def _lru_core(r, y, conv_w, conv_b, wgx, wga, bgx, bga, a_param):
    t, d = r.shape
    tt = min(TOKEN_TILE, t)
    row = lambda i: (i, 0)
    fixed2 = lambda i: (0, 0)
    fixed3 = lambda i: (0, 0, 0)
    return pl.pallas_call(
        _lru_core_kernel,
        grid=(t // tt,),
        in_specs=[pl.BlockSpec((tt, d), row),
                  pl.BlockSpec((tt, d), row),
                  pl.BlockSpec(conv_w.shape, fixed2),
                  pl.BlockSpec((1, d), fixed2),
                  pl.BlockSpec(wgx.shape, fixed3),
                  pl.BlockSpec(wga.shape, fixed3),
                  pl.BlockSpec((1, d), fixed2),
                  pl.BlockSpec((1, d), fixed2),
                  pl.BlockSpec((1, d), fixed2)],
        out_specs=pl.BlockSpec((tt, d), row),
        out_shape=jax.ShapeDtypeStruct((t, d), BF16),
        scratch_shapes=[pltpu.VMEM((8, d), F32),
                        pltpu.VMEM((1, d), F32),
                        pltpu.VMEM((tt, d), F32),
                        pltpu.VMEM((tt, d), F32)],
        compiler_params=_params(1),
        name="lru_core",
    )(r, y, conv_w, conv_b, wgx, wga, bgx, bga, a_param)


def _moe_layout(counts, n_rows):
    tm = EXPERT_ROWS
    n_exp = counts.shape[0]
    padded = (counts + tm - 1) // tm * tm
    end = jnp.cumsum(padded).astype(I32)
    start = end - padded
    nused = (end[n_exp - 1] // tm).reshape(1)
    chunk_start = jnp.arange(n_rows // tm, dtype=I32) * tm
    chunk_start = jnp.minimum(chunk_start, end[n_exp - 1] - tm)
    ce = jnp.sum((end[None, :] <= chunk_start[:, None]).astype(I32), axis=1)
    ce = jnp.minimum(ce, n_exp - 1).astype(I32)
    all_chunks = jnp.arange(n_rows // tm, dtype=I32)
    valid = jnp.clip((start + counts)[ce] - all_chunks * tm, 0, tm)
    valid = jnp.where(all_chunks < nused[0], valid, 0).astype(I32)
    return start, end, nused, ce, valid


def _moe(xn_tiles, t, d, ids, gates, rank, counts128, layer, wg, wu, wd, bg, bu, bd):
    n_exp = wg.shape[1]
    n_rows = t * EXPERT_TOPK + n_exp * EXPERT_ROWS
    counts = counts128[:, 0].astype(I32)
    start, end, nused, ce, valid = _moe_layout(counts, n_rows)
    experts = jnp.arange(n_exp, dtype=I32)
    dest = rank + jnp.sum(jnp.where(ids[:, :, None] == experts, start, 0), axis=-1)
    rows, src, grow = _dispatch(xn_tiles, d, dest.reshape(-1).astype(I32), gates.reshape(-1),
                                start, counts, end, nused, n_rows)
    return _experts(rows, ce, nused, valid, src, grow, layer, wg, wu, wd, bg, bu, bd, t)


def kernel(x, rel_bias_table, w_qkv, w_attn_out, w_lru_in, lru_conv_w, lru_conv_b, w_lru_gate_x, b_lru_gate_x, w_lru_gate_a, b_lru_gate_a, lru_a_param, w_lru_out, w_router, b_router, w_exp_gate, b_exp_gate, w_exp_up, b_exp_up, w_exp_down, b_exp_down, ln_mix_g, ln_mix_b, ln_ffn_g, ln_ffn_b):
    bsz, t, d = x.shape
    depth = w_router.shape[0]
    n_mixers = 2
    alpha = (2 * depth) ** 0.25
    assert t % MOBA_BLOCK == 0 and t % TOKEN_TILE == 0

    biases = _attn_biases(rel_bias_table, t)
    w_qkv_b = w_qkv.astype(BF16)
    w_attn_out_b = w_attn_out.astype(BF16)
    w_lru_in_b = w_lru_in.astype(BF16)
    w_lru_out_b = w_lru_out.astype(BF16)
    wgx_b = w_lru_gate_x.astype(BF16)
    wga_b = w_lru_gate_a.astype(BF16)
    wg_b = w_exp_gate.astype(BF16)
    wu_b = w_exp_up.astype(BF16)
    wd_b = w_exp_down.astype(BF16)
    wr_t = jnp.swapaxes(w_router, 1, 2)
    wr_hi = wr_t.astype(BF16)
    wr_lo = (wr_t - wr_hi.astype(F32)).astype(BF16)

    bg4 = b_exp_gate[:, :, None, :]
    bu4 = b_exp_up[:, :, None, :]
    bd4 = b_exp_down[:, :, None, :]

    outs = []
    for b in range(bsz):
        xs = x[b]
        for i in range(depth):
            j = i // n_mixers
            if i % n_mixers == 0:
                qkv = _matmul(xs, w_qkv_b[j], BF16)
                a = _moba_attention(qkv, biases)
                w_out = w_attn_out_b[j]
            else:
                y, r = _lru_in(xs, w_lru_in_b[j])
                a = _lru_core(r, y, lru_conv_w[j], lru_conv_b[j][None], wgx_b[j], wga_b[j],
                              b_lru_gate_x[j][None], b_lru_gate_a[j][None], lru_a_param[j][None])
                w_out = w_lru_out_b[j]
            xn, ids, gates, rank, counts = _proj_ln_router(
                a, w_out, xs, ln_mix_g[i][None], ln_mix_b[i][None],
                wr_hi[i], wr_lo[i], b_router[i][:, None], alpha)
            ffn = _moe(xn, t, d, ids, gates, rank, counts, i, wg_b, wu_b, wd_b, bg4, bu4, bd4)
            xs = _add_ln(xn, ffn, t, d, ln_ffn_g[i][None], ln_ffn_b[i][None], alpha)
        outs.append(xs)
    return jnp.stack(outs, axis=0)
```

```python
import functools
import math

import numpy as np
import jax
import jax.numpy as jnp
from jax import lax
from jax.experimental import pallas as pl
from jax.experimental.pallas import tpu as pltpu

F32 = jnp.float32
BF16 = jnp.bfloat16
I32 = jnp.int32

N_HEADS = 8
MOBA_BLOCK = 256
MOBA_TOPK = 3
NUM_BUCKETS = 32
REL_MAX_DISTANCE = 128
LRU_C = 8.0
EXPERT_TOPK = 4
SWIGLU_LIMIT = 7.0
SWIGLU_ALPHA = 1.702
LN_EPS = 1e-5

EXPERT_ROWS = 256
TOKEN_TILE = 512
SCAN_ROWS = 8
LANES = 128
HEAD_GROUP = 8
V7X_VMEM_LIMIT = 56 * 1024 * 1024

_NT = (((1,), (1,)), ((), ()))


def _params(n_axes, vmem=V7X_VMEM_LIMIT):
    return pltpu.CompilerParams(
        dimension_semantics=("arbitrary",) * n_axes, vmem_limit_bytes=vmem)


def _sigmoid(x):
    return 1.0 / (1.0 + jnp.exp(-x))


def _layer_norm(z, g, b):
    mu = jnp.mean(z, axis=-1, keepdims=True)
    zc = z - mu
    var = jnp.mean(zc * zc, axis=-1, keepdims=True)
    return zc * lax.rsqrt(var + LN_EPS) * g + b


def _matmul_kernel(x_ref, w_ref, o_ref):
    o_ref[...] = jnp.dot(x_ref[...].astype(BF16), w_ref[...],
                         preferred_element_type=F32).astype(o_ref.dtype)


def _matmul(x, w, out_dtype):
    m, k = x.shape
    n = w.shape[1]
    tm = min(TOKEN_TILE, m)
    return pl.pallas_call(
        _matmul_kernel,
        grid=(m // tm,),
        in_specs=[pl.BlockSpec((tm, k), lambda i: (i, 0)),
                  pl.BlockSpec((k, n), lambda i: (0, 0))],
        out_specs=pl.BlockSpec((tm, n), lambda i: (i, 0)),
        out_shape=jax.ShapeDtypeStruct((m, n), out_dtype),
        compiler_params=_params(1),
        name="dense_proj",
    )(x, w)


def _attn_kernel(q_ref, k_ref, v_ref, bself_ref, badj_ref, bfar_ref, o_ref,
                 kmean_ref, vt_ref, pen_ref, m_ref, l_ref, acc_ref, *, nblk, scale):
    blk = MOBA_BLOCK
    hg = bself_ref.shape[0]
    dh = q_ref.shape[1] // hg
    i = pl.program_id(1)

    def head(ref, h):
        return ref.at[:, h * dh:(h + 1) * dh]

    @pl.when(i == 0)
    def _():
        for h in range(hg):
            for j in range(nblk):
                kj = head(k_ref, h)[j * blk:(j + 1) * blk, :].astype(F32)
                kmean_ref[h, j:j + 1, :] = jnp.sum(kj, axis=0, keepdims=True) * (1.0 / blk)
                vt_ref[h, j] = head(v_ref, h)[j * blk:(j + 1) * blk, :].astype(F32).T.astype(BF16)

    qs = []
    for h in range(hg):
        q = head(q_ref, h)[...]

        km = kmean_ref[h]
        km_hi = km.astype(BF16)
        km_lo = (km - km_hi.astype(F32)).astype(BF16)
        gate = (lax.dot_general(km_hi, q, _NT, preferred_element_type=F32)
                + lax.dot_general(km_lo, q, _NT, preferred_element_type=F32))
        jj = lax.broadcasted_iota(I32, gate.shape, 0)
        past = jj < i
        g = jnp.where(past, gate, -jnp.inf)
        sel = jnp.zeros(gate.shape, jnp.bool_)
        for _ in range(min(MOBA_TOPK, nblk)):
            mx = jnp.max(g, axis=0, keepdims=True)
            jm = jnp.min(jnp.where(g == mx, jj, nblk), axis=0, keepdims=True)
            pick = jj == jm
            sel = jnp.logical_or(sel, pick)
            g = jnp.where(pick, -jnp.inf, g)
        pen_ref[h] = jnp.where(jnp.logical_and(sel, past), 0.0, -jnp.inf)
        qs.append((q.astype(F32) * scale).astype(BF16))

    def scores(h, j):
        kb = head(k_ref, h)[pl.ds(pl.multiple_of(j * blk, blk), blk), :]
        return lax.dot_general(kb, qs[h], _NT, preferred_element_type=F32)

    def fold(x, op):
        return op(x.reshape(blk // 8, 8, blk), axis=0)

    def adj_bias(h):
        return badj_ref[h] + pen_ref[h, pl.ds(jnp.maximum(i - 1, 0), 1), :]

    def far_bias(h, j):
        return bfar_ref[h, 0:1, 0:1] + pen_ref[h, pl.ds(j, 1), :]

    n_far = jnp.maximum(i - 1, 0)

    for h in range(hg):
        m_ref[h] = fold(scores(h, i) + bself_ref[h], jnp.max)

    @pl.when(i >= 1)
    def _():
        for h in range(hg):
            m_ref[h] = jnp.maximum(m_ref[h], fold(scores(h, i - 1) + adj_bias(h), jnp.max))

    def far_max(j, ms):
        return tuple(jnp.maximum(ms[h], fold(scores(h, j) + far_bias(h, j), jnp.max))
                     for h in range(hg))

    ms = lax.fori_loop(0, n_far, far_max, tuple(m_ref[h] for h in range(hg)))
    ms = [jnp.max(m, axis=0, keepdims=True) for m in ms]

    def accumulate(h, j, s):
        p = jnp.exp(s - ms[h])
        acc_ref[h] += jnp.dot(vt_ref[h, j], p.astype(BF16), preferred_element_type=F32)
        return fold(p, jnp.sum)

    for h in range(hg):
        acc_ref[h] = jnp.zeros(acc_ref.shape[1:], F32)
    ss = [scores(h, i) + bself_ref[h] for h in range(hg)]
    for h in range(hg):
        l_ref[h] = accumulate(h, i, ss[h])

    @pl.when(i >= 1)
    def _():
        ss = [scores(h, i - 1) + adj_bias(h) for h in range(hg)]
        for h in range(hg):
            l_ref[h] += accumulate(h, i - 1, ss[h])

    def far_acc(j, ls):
        ss = [scores(h, j) + far_bias(h, j) for h in range(hg)]
        return tuple(ls[h] + accumulate(h, j, ss[h]) for h in range(hg))

    ls = lax.fori_loop(0, n_far, far_acc, tuple(l_ref[h] for h in range(hg)))

    for h in range(hg):
        o = acc_ref[h] / jnp.sum(ls[h], axis=0, keepdims=True)
        head(o_ref, h)[...] = o.T.astype(o_ref.dtype)


def _t5_bucket_np(dist):
    max_exact = NUM_BUCKETS // 2
    n = dist.astype(np.float32)
    large = max_exact + (np.log(np.maximum(n, np.float32(1.0)) / np.float32(max_exact))
                         / np.float32(math.log(REL_MAX_DISTANCE / max_exact))
                         * np.float32(NUM_BUCKETS - max_exact)).astype(np.int32)
    large = np.minimum(large, NUM_BUCKETS - 1)
    return np.where(dist < max_exact, dist, large)


def _t5_bucket(dist):
    max_exact = NUM_BUCKETS // 2
    n = dist.astype(F32)
    large = max_exact + (jnp.log(jnp.maximum(n, 1.0) / max_exact)
                         / math.log(REL_MAX_DISTANCE / max_exact)
                         * (NUM_BUCKETS - max_exact)).astype(I32)
    large = jnp.minimum(large, NUM_BUCKETS - 1)
    return jnp.where(dist < max_exact, dist, large)


def _attn_biases(rel_bias_table, t):
    blk = MOBA_BLOCK
    far_buckets = _t5_bucket_np(np.arange(blk + 1, max(t, 2 * blk) + 1, dtype=np.int32))
    assert (far_buckets == NUM_BUCKETS - 1).all()
    bbd = rel_bias_table.astype(F32)[_t5_bucket(jnp.arange(2 * blk + 1, dtype=I32))].T
    n_heads = bbd.shape[0]

    def toeplitz(w):
        rep = jnp.tile(w, (1, blk))[:, :blk * (2 * blk - 1)].reshape(n_heads, blk, 2 * blk - 1)
        return rep[:, :, blk - 1:]

    w_self = jnp.concatenate([jnp.full((n_heads, blk - 1), -jnp.inf, F32), bbd[:, :blk + 1]], axis=1)
    w_adj = jnp.concatenate([bbd[:, 1:2 * blk], bbd[:, :1]], axis=1)
    bself = toeplitz(w_self)
    badj = toeplitz(w_adj)
    bfar = jnp.broadcast_to(bbd[:, 2 * blk][:, None, None], (n_heads, 8, 128))
    return bself, badj, bfar


def _moba_attention(qkv, biases):
    t, d3 = qkv.shape
    d = d3 // 3
    dh = d // N_HEADS
    blk = MOBA_BLOCK
    nblk = t // blk
    bself, badj, bfar = biases
    hg = HEAD_GROUP
    n_grp = N_HEADS // hg
    kernel = functools.partial(_attn_kernel, nblk=nblk, scale=dh ** -0.5)
    return pl.pallas_call(
        kernel,
        grid=(n_grp, nblk),
        in_specs=[pl.BlockSpec((blk, hg * dh), lambda g, i: (i, g)),
                  pl.BlockSpec((t, hg * dh), lambda g, i: (0, n_grp + g)),
                  pl.BlockSpec((t, hg * dh), lambda g, i: (0, 2 * n_grp + g)),
                  pl.BlockSpec((hg, blk, blk), lambda g, i: (g, 0, 0)),
                  pl.BlockSpec((hg, blk, blk), lambda g, i: (g, 0, 0)),
                  pl.BlockSpec((hg, 8, LANES), lambda g, i: (g, 0, 0))],
        out_specs=pl.BlockSpec((blk, hg * dh), lambda g, i: (i, g)),
        out_shape=jax.ShapeDtypeStruct((t, d), BF16),
        scratch_shapes=[pltpu.VMEM((hg, nblk, dh), F32),
                        pltpu.VMEM((hg, nblk, dh, blk), BF16),
                        pltpu.VMEM((hg, nblk, blk), F32),
                        pltpu.VMEM((hg, 8, blk), F32),
                        pltpu.VMEM((hg, 8, blk), F32),
                        pltpu.VMEM((hg, dh, blk), F32)],
        compiler_params=_params(2),
        name="moba_attention",
    )(qkv, qkv, qkv, bself, badj, bfar)


def _proj_ln_router_kernel(a_ref, w_ref, x_ref, g_ref, b_ref, wrh_ref, wrl_ref, br_ref,
                           xn_ref, ids_ref, gates_ref, rank_ref, counts_ref, base_ref,
                           *, alpha, n_exp):
    t = pl.program_id(0)
    tm = a_ref.shape[0]

    @pl.when(t == 0)
    def _():
        base_ref[...] = jnp.zeros_like(base_ref)

    mix = jnp.dot(a_ref[...].astype(BF16), w_ref[...], preferred_element_type=F32)
    xn = _layer_norm(alpha * x_ref[...] + mix, g_ref[...], b_ref[...])
    n_sub = xn.shape[1] // LANES
    for s in range(n_sub):
        xn_ref[pl.ds(s, tm, stride=n_sub), :] = xn[:, s * LANES:(s + 1) * LANES]

    x_hi = xn.astype(BF16)
    x_lo = (xn - x_hi.astype(F32)).astype(BF16)
    wr_hi = wrh_ref[...]
    logits = (lax.dot_general(wr_hi, x_hi, _NT, preferred_element_type=F32)
              + lax.dot_general(wr_hi, x_lo, _NT, preferred_element_type=F32)
              + lax.dot_general(wrl_ref[...], x_hi, _NT, preferred_element_type=F32)
              + br_ref[...])

    e_iota = lax.broadcasted_iota(I32, logits.shape, 0)
    work = logits
    tops, picks = [], []
    for k in range(EXPERT_TOPK):
        mx = jnp.max(work, axis=0, keepdims=True)
        em = jnp.min(jnp.where(work == mx, e_iota, n_exp), axis=0, keepdims=True)
        pick = e_iota == em
        ids_ref[k:k + 1, :] = em
        tops.append(mx)
        picks.append(pick)
        work = jnp.where(pick, -jnp.inf, work)

    ex = [jnp.exp(tk - tops[0]) for tk in tops]
    den = ex[0]
    for e in ex[1:]:
        den = den + e
    for k in range(EXPERT_TOPK):
        gates_ref[k:k + 1, :] = ex[k] / den

    onehot = jnp.zeros(logits.shape, F32)
    for pick in picks:
        onehot = onehot + pick.astype(F32)
    earlier = (lax.broadcasted_iota(I32, (tm, tm), 0)
               < lax.broadcasted_iota(I32, (tm, tm), 1)).astype(BF16)
    pos = jnp.dot(onehot.astype(BF16), earlier, preferred_element_type=F32) + base_ref[:, 0:1]
    for k in range(EXPERT_TOPK):
        rank_ref[k:k + 1, :] = jnp.sum(jnp.where(picks[k], pos, 0.0), axis=0,
                                       keepdims=True).astype(I32)
    total = base_ref[...] + jnp.sum(onehot, axis=1, keepdims=True)
    base_ref[...] = total
    counts_ref[...] = total


def _proj_ln_router(a, w, x, ln_g, ln_b, wr_hi, wr_lo, br, alpha):
    t, d = x.shape
    n_exp = wr_hi.shape[0]
    tm = min(TOKEN_TILE, t)
    kernel = functools.partial(_proj_ln_router_kernel, alpha=alpha, n_exp=n_exp)
    row = lambda i: (i, 0)
    fixed = lambda i: (0, 0)
    col = lambda i: (0, i)
    return pl.pallas_call(
        kernel,
        grid=(t // tm,),
        in_specs=[pl.BlockSpec((tm, d), row),
                  pl.BlockSpec((d, d), fixed),
                  pl.BlockSpec((tm, d), row),
                  pl.BlockSpec((1, d), fixed),
                  pl.BlockSpec((1, d), fixed),
                  pl.BlockSpec((n_exp, d), fixed),
                  pl.BlockSpec((n_exp, d), fixed),
                  pl.BlockSpec((n_exp, 1), fixed)],
        out_specs=[pl.BlockSpec((tm * (d // LANES), LANES), row),
                   pl.BlockSpec((EXPERT_TOPK, tm), col),
                   pl.BlockSpec((EXPERT_TOPK, tm), col),
                   pl.BlockSpec((EXPERT_TOPK, tm), col),
                   pl.BlockSpec((n_exp, 128), fixed)],
        out_shape=[jax.ShapeDtypeStruct((t * (d // LANES), LANES), F32),
                   jax.ShapeDtypeStruct((EXPERT_TOPK, t), I32),
                   jax.ShapeDtypeStruct((EXPERT_TOPK, t), F32),
                   jax.ShapeDtypeStruct((EXPERT_TOPK, t), I32),
                   jax.ShapeDtypeStruct((n_exp, 128), F32)],
        scratch_shapes=[pltpu.VMEM((n_exp, 128), F32)],
        compiler_params=_params(1),
        name="proj_ln_router",
    )(a, w, x, ln_g, ln_b, wr_hi, wr_lo, br)


def _moe_kernel(dest_ref, gate_ref, start_ref, count_ref, nch_ref,
                x2_ref, wg_ref, wu_ref, wd_ref, bg_ref, bu_ref, bd_ref,
                out_ref, src_ref, grow_ref, xg_ref, xb_ref, yf_ref, *, n_tok, n_exp):
    e = pl.program_id(0)
    tm, d = xb_ref.shape
    n_sub = d // LANES

    def tile(ref, idx):
        return ref.at[pl.ds(pl.multiple_of(idx * n_sub, n_sub), n_sub), :]

    @pl.when(e == 0)
    def _():
        out_ref[...] = jnp.zeros_like(out_ref)

        def fill(r, carry):
            src_ref[r] = 0
            return carry

        for ee in range(n_exp):
            lax.fori_loop(start_ref[ee] + count_ref[ee], start_ref[ee] + nch_ref[ee] * tm, fill, 0)

        def scatter(a, carry):
            for k in range(EXPERT_TOPK):
                r = dest_ref[k * n_tok + a]
                src_ref[r] = a
                grow_ref[r] = gate_ref[k * n_tok + a]
            return carry

        lax.fori_loop(0, n_tok, scatter, 0, unroll=8)

    def chunk(ch, carry):
        base = start_ref[e] + ch * tm

        def gather(r16, carry):
            for k in range(16):
                r = r16 * 16 + k
                tile(xg_ref, r)[...] = tile(x2_ref, src_ref[base + r])[...]
            return carry

        lax.fori_loop(0, tm // 16, gather, 0)
        for s in range(n_sub):
            xb_ref[:, s * LANES:(s + 1) * LANES] = xg_ref[pl.ds(s, tm, stride=n_sub), :].astype(BF16)

        x = xb_ref[...]
        g = jnp.dot(x, wg_ref[...], preferred_element_type=F32) + bg_ref[...]
        u = jnp.dot(x, wu_ref[...], preferred_element_type=F32) + bu_ref[...]
        g = jnp.minimum(g, SWIGLU_LIMIT)
        u = jnp.clip(u, -SWIGLU_LIMIT, SWIGLU_LIMIT)
        h = g * _sigmoid(SWIGLU_ALPHA * g) * (u + 1.0)
        y = jnp.dot(h.astype(BF16), wd_ref[...], preferred_element_type=F32) + bd_ref[...]
        for s in range(n_sub):
            yf_ref[pl.ds(s, tm, stride=n_sub), :] = y[:, s * LANES:(s + 1) * LANES]

        def combine(r8, carry):
            toks = [src_ref[base + r8 * 8 + k] for k in range(8)]
            gs = [grow_ref[base + r8 * 8 + k] for k in range(8)]
            olds = [tile(out_ref, toks[k])[...] for k in range(8)]
            for k in range(8):
                tile(out_ref, toks[k])[...] = olds[k] + gs[k] * tile(yf_ref, r8 * 8 + k)[...]
            return carry

        def combine_one(r, carry):
            tile(out_ref, src_ref[base + r])[...] += grow_ref[base + r] * tile(yf_ref, r)[...]
            return carry

        n_real = jnp.minimum(count_ref[e] - ch * tm, tm)
        lax.fori_loop(0, n_real // 8, combine, 0)
        lax.fori_loop(n_real // 8 * 8, n_real, combine_one, 0)
        return carry

    lax.fori_loop(0, nch_ref[e], chunk, 0)


def _moe_experts(x_tiles, t, d, dest, gates, start, count, nch, layer, wg, wu, wd, bg, bu, bd):
    n_sub = d // LANES
    tm = EXPERT_ROWS
    n_exp, dff = wg.shape[1], wg.shape[3]
    n_rows = t * EXPERT_TOPK + n_exp * tm

    def w_map(e, *_):
        return (layer, e, 0, 0)

    grid_spec = pltpu.PrefetchScalarGridSpec(
        num_scalar_prefetch=5,
        grid=(n_exp,),
        in_specs=[pl.BlockSpec((t * n_sub, LANES), lambda e, *_: (0, 0)),
                  pl.BlockSpec((None, None, d, dff), w_map),
                  pl.BlockSpec((None, None, d, dff), w_map),
                  pl.BlockSpec((None, None, dff, d), w_map),
                  pl.BlockSpec((None, None, 1, dff), w_map),
                  pl.BlockSpec((None, None, 1, dff), w_map),
                  pl.BlockSpec((None, None, 1, d), w_map)],
        out_specs=pl.BlockSpec((t * n_sub, LANES), lambda e, *_: (0, 0)),
        scratch_shapes=[pltpu.SMEM((n_rows,), I32),
                        pltpu.SMEM((n_rows,), F32),
                        pltpu.VMEM((tm * n_sub, LANES), F32),
                        pltpu.VMEM((tm, d), BF16),
                        pltpu.VMEM((tm * n_sub, LANES), F32)],
    )
    return pl.pallas_call(
        functools.partial(_moe_kernel, n_tok=t, n_exp=n_exp),
        grid_spec=grid_spec,
        out_shape=jax.ShapeDtypeStruct((t * n_sub, LANES), F32),
        compiler_params=_params(1),
        name="moe_experts",
    )(dest, gates, start, count, nch, x_tiles, wg, wu, wd, bg, bu, bd)


def _add_ln_kernel(x_ref, f_ref, g_ref, b_ref, o_ref, *, alpha):
    tm, d = o_ref.shape
    n_sub = d // LANES
    z = jnp.concatenate(
        [alpha * x_ref[pl.ds(s, tm, stride=n_sub), :] + f_ref[pl.ds(s, tm, stride=n_sub), :]
         for s in range(n_sub)], axis=1)
    o_ref[...] = _layer_norm(z, g_ref[...], b_ref[...])


def _add_ln(x_tiles, ffn_tiles, t, d, ln_g, ln_b, alpha):
    n_sub = d // LANES
    tm = min(TOKEN_TILE, t)
    return pl.pallas_call(
        functools.partial(_add_ln_kernel, alpha=alpha),
        grid=(t // tm,),
        in_specs=[pl.BlockSpec((tm * n_sub, LANES), lambda i: (i, 0)),
                  pl.BlockSpec((tm * n_sub, LANES), lambda i: (i, 0)),
                  pl.BlockSpec((1, d), lambda i: (0, 0)),
                  pl.BlockSpec((1, d), lambda i: (0, 0))],
        out_specs=pl.BlockSpec((tm, d), lambda i: (i, 0)),
        out_shape=jax.ShapeDtypeStruct((t, d), F32),
        compiler_params=_params(1),
        name="add_ln",
    )(x_tiles, ffn_tiles, ln_g, ln_b)


def _lru_in_kernel(x_ref, w_ref, y_ref, r_ref):
    d = y_ref.shape[1]
    yz = jnp.dot(x_ref[...].astype(BF16), w_ref[...], preferred_element_type=F32)
    y = yz[:, :d]
    y_ref[...] = 0.5 * y * (1.0 + jnp.tanh(math.sqrt(2.0 / math.pi) * (y + 0.044715 * (y * y * y))))
    r_ref[...] = yz[:, d:]


def _lru_in(x, w_in):
    t, d = x.shape
    n = w_in.shape[1] // 2
    tm = min(TOKEN_TILE, t)
    return pl.pallas_call(
        _lru_in_kernel,
        grid=(t // tm,),
        in_specs=[pl.BlockSpec((tm, d), lambda i: (i, 0)),
                  pl.BlockSpec((d, 2 * n), lambda i: (0, 0))],
        out_specs=[pl.BlockSpec((tm, n), lambda i: (i, 0)),
                   pl.BlockSpec((tm, n), lambda i: (i, 0))],
        out_shape=[jax.ShapeDtypeStruct((t, n), F32), jax.ShapeDtypeStruct((t, n), F32)],
        compiler_params=_params(1),
        name="lru_in",
    )(x, w_in)


def _lru_core_kernel(r_ref, y_ref, cw_ref, cb_ref, wgx_ref, wga_ref, bgx_ref, bga_ref, ap_ref,
                     hy_ref, rprev_ref, carry_ref, a_ref, u_ref):
    t = pl.program_id(0)
    tt, d = r_ref.shape
    n_blk, bw, _ = wgx_ref.shape
    cwid = cw_ref.shape[0]

    @pl.when(t == 0)
    def _():
        rprev_ref[...] = jnp.zeros_like(rprev_ref)
        carry_ref[...] = jnp.zeros_like(carry_ref)

    r = r_ref[...]
    rext = jnp.concatenate([rprev_ref[...], r], axis=0)
    rc = jnp.broadcast_to(cb_ref[...], (tt, d))
    for w in range(cwid):
        off = 8 - (cwid - 1) + w
        rc = rc + cw_ref[w:w + 1, :] * rext[off:off + tt, :]
    rprev_ref[...] = r[tt - 8:, :]

    rcb = rc.astype(BF16)
    gx = jnp.concatenate(
        [jnp.dot(rcb[:, n * bw:(n + 1) * bw], wgx_ref[n], preferred_element_type=F32)
         for n in range(n_blk)], axis=1)
    ga = jnp.concatenate(
        [jnp.dot(rcb[:, n * bw:(n + 1) * bw], wga_ref[n], preferred_element_type=F32)
         for n in range(n_blk)], axis=1)
    gate_x = _sigmoid(gx + bgx_ref[...])
    gate_a = _sigmoid(ga + bga_ref[...])
    z = -ap_ref[...]
    softplus = jnp.maximum(z, 0.0) + jnp.log(1.0 + jnp.exp(-jnp.abs(z)))
    log_a = -LRU_C * gate_a * softplus
    a_ref[...] = jnp.exp(log_a)
    u_ref[...] = jnp.sqrt(1.0 - jnp.exp(2.0 * log_a)) * (gate_x * rc)

    row = lax.broadcasted_iota(I32, (SCAN_ROWS, d), 0)

    def slab(s, carry):
        off = pl.multiple_of(s * SCAN_ROWS, SCAN_ROWS)
        a = a_ref[pl.ds(off, SCAN_ROWS), :]
        b = u_ref[pl.ds(off, SCAN_ROWS), :]
        step = 1
        while step < SCAN_ROWS:
            a_sh = pltpu.roll(a, step, 0)
            b_sh = pltpu.roll(b, step, 0)
            live = row >= step
            b = jnp.where(live, a * b_sh + b, b)
            a = jnp.where(live, a * a_sh, a)
            step *= 2
        h = b + a * carry
        u_ref[pl.ds(off, SCAN_ROWS), :] = h
        return h[SCAN_ROWS - 1:SCAN_ROWS, :]

    carry_ref[...] = lax.fori_loop(0, tt // SCAN_ROWS, slab, carry_ref[...])
    hy_ref[...] = (u_ref[...] * y_ref[...]).astype(hy_ref.dtype)


def _lru_core(r, y, conv_w, conv_b, wgx, wga, bgx, bga, a_param):
    t, d = r.shape
    tt = min(TOKEN_TILE, t)
    row = lambda i: (i, 0)
    fixed2 = lambda i: (0, 0)
    fixed3 = lambda i: (0, 0, 0)
    return pl.pallas_call(
        _lru_core_kernel,
        grid=(t // tt,),
        in_specs=[pl.BlockSpec((tt, d), row),
                  pl.BlockSpec((tt, d), row),
                  pl.BlockSpec(conv_w.shape, fixed2),
                  pl.BlockSpec((1, d), fixed2),
                  pl.BlockSpec(wgx.shape, fixed3),
                  pl.BlockSpec(wga.shape, fixed3),
                  pl.BlockSpec((1, d), fixed2),
                  pl.BlockSpec((1, d), fixed2),
                  pl.BlockSpec((1, d), fixed2)],
        out_specs=pl.BlockSpec((tt, d), row),
        out_shape=jax.ShapeDtypeStruct((t, d), BF16),
        scratch_shapes=[pltpu.VMEM((8, d), F32),
                        pltpu.VMEM((1, d), F32),
                        pltpu.VMEM((tt, d), F32),
                        pltpu.VMEM((tt, d), F32)],
        compiler_params=_params(1),
        name="lru_core",
    )(r, y, conv_w, conv_b, wgx, wga, bgx, bga, a_param)


def _moe(xn_tiles, t, d, ids, gates, rank, counts128, layer, wg, wu, wd, bg, bu, bd):
    tm = EXPERT_ROWS
    n_exp = wg.shape[1]
    counts = counts128[:, 0].astype(I32)
    nch = (counts + tm - 1) // tm
    start = (jnp.cumsum(nch) - nch).astype(I32) * tm
    experts = jnp.arange(n_exp, dtype=I32)
    dest = rank + jnp.sum(jnp.where(ids[:, :, None] == experts, start, 0), axis=-1)
    return _moe_experts(xn_tiles, t, d, dest.reshape(-1).astype(I32), gates.reshape(-1),
                        start, counts, nch.astype(I32), layer, wg, wu, wd, bg, bu, bd)


def kernel(x, rel_bias_table, w_qkv, w_attn_out, w_lru_in, lru_conv_w, lru_conv_b, w_lru_gate_x, b_lru_gate_x, w_lru_gate_a, b_lru_gate_a, lru_a_param, w_lru_out, w_router, b_router, w_exp_gate, b_exp_gate, w_exp_up, b_exp_up, w_exp_down, b_exp_down, ln_mix_g, ln_mix_b, ln_ffn_g, ln_ffn_b):
    bsz, t, d = x.shape
    depth = w_router.shape[0]
    n_mixers = 2
    alpha = (2 * depth) ** 0.25
    assert t % MOBA_BLOCK == 0 and t % TOKEN_TILE == 0

    biases = _attn_biases(rel_bias_table, t)
    w_qkv_b = w_qkv.astype(BF16)
    w_attn_out_b = w_attn_out.astype(BF16)
    w_lru_in_b = w_lru_in.astype(BF16)
    w_lru_out_b = w_lru_out.astype(BF16)
    wgx_b = w_lru_gate_x.astype(BF16)
    wga_b = w_lru_gate_a.astype(BF16)
    wg_b = w_exp_gate.astype(BF16)
    wu_b = w_exp_up.astype(BF16)
    wd_b = w_exp_down.astype(BF16)
    wr_t = jnp.swapaxes(w_router, 1, 2)
    wr_hi = wr_t.astype(BF16)
    wr_lo = (wr_t - wr_hi.astype(F32)).astype(BF16)

    bg4 = b_exp_gate[:, :, None, :]
    bu4 = b_exp_up[:, :, None, :]
    bd4 = b_exp_down[:, :, None, :]

    outs = []
    for b in range(bsz):
        xs = x[b]
        for i in range(depth):
            j = i // n_mixers
            if i % n_mixers == 0:
                qkv = _matmul(xs, w_qkv_b[j], BF16)
                a = _moba_attention(qkv, biases)
                w_out = w_attn_out_b[j]
            else:
                y, r = _lru_in(xs, w_lru_in_b[j])
                a = _lru_core(r, y, lru_conv_w[j], lru_conv_b[j][None], wgx_b[j], wga_b[j],
                              b_lru_gate_x[j][None], b_lru_gate_a[j][None], lru_a_param[j][None])
                w_out = w_lru_out_b[j]
            xn, ids, gates, rank, counts = _proj_ln_router(
                a, w_out, xs, ln_mix_g[i][None], ln_mix_b[i][None],
                wr_hi[i], wr_lo[i], b_router[i][:, None], alpha)
            ffn = _moe(xn, t, d, ids, gates, rank, counts, i, wg_b, wu_b, wd_b, bg4, bu4, bd4)
            xs = _add_ln(xn, ffn, t, d, ln_ffn_g[i][None], ln_ffn_b[i][None], alpha)
        outs.append(xs)
    return jnp.stack(outs, axis=0)
```

```python
import functools
import math

import numpy as np
import jax
import jax.numpy as jnp
from jax import lax
from jax.experimental import pallas as pl
from jax.experimental.pallas import tpu as pltpu

F32 = jnp.float32
BF16 = jnp.bfloat16
I32 = jnp.int32

N_HEADS = 8
MOBA_BLOCK = 256
MOBA_TOPK = 3
NUM_BUCKETS = 32
REL_MAX_DISTANCE = 128
LRU_C = 8.0
EXPERT_TOPK = 4
SWIGLU_LIMIT = 7.0
SWIGLU_ALPHA = 1.702
LN_EPS = 1e-5

EXPERT_ROWS = 256
TOKEN_TILE = 512
SCAN_ROWS = 8
LANES = 128
HEAD_GROUP = 8
COMBINE_BATCH = 8
V7X_VMEM_LIMIT = 56 * 1024 * 1024

_NT = (((1,), (1,)), ((), ()))


def _params(n_axes, vmem=V7X_VMEM_LIMIT):
    return pltpu.CompilerParams(
        dimension_semantics=("arbitrary",) * n_axes, vmem_limit_bytes=vmem)


def _sigmoid(x):
    return 1.0 / (1.0 + jnp.exp(-x))


def _layer_norm(z, g, b):
    mu = jnp.mean(z, axis=-1, keepdims=True)
    zc = z - mu
    var = jnp.mean(zc * zc, axis=-1, keepdims=True)
    return zc * lax.rsqrt(var + LN_EPS) * g + b


def _matmul_kernel(x_ref, w_ref, o_ref):
    o_ref[...] = jnp.dot(x_ref[...].astype(BF16), w_ref[...],
                         preferred_element_type=F32).astype(o_ref.dtype)


def _matmul(x, w, out_dtype):
    m, k = x.shape
    n = w.shape[1]
    tm = min(TOKEN_TILE, m)
    return pl.pallas_call(
        _matmul_kernel,
        grid=(m // tm,),
        in_specs=[pl.BlockSpec((tm, k), lambda i: (i, 0)),
                  pl.BlockSpec((k, n), lambda i: (0, 0))],
        out_specs=pl.BlockSpec((tm, n), lambda i: (i, 0)),
        out_shape=jax.ShapeDtypeStruct((m, n), out_dtype),
        compiler_params=_params(1),
        name="dense_proj",
    )(x, w)


def _attn_kernel(q_ref, k_ref, v_ref, bself_ref, badj_ref, bfar_ref, o_ref,
                 kmean_ref, vt_ref, pen_ref, m_ref, l_ref, acc_ref, *, nblk, scale):
    blk = MOBA_BLOCK
    hg = bself_ref.shape[0]
    dh = q_ref.shape[1] // hg
    i = pl.program_id(1)

    def head(ref, h):
        return ref.at[:, h * dh:(h + 1) * dh]

    @pl.when(i == 0)
    def _():
        for h in range(hg):
            for j in range(nblk):
                kj = head(k_ref, h)[j * blk:(j + 1) * blk, :].astype(F32)
                kmean_ref[h, j:j + 1, :] = jnp.sum(kj, axis=0, keepdims=True) * (1.0 / blk)
                vt_ref[h, j] = head(v_ref, h)[j * blk:(j + 1) * blk, :].astype(F32).T.astype(BF16)

    qs = []
    for h in range(hg):
        q = head(q_ref, h)[...]

        km = kmean_ref[h]
        km_hi = km.astype(BF16)
        km_lo = (km - km_hi.astype(F32)).astype(BF16)
        gate = (lax.dot_general(km_hi, q, _NT, preferred_element_type=F32)
                + lax.dot_general(km_lo, q, _NT, preferred_element_type=F32))
        jj = lax.broadcasted_iota(I32, gate.shape, 0)
        past = jj < i
        g = jnp.where(past, gate, -jnp.inf)
        sel = jnp.zeros(gate.shape, jnp.bool_)
        for _ in range(min(MOBA_TOPK, nblk)):
            mx = jnp.max(g, axis=0, keepdims=True)
            jm = jnp.min(jnp.where(g == mx, jj, nblk), axis=0, keepdims=True)
            pick = jj == jm
            sel = jnp.logical_or(sel, pick)
            g = jnp.where(pick, -jnp.inf, g)
        pen_ref[h] = jnp.where(jnp.logical_and(sel, past), 0.0, -jnp.inf)
        qs.append((q.astype(F32) * scale).astype(BF16))

    def scores(h, j):
        kb = head(k_ref, h)[pl.ds(pl.multiple_of(j * blk, blk), blk), :]
        return lax.dot_general(kb, qs[h], _NT, preferred_element_type=F32)

    def fold(x, op):
        return op(x.reshape(blk // 8, 8, blk), axis=0)

    def adj_bias(h):
        return badj_ref[h] + pen_ref[h, pl.ds(jnp.maximum(i - 1, 0), 1), :]

    def far_bias(h, j):
        return bfar_ref[h, 0:1, 0:1] + pen_ref[h, pl.ds(j, 1), :]

    n_far = jnp.maximum(i - 1, 0)

    for h in range(hg):
        m_ref[h] = fold(scores(h, i) + bself_ref[h], jnp.max)

    @pl.when(i >= 1)
    def _():
        for h in range(hg):
            m_ref[h] = jnp.maximum(m_ref[h], fold(scores(h, i - 1) + adj_bias(h), jnp.max))

    def far_max(j, ms):
        return tuple(jnp.maximum(ms[h], fold(scores(h, j) + far_bias(h, j), jnp.max))
                     for h in range(hg))

    ms = lax.fori_loop(0, n_far, far_max, tuple(m_ref[h] for h in range(hg)))
    ms = [jnp.max(m, axis=0, keepdims=True) for m in ms]

    def accumulate(h, j, s):
        p = jnp.exp(s - ms[h])
        acc_ref[h] += jnp.dot(vt_ref[h, j], p.astype(BF16), preferred_element_type=F32)
        return fold(p, jnp.sum)

    for h in range(hg):
        acc_ref[h] = jnp.zeros(acc_ref.shape[1:], F32)
    ss = [scores(h, i) + bself_ref[h] for h in range(hg)]
    for h in range(hg):
        l_ref[h] = accumulate(h, i, ss[h])

    @pl.when(i >= 1)
    def _():
        ss = [scores(h, i - 1) + adj_bias(h) for h in range(hg)]
        for h in range(hg):
            l_ref[h] += accumulate(h, i - 1, ss[h])

    def far_acc(j, ls):
        ss = [scores(h, j) + far_bias(h, j) for h in range(hg)]
        return tuple(ls[h] + accumulate(h, j, ss[h]) for h in range(hg))

    ls = lax.fori_loop(0, n_far, far_acc, tuple(l_ref[h] for h in range(hg)))

    for h in range(hg):
        o = acc_ref[h] / jnp.sum(ls[h], axis=0, keepdims=True)
        head(o_ref, h)[...] = o.T.astype(o_ref.dtype)


def _t5_bucket_np(dist):
    max_exact = NUM_BUCKETS // 2
    n = dist.astype(np.float32)
    large = max_exact + (np.log(np.maximum(n, np.float32(1.0)) / np.float32(max_exact))
                         / np.float32(math.log(REL_MAX_DISTANCE / max_exact))
                         * np.float32(NUM_BUCKETS - max_exact)).astype(np.int32)
    large = np.minimum(large, NUM_BUCKETS - 1)
    return np.where(dist < max_exact, dist, large)


def _t5_bucket(dist):
    max_exact = NUM_BUCKETS // 2
    n = dist.astype(F32)
    large = max_exact + (jnp.log(jnp.maximum(n, 1.0) / max_exact)
                         / math.log(REL_MAX_DISTANCE / max_exact)
                         * (NUM_BUCKETS - max_exact)).astype(I32)
    large = jnp.minimum(large, NUM_BUCKETS - 1)
    return jnp.where(dist < max_exact, dist, large)


def _attn_biases(rel_bias_table, t):
    blk = MOBA_BLOCK
    far_buckets = _t5_bucket_np(np.arange(blk + 1, max(t, 2 * blk) + 1, dtype=np.int32))
    assert (far_buckets == NUM_BUCKETS - 1).all()
    bbd = rel_bias_table.astype(F32)[_t5_bucket(jnp.arange(2 * blk + 1, dtype=I32))].T
    n_heads = bbd.shape[0]

    def toeplitz(w):
        rep = jnp.tile(w, (1, blk))[:, :blk * (2 * blk - 1)].reshape(n_heads, blk, 2 * blk - 1)
        return rep[:, :, blk - 1:]

    w_self = jnp.concatenate([jnp.full((n_heads, blk - 1), -jnp.inf, F32), bbd[:, :blk + 1]], axis=1)
    w_adj = jnp.concatenate([bbd[:, 1:2 * blk], bbd[:, :1]], axis=1)
    bself = toeplitz(w_self)
    badj = toeplitz(w_adj)
    bfar = jnp.broadcast_to(bbd[:, 2 * blk][:, None, None], (n_heads, 8, 128))
    return bself, badj, bfar


def _moba_attention(qkv, biases):
    t, d3 = qkv.shape
    d = d3 // 3
    dh = d // N_HEADS
    blk = MOBA_BLOCK
    nblk = t // blk
    bself, badj, bfar = biases
    hg = HEAD_GROUP
    n_grp = N_HEADS // hg
    kernel = functools.partial(_attn_kernel, nblk=nblk, scale=dh ** -0.5)
    return pl.pallas_call(
        kernel,
        grid=(n_grp, nblk),
        in_specs=[pl.BlockSpec((blk, hg * dh), lambda g, i: (i, g)),
                  pl.BlockSpec((t, hg * dh), lambda g, i: (0, n_grp + g)),
                  pl.BlockSpec((t, hg * dh), lambda g, i: (0, 2 * n_grp + g)),
                  pl.BlockSpec((hg, blk, blk), lambda g, i: (g, 0, 0)),
                  pl.BlockSpec((hg, blk, blk), lambda g, i: (g, 0, 0)),
                  pl.BlockSpec((hg, 8, LANES), lambda g, i: (g, 0, 0))],
        out_specs=pl.BlockSpec((blk, hg * dh), lambda g, i: (i, g)),
        out_shape=jax.ShapeDtypeStruct((t, d), BF16),
        scratch_shapes=[pltpu.VMEM((hg, nblk, dh), F32),
                        pltpu.VMEM((hg, nblk, dh, blk), BF16),
                        pltpu.VMEM((hg, nblk, blk), F32),
                        pltpu.VMEM((hg, 8, blk), F32),
                        pltpu.VMEM((hg, 8, blk), F32),
                        pltpu.VMEM((hg, dh, blk), F32)],
        compiler_params=_params(2),
        name="moba_attention",
    )(qkv, qkv, qkv, bself, badj, bfar)


def _proj_ln_router_kernel(a_ref, w_ref, x_ref, g_ref, b_ref, wrh_ref, wrl_ref, br_ref,
                           xn_ref, ids_ref, gates_ref, rank_ref, counts_ref, base_ref,
                           *, alpha, n_exp):
    t = pl.program_id(0)
    tm = a_ref.shape[0]

    @pl.when(t == 0)
    def _():
        base_ref[...] = jnp.zeros_like(base_ref)

    mix = jnp.dot(a_ref[...].astype(BF16), w_ref[...], preferred_element_type=F32)
    xn = _layer_norm(alpha * x_ref[...] + mix, g_ref[...], b_ref[...])
    n_sub = xn.shape[1] // LANES
    for s in range(n_sub):
        xn_ref[pl.ds(s, tm, stride=n_sub), :] = xn[:, s * LANES:(s + 1) * LANES]

    x_hi = xn.astype(BF16)
    x_lo = (xn - x_hi.astype(F32)).astype(BF16)
    wr_hi = wrh_ref[...]
    logits = (lax.dot_general(wr_hi, x_hi, _NT, preferred_element_type=F32)
              + lax.dot_general(wr_hi, x_lo, _NT, preferred_element_type=F32)
              + lax.dot_general(wrl_ref[...], x_hi, _NT, preferred_element_type=F32)
              + br_ref[...])

    e_iota = lax.broadcasted_iota(I32, logits.shape, 0)
    work = logits
    tops, picks = [], []
    for k in range(EXPERT_TOPK):
        mx = jnp.max(work, axis=0, keepdims=True)
        em = jnp.min(jnp.where(work == mx, e_iota, n_exp), axis=0, keepdims=True)
        pick = e_iota == em
        ids_ref[k:k + 1, :] = em
        tops.append(mx)
        picks.append(pick)
        work = jnp.where(pick, -jnp.inf, work)

    ex = [jnp.exp(tk - tops[0]) for tk in tops]
    den = ex[0]
    for e in ex[1:]:
        den = den + e
    for k in range(EXPERT_TOPK):
        gates_ref[k:k + 1, :] = ex[k] / den

    onehot = jnp.zeros(logits.shape, F32)
    for pick in picks:
        onehot = onehot + pick.astype(F32)
    earlier = (lax.broadcasted_iota(I32, (tm, tm), 0)
               < lax.broadcasted_iota(I32, (tm, tm), 1)).astype(BF16)
    pos = jnp.dot(onehot.astype(BF16), earlier, preferred_element_type=F32) + base_ref[:, 0:1]
    for k in range(EXPERT_TOPK):
        rank_ref[k:k + 1, :] = jnp.sum(jnp.where(picks[k], pos, 0.0), axis=0,
                                       keepdims=True).astype(I32)
    total = base_ref[...] + jnp.sum(onehot, axis=1, keepdims=True)
    base_ref[...] = total
    counts_ref[...] = total


def _proj_ln_router(a, w, x, ln_g, ln_b, wr_hi, wr_lo, br, alpha):
    t, d = x.shape
    n_exp = wr_hi.shape[0]
    tm = min(TOKEN_TILE, t)
    kernel = functools.partial(_proj_ln_router_kernel, alpha=alpha, n_exp=n_exp)
    row = lambda i: (i, 0)
    fixed = lambda i: (0, 0)
    col = lambda i: (0, i)
    return pl.pallas_call(
        kernel,
        grid=(t // tm,),
        in_specs=[pl.BlockSpec((tm, d), row),
                  pl.BlockSpec((d, d), fixed),
                  pl.BlockSpec((tm, d), row),
                  pl.BlockSpec((1, d), fixed),
                  pl.BlockSpec((1, d), fixed),
                  pl.BlockSpec((n_exp, d), fixed),
                  pl.BlockSpec((n_exp, d), fixed),
                  pl.BlockSpec((n_exp, 1), fixed)],
        out_specs=[pl.BlockSpec((tm * (d // LANES), LANES), row),
                   pl.BlockSpec((EXPERT_TOPK, tm), col),
                   pl.BlockSpec((EXPERT_TOPK, tm), col),
                   pl.BlockSpec((EXPERT_TOPK, tm), col),
                   pl.BlockSpec((n_exp, 128), fixed)],
        out_shape=[jax.ShapeDtypeStruct((t * (d // LANES), LANES), F32),
                   jax.ShapeDtypeStruct((EXPERT_TOPK, t), I32),
                   jax.ShapeDtypeStruct((EXPERT_TOPK, t), F32),
                   jax.ShapeDtypeStruct((EXPERT_TOPK, t), I32),
                   jax.ShapeDtypeStruct((n_exp, 128), F32)],
        scratch_shapes=[pltpu.VMEM((n_exp, 128), F32)],
        compiler_params=_params(1),
        name="proj_ln_router",
    )(a, w, x, ln_g, ln_b, wr_hi, wr_lo, br)


def _moe_kernel(dest_ref, gate_ref, start_ref, count_ref, nch_ref, last_ref,
                x2_ref, wg_ref, wu_ref, wd_ref, bg_ref, bu_ref, bd_ref,
                out_ref, src_ref, grow_ref, xg_ref, yf_ref, *, n_tok, n_exp):
    e = pl.program_id(0)
    n_sub = wg_ref.shape[0] // LANES
    tm = xg_ref.shape[0] // n_sub
    spare = COMBINE_BATCH

    def tile(ref, idx):
        return ref.at[pl.ds(pl.multiple_of(idx * n_sub, n_sub), n_sub), :]

    def gather_row(base, r):
        tok = jnp.minimum(src_ref[base + r], n_tok - 1)
        tile(xg_ref, r)[...] = tile(x2_ref, tok)[...]

    def combine_batch(base, r0):
        toks = [src_ref[base + r0 + k] for k in range(spare)]
        gs = [grow_ref[base + r0 + k] for k in range(spare)]
        olds = [tile(out_ref, toks[k])[...] for k in range(spare)]
        for k in range(spare):
            tile(out_ref, toks[k])[...] = olds[k] + gs[k] * tile(yf_ref, r0 + k)[...]

    @pl.when(e == 0)
    def _():
        out_ref[...] = jnp.zeros_like(out_ref)
        yf_ref[...] = jnp.zeros_like(yf_ref)

        def fill(r, carry):
            src_ref[r] = n_tok + r % spare
            grow_ref[r] = 0.0
            return carry

        for ee in range(n_exp):
            lax.fori_loop(start_ref[ee] + count_ref[ee], start_ref[ee] + nch_ref[ee] * tm, fill, 0)

        def scatter(a, carry):
            for k in range(EXPERT_TOPK):
                r = dest_ref[k * n_tok + a]
                src_ref[r] = a
                grow_ref[r] = gate_ref[k * n_tok + a]
            return carry

        lax.fori_loop(0, n_tok, scatter, 0, unroll=8)

        def first_rows(r, carry):
            gather_row(0, r)
            return carry

        lax.fori_loop(0, tm, first_rows, 0, unroll=8)

    def chunk(ch, carry):
        base = start_ref[e] + ch * tm
        x = jnp.concatenate(
            [xg_ref[pl.ds(s, tm, stride=n_sub), :].astype(BF16) for s in range(n_sub)], axis=1)

        nxt = jnp.minimum(base + tm, last_ref[0])
        for r in range(tm):
            gather_row(nxt, r)
        prev = jnp.maximum(base - tm, 0)
        for r0 in range(0, tm, spare):
            combine_batch(prev, r0)

        g = jnp.dot(x, wg_ref[...], preferred_element_type=F32) + bg_ref[...]
        u = jnp.dot(x, wu_ref[...], preferred_element_type=F32) + bu_ref[...]
        g = jnp.minimum(g, SWIGLU_LIMIT)
        u = jnp.clip(u, -SWIGLU_LIMIT, SWIGLU_LIMIT)
        h = g * _sigmoid(SWIGLU_ALPHA * g) * (u + 1.0)
        y = jnp.dot(h.astype(BF16), wd_ref[...], preferred_element_type=F32) + bd_ref[...]
        for s in range(n_sub):
            yf_ref[pl.ds(s, tm, stride=n_sub), :] = y[:, s * LANES:(s + 1) * LANES]
        return carry

    lax.fori_loop(0, nch_ref[e], chunk, 0)

    @pl.when(e == n_exp - 1)
    def _():
        def last_rows(b, carry):
            combine_batch(last_ref[0], b * spare)
            return carry

        lax.fori_loop(0, tm // spare, last_rows, 0)


def _moe_experts(x_tiles, t, d, dest, gates, start, count, nch, last, layer, wg, wu, wd, bg, bu, bd):
    n_sub = d // LANES
    tm = EXPERT_ROWS
    n_exp, dff = wg.shape[1], wg.shape[3]
    n_rows = t * EXPERT_TOPK + n_exp * tm

    def w_map(e, *_):
        return (layer, e, 0, 0)

    out_rows = (t + COMBINE_BATCH) * n_sub
    grid_spec = pltpu.PrefetchScalarGridSpec(
        num_scalar_prefetch=6,
        grid=(n_exp,),
        in_specs=[pl.BlockSpec((t * n_sub, LANES), lambda e, *_: (0, 0)),
                  pl.BlockSpec((None, None, d, dff), w_map),
                  pl.BlockSpec((None, None, d, dff), w_map),
                  pl.BlockSpec((None, None, dff, d), w_map),
                  pl.BlockSpec((None, None, 1, dff), w_map),
                  pl.BlockSpec((None, None, 1, dff), w_map),
                  pl.BlockSpec((None, None, 1, d), w_map)],
        out_specs=pl.BlockSpec((out_rows, LANES), lambda e, *_: (0, 0)),
        scratch_shapes=[pltpu.SMEM((n_rows,), I32),
                        pltpu.SMEM((n_rows,), F32),
                        pltpu.VMEM((tm * n_sub, LANES), F32),
                        pltpu.VMEM((tm * n_sub, LANES), F32)],
    )
    return pl.pallas_call(
        functools.partial(_moe_kernel, n_tok=t, n_exp=n_exp),
        grid_spec=grid_spec,
        out_shape=jax.ShapeDtypeStruct((out_rows, LANES), F32),
        compiler_params=_params(1),
        name="moe_experts",
    )(dest, gates, start, count, nch, last, x_tiles, wg, wu, wd, bg, bu, bd)


def _add_ln_kernel(x_ref, f_ref, g_ref, b_ref, o_ref, *, alpha):
    tm, d = o_ref.shape
    n_sub = d // LANES
    z = jnp.concatenate(
        [alpha * x_ref[pl.ds(s, tm, stride=n_sub), :] + f_ref[pl.ds(s, tm, stride=n_sub), :]
         for s in range(n_sub)], axis=1)
    o_ref[...] = _layer_norm(z, g_ref[...], b_ref[...])


def _add_ln(x_tiles, ffn_tiles, t, d, ln_g, ln_b, alpha):
    n_sub = d // LANES
    tm = min(TOKEN_TILE, t)
    return pl.pallas_call(
        functools.partial(_add_ln_kernel, alpha=alpha),
        grid=(t // tm,),
        in_specs=[pl.BlockSpec((tm * n_sub, LANES), lambda i: (i, 0)),
                  pl.BlockSpec((tm * n_sub, LANES), lambda i: (i, 0)),
                  pl.BlockSpec((1, d), lambda i: (0, 0)),
                  pl.BlockSpec((1, d), lambda i: (0, 0))],
        out_specs=pl.BlockSpec((tm, d), lambda i: (i, 0)),
        out_shape=jax.ShapeDtypeStruct((t, d), F32),
        compiler_params=_params(1),
        name="add_ln",
    )(x_tiles, ffn_tiles, ln_g, ln_b)


def _lru_in_kernel(x_ref, w_ref, y_ref, r_ref):
    d = y_ref.shape[1]
    yz = jnp.dot(x_ref[...].astype(BF16), w_ref[...], preferred_element_type=F32)
    y = yz[:, :d]
    y_ref[...] = 0.5 * y * (1.0 + jnp.tanh(math.sqrt(2.0 / math.pi) * (y + 0.044715 * (y * y * y))))
    r_ref[...] = yz[:, d:]


def _lru_in(x, w_in):
    t, d = x.shape
    n = w_in.shape[1] // 2
    tm = min(TOKEN_TILE, t)
    return pl.pallas_call(
        _lru_in_kernel,
        grid=(t // tm,),
        in_specs=[pl.BlockSpec((tm, d), lambda i: (i, 0)),
                  pl.BlockSpec((d, 2 * n), lambda i: (0, 0))],
        out_specs=[pl.BlockSpec((tm, n), lambda i: (i, 0)),
                   pl.BlockSpec((tm, n), lambda i: (i, 0))],
        out_shape=[jax.ShapeDtypeStruct((t, n), F32), jax.ShapeDtypeStruct((t, n), F32)],
        compiler_params=_params(1),
        name="lru_in",
    )(x, w_in)


def _lru_core_kernel(r_ref, y_ref, cw_ref, cb_ref, wgx_ref, wga_ref, bgx_ref, bga_ref, ap_ref,
                     hy_ref, rprev_ref, carry_ref, a_ref, u_ref):
    t = pl.program_id(0)
    tt, d = r_ref.shape
    n_blk, bw, _ = wgx_ref.shape
    cwid = cw_ref.shape[0]

    @pl.when(t == 0)
    def _():
        rprev_ref[...] = jnp.zeros_like(rprev_ref)
        carry_ref[...] = jnp.zeros_like(carry_ref)

    r = r_ref[...]
    rext = jnp.concatenate([rprev_ref[...], r], axis=0)
    rc = jnp.broadcast_to(cb_ref[...], (tt, d))
    for w in range(cwid):
        off = 8 - (cwid - 1) + w
        rc = rc + cw_ref[w:w + 1, :] * rext[off:off + tt, :]
    rprev_ref[...] = r[tt - 8:, :]

    rcb = rc.astype(BF16)
    gx = jnp.concatenate(
        [jnp.dot(rcb[:, n * bw:(n + 1) * bw], wgx_ref[n], preferred_element_type=F32)
         for n in range(n_blk)], axis=1)
    ga = jnp.concatenate(
        [jnp.dot(rcb[:, n * bw:(n + 1) * bw], wga_ref[n], preferred_element_type=F32)
         for n in range(n_blk)], axis=1)
    gate_x = _sigmoid(gx + bgx_ref[...])
    gate_a = _sigmoid(ga + bga_ref[...])
    z = -ap_ref[...]
    softplus = jnp.maximum(z, 0.0) + jnp.log(1.0 + jnp.exp(-jnp.abs(z)))
    log_a = -LRU_C * gate_a * softplus
    a_ref[...] = jnp.exp(log_a)
    u_ref[...] = jnp.sqrt(1.0 - jnp.exp(2.0 * log_a)) * (gate_x * rc)

    row = lax.broadcasted_iota(I32, (SCAN_ROWS, d), 0)

    def slab(s, carry):
        off = pl.multiple_of(s * SCAN_ROWS, SCAN_ROWS)
        a = a_ref[pl.ds(off, SCAN_ROWS), :]
        b = u_ref[pl.ds(off, SCAN_ROWS), :]
        step = 1
        while step < SCAN_ROWS:
            a_sh = pltpu.roll(a, step, 0)
            b_sh = pltpu.roll(b, step, 0)
            live = row >= step
            b = jnp.where(live, a * b_sh + b, b)
            a = jnp.where(live, a * a_sh, a)
            step *= 2
        h = b + a * carry
        u_ref[pl.ds(off, SCAN_ROWS), :] = h
        return h[SCAN_ROWS - 1:SCAN_ROWS, :]

    carry_ref[...] = lax.fori_loop(0, tt // SCAN_ROWS, slab, carry_ref[...])
    hy_ref[...] = (u_ref[...] * y_ref[...]).astype(hy_ref.dtype)


def _lru_core(r, y, conv_w, conv_b, wgx, wga, bgx, bga, a_param):
    t, d = r.shape
    tt = min(TOKEN_TILE, t)
    row = lambda i: (i, 0)
    fixed2 = lambda i: (0, 0)
    fixed3 = lambda i: (0, 0, 0)
    return pl.pallas_call(
        _lru_core_kernel,
        grid=(t // tt,),
        in_specs=[pl.BlockSpec((tt, d), row),
                  pl.BlockSpec((tt, d), row),
                  pl.BlockSpec(conv_w.shape, fixed2),
                  pl.BlockSpec((1, d), fixed2),
                  pl.BlockSpec(wgx.shape, fixed3),
                  pl.BlockSpec(wga.shape, fixed3),
                  pl.BlockSpec((1, d), fixed2),
                  pl.BlockSpec((1, d), fixed2),
                  pl.BlockSpec((1, d), fixed2)],
        out_specs=pl.BlockSpec((tt, d), row),
        out_shape=jax.ShapeDtypeStruct((t, d), BF16),
        scratch_shapes=[pltpu.VMEM((8, d), F32),
                        pltpu.VMEM((1, d), F32),
                        pltpu.VMEM((tt, d), F32),
                        pltpu.VMEM((tt, d), F32)],
        compiler_params=_params(1),
        name="lru_core",
    )(r, y, conv_w, conv_b, wgx, wga, bgx, bga, a_param)


def _moe(xn_tiles, t, d, ids, gates, rank, counts128, layer, wg, wu, wd, bg, bu, bd):
    tm = EXPERT_ROWS
    n_exp = wg.shape[1]
    counts = counts128[:, 0].astype(I32)
    nch = (counts + tm - 1) // tm
    start = (jnp.cumsum(nch) - nch).astype(I32) * tm
    experts = jnp.arange(n_exp, dtype=I32)
    dest = rank + jnp.sum(jnp.where(ids[:, :, None] == experts, start, 0), axis=-1)
    last = ((jnp.sum(nch) - 1) * tm).astype(I32).reshape(1)
    return _moe_experts(xn_tiles, t, d, dest.reshape(-1).astype(I32), gates.reshape(-1),
                        start, counts, nch.astype(I32), last, layer, wg, wu, wd, bg, bu, bd)


def kernel(x, rel_bias_table, w_qkv, w_attn_out, w_lru_in, lru_conv_w, lru_conv_b, w_lru_gate_x, b_lru_gate_x, w_lru_gate_a, b_lru_gate_a, lru_a_param, w_lru_out, w_router, b_router, w_exp_gate, b_exp_gate, w_exp_up, b_exp_up, w_exp_down, b_exp_down, ln_mix_g, ln_mix_b, ln_ffn_g, ln_ffn_b):
    bsz, t, d = x.shape
    depth = w_router.shape[0]
    n_mixers = 2
    alpha = (2 * depth) ** 0.25
    assert t % MOBA_BLOCK == 0 and t % TOKEN_TILE == 0

    biases = _attn_biases(rel_bias_table, t)
    w_qkv_b = w_qkv.astype(BF16)
    w_attn_out_b = w_attn_out.astype(BF16)
    w_lru_in_b = w_lru_in.astype(BF16)
    w_lru_out_b = w_lru_out.astype(BF16)
    wgx_b = w_lru_gate_x.astype(BF16)
    wga_b = w_lru_gate_a.astype(BF16)
    wg_b = w_exp_gate.astype(BF16)
    wu_b = w_exp_up.astype(BF16)
    wd_b = w_exp_down.astype(BF16)
    wr_t = jnp.swapaxes(w_router, 1, 2)
    wr_hi = wr_t.astype(BF16)
    wr_lo = (wr_t - wr_hi.astype(F32)).astype(BF16)

    bg4 = b_exp_gate[:, :, None, :]
    bu4 = b_exp_up[:, :, None, :]
    bd4 = b_exp_down[:, :, None, :]

    outs = []
    for b in range(bsz):
        xs = x[b]
        for i in range(depth):
            j = i // n_mixers
            if i % n_mixers == 0:
                qkv = _matmul(xs, w_qkv_b[j], BF16)
                a = _moba_attention(qkv, biases)
                w_out = w_attn_out_b[j]
            else:
                y, r = _lru_in(xs, w_lru_in_b[j])
                a = _lru_core(r, y, lru_conv_w[j], lru_conv_b[j][None], wgx_b[j], wga_b[j],
                              b_lru_gate_x[j][None], b_lru_gate_a[j][None], lru_a_param[j][None])
                w_out = w_lru_out_b[j]
            xn, ids, gates, rank, counts = _proj_ln_router(
                a, w_out, xs, ln_mix_g[i][None], ln_mix_b[i][None],
                wr_hi[i], wr_lo[i], b_router[i][:, None], alpha)
            ffn = _moe(xn, t, d, ids, gates, rank, counts, i, wg_b, wu_b, wd_b, bg4, bu4, bd4)
            xs = _add_ln(xn, ffn, t, d, ln_ffn_g[i][None], ln_ffn_b[i][None], alpha)
        outs.append(xs)
    return jnp.stack(outs, axis=0)
```

```python
import functools
import math

import numpy as np
import jax
import jax.numpy as jnp
from jax import lax
from jax.experimental import pallas as pl
from jax.experimental.pallas import tpu as pltpu

F32 = jnp.float32
BF16 = jnp.bfloat16
I32 = jnp.int32

N_HEADS = 8
MOBA_BLOCK = 256
MOBA_TOPK = 3
NUM_BUCKETS = 32
REL_MAX_DISTANCE = 128
LRU_C = 8.0
EXPERT_TOPK = 4
SWIGLU_LIMIT = 7.0
SWIGLU_ALPHA = 1.702
LN_EPS = 1e-5
LOG2_E = 1.4426950408889634

EXPERT_ROWS = 256
TOKEN_TILE = 512
SCAN_ROWS = 8
LANES = 128
HEAD_GROUP = 8
COMBINE_BATCH = 8
V7X_VMEM_LIMIT = 56 * 1024 * 1024

_NT = (((1,), (1,)), ((), ()))


def _params(n_axes, vmem=V7X_VMEM_LIMIT):
    return pltpu.CompilerParams(
        dimension_semantics=("arbitrary",) * n_axes, vmem_limit_bytes=vmem)


def _sigmoid(x):
    return 1.0 / (1.0 + jnp.exp(-x))


def _layer_norm(z, g, b):
    mu = jnp.mean(z, axis=-1, keepdims=True)
    zc = z - mu
    var = jnp.mean(zc * zc, axis=-1, keepdims=True)
    return zc * lax.rsqrt(var + LN_EPS) * g + b


def _matmul_kernel(x_ref, w_ref, o_ref):
    o_ref[...] = jnp.dot(x_ref[...].astype(BF16), w_ref[...],
                         preferred_element_type=F32).astype(o_ref.dtype)


def _matmul(x, w, out_dtype):
    m, k = x.shape
    n = w.shape[1]
    tm = min(TOKEN_TILE, m)
    return pl.pallas_call(
        _matmul_kernel,
        grid=(m // tm,),
        in_specs=[pl.BlockSpec((tm, k), lambda i: (i, 0)),
                  pl.BlockSpec((k, n), lambda i: (0, 0))],
        out_specs=pl.BlockSpec((tm, n), lambda i: (i, 0)),
        out_shape=jax.ShapeDtypeStruct((m, n), out_dtype),
        compiler_params=_params(1),
        name="dense_proj",
    )(x, w)


def _attn_kernel(q_ref, k_ref, v_ref, bself_ref, badj_ref, bfar_ref, o_ref,
                 kmean_ref, vt_ref, pen_ref, m_ref, l_ref, acc_ref, *, nblk, scale):
    blk = MOBA_BLOCK
    hg = bself_ref.shape[0]
    dh = q_ref.shape[1] // hg
    i = pl.program_id(1)

    def head(ref, h):
        return ref.at[:, h * dh:(h + 1) * dh]

    @pl.when(i == 0)
    def _():
        for h in range(hg):
            for j in range(nblk):
                kj = head(k_ref, h)[j * blk:(j + 1) * blk, :].astype(F32)
                kmean_ref[h, j:j + 1, :] = jnp.sum(kj, axis=0, keepdims=True) * (1.0 / blk)
                vt_ref[h, j] = head(v_ref, h)[j * blk:(j + 1) * blk, :].astype(F32).T.astype(BF16)

    qs = []
    for h in range(hg):
        q = head(q_ref, h)[...]

        km = kmean_ref[h]
        km_hi = km.astype(BF16)
        km_lo = (km - km_hi.astype(F32)).astype(BF16)
        gate = (lax.dot_general(km_hi, q, _NT, preferred_element_type=F32)
                + lax.dot_general(km_lo, q, _NT, preferred_element_type=F32))
        jj = lax.broadcasted_iota(I32, gate.shape, 0)
        past = jj < i
        g = jnp.where(past, gate, -jnp.inf)
        sel = jnp.zeros(gate.shape, jnp.bool_)
        for _ in range(min(MOBA_TOPK, nblk)):
            mx = jnp.max(g, axis=0, keepdims=True)
            jm = jnp.min(jnp.where(g == mx, jj, nblk), axis=0, keepdims=True)
            pick = jj == jm
            sel = jnp.logical_or(sel, pick)
            g = jnp.where(pick, -jnp.inf, g)
        pen_ref[h] = jnp.where(jnp.logical_and(sel, past), 0.0, -jnp.inf)
        qs.append((q.astype(F32) * (scale * LOG2_E)).astype(BF16))

    def scores(h, j):
        kb = head(k_ref, h)[pl.ds(pl.multiple_of(j * blk, blk), blk), :]
        return lax.dot_general(kb, qs[h], _NT, preferred_element_type=F32)

    def fold(x, op):
        return op(x.reshape(blk // 8, 8, blk), axis=0)

    def adj_bias(h):
        return badj_ref[h] + pen_ref[h, pl.ds(jnp.maximum(i - 1, 0), 1), :]

    def far_bias(h, j):
        return bfar_ref[h, 0:1, 0:1] + pen_ref[h, pl.ds(j, 1), :]

    n_far = jnp.maximum(i - 1, 0)

    for h in range(hg):
        m_ref[h] = fold(scores(h, i) + bself_ref[h], jnp.max)

    @pl.when(i >= 1)
    def _():
        for h in range(hg):
            m_ref[h] = jnp.maximum(m_ref[h], fold(scores(h, i - 1) + adj_bias(h), jnp.max))

    def far_max(j, ms):
        return tuple(jnp.maximum(ms[h], fold(scores(h, j), jnp.max) + far_bias(h, j))
                     for h in range(hg))

    ms = lax.fori_loop(0, n_far, far_max, tuple(m_ref[h] for h in range(hg)))
    ms = [jnp.max(m, axis=0, keepdims=True) for m in ms]

    def accumulate(h, j, s_minus_max):
        p = jnp.exp2(s_minus_max)
        acc_ref[h] += jnp.dot(vt_ref[h, j], p.astype(BF16), preferred_element_type=F32)
        return fold(p, jnp.sum)

    for h in range(hg):
        acc_ref[h] = jnp.zeros(acc_ref.shape[1:], F32)
    ss = [scores(h, i) + (bself_ref[h] - ms[h]) for h in range(hg)]
    for h in range(hg):
        l_ref[h] = accumulate(h, i, ss[h])

    @pl.when(i >= 1)
    def _():
        ss = [scores(h, i - 1) + (adj_bias(h) - ms[h]) for h in range(hg)]
        for h in range(hg):
            l_ref[h] += accumulate(h, i - 1, ss[h])

    def far_acc(j, ls):
        ss = [scores(h, j) + (far_bias(h, j) - ms[h]) for h in range(hg)]
        return tuple(ls[h] + accumulate(h, j, ss[h]) for h in range(hg))

    ls = lax.fori_loop(0, n_far, far_acc, tuple(l_ref[h] for h in range(hg)))

    for h in range(hg):
        o = acc_ref[h] / jnp.sum(ls[h], axis=0, keepdims=True)
        head(o_ref, h)[...] = o.T.astype(o_ref.dtype)


def _t5_bucket_np(dist):
    max_exact = NUM_BUCKETS // 2
    n = dist.astype(np.float32)
    large = max_exact + (np.log(np.maximum(n, np.float32(1.0)) / np.float32(max_exact))
                         / np.float32(math.log(REL_MAX_DISTANCE / max_exact))
                         * np.float32(NUM_BUCKETS - max_exact)).astype(np.int32)
    large = np.minimum(large, NUM_BUCKETS - 1)
    return np.where(dist < max_exact, dist, large)


def _t5_bucket(dist):
    max_exact = NUM_BUCKETS // 2
    n = dist.astype(F32)
    large = max_exact + (jnp.log(jnp.maximum(n, 1.0) / max_exact)
                         / math.log(REL_MAX_DISTANCE / max_exact)
                         * (NUM_BUCKETS - max_exact)).astype(I32)
    large = jnp.minimum(large, NUM_BUCKETS - 1)
    return jnp.where(dist < max_exact, dist, large)


def _attn_biases(rel_bias_table, t):
    blk = MOBA_BLOCK
    far_buckets = _t5_bucket_np(np.arange(blk + 1, max(t, 2 * blk) + 1, dtype=np.int32))
    assert (far_buckets == NUM_BUCKETS - 1).all()
    bbd = rel_bias_table.astype(F32)[_t5_bucket(jnp.arange(2 * blk + 1, dtype=I32))].T * LOG2_E
    n_heads = bbd.shape[0]

    def toeplitz(w):
        rep = jnp.tile(w, (1, blk))[:, :blk * (2 * blk - 1)].reshape(n_heads, blk, 2 * blk - 1)
        return rep[:, :, blk - 1:]

    w_self = jnp.concatenate([jnp.full((n_heads, blk - 1), -jnp.inf, F32), bbd[:, :blk + 1]], axis=1)
    w_adj = jnp.concatenate([bbd[:, 1:2 * blk], bbd[:, :1]], axis=1)
    bself = toeplitz(w_self)
    badj = toeplitz(w_adj)
    bfar = jnp.broadcast_to(bbd[:, 2 * blk][:, None, None], (n_heads, 8, 128))
    return bself, badj, bfar


def _moba_attention(qkv, biases):
    t, d3 = qkv.shape
    d = d3 // 3
    dh = d // N_HEADS
    blk = MOBA_BLOCK
    nblk = t // blk
    bself, badj, bfar = biases
    hg = HEAD_GROUP
    n_grp = N_HEADS // hg
    kernel = functools.partial(_attn_kernel, nblk=nblk, scale=dh ** -0.5)
    return pl.pallas_call(
        kernel,
        grid=(n_grp, nblk),
        in_specs=[pl.BlockSpec((blk, hg * dh), lambda g, i: (i, g)),
                  pl.BlockSpec((t, hg * dh), lambda g, i: (0, n_grp + g)),
                  pl.BlockSpec((t, hg * dh), lambda g, i: (0, 2 * n_grp + g)),
                  pl.BlockSpec((hg, blk, blk), lambda g, i: (g, 0, 0)),
                  pl.BlockSpec((hg, blk, blk), lambda g, i: (g, 0, 0)),
                  pl.BlockSpec((hg, 8, LANES), lambda g, i: (g, 0, 0))],
        out_specs=pl.BlockSpec((blk, hg * dh), lambda g, i: (i, g)),
        out_shape=jax.ShapeDtypeStruct((t, d), BF16),
        scratch_shapes=[pltpu.VMEM((hg, nblk, dh), F32),
                        pltpu.VMEM((hg, nblk, dh, blk), BF16),
                        pltpu.VMEM((hg, nblk, blk), F32),
                        pltpu.VMEM((hg, 8, blk), F32),
                        pltpu.VMEM((hg, 8, blk), F32),
                        pltpu.VMEM((hg, dh, blk), F32)],
        compiler_params=_params(2),
        name="moba_attention",
    )(qkv, qkv, qkv, bself, badj, bfar)


def _proj_ln_router_kernel(a_ref, w_ref, x_ref, g_ref, b_ref, wrh_ref, wrl_ref, br_ref,
                           xn_ref, ids_ref, gates_ref, rank_ref, counts_ref, base_ref,
                           *, alpha, n_exp):
    t = pl.program_id(0)
    tm = a_ref.shape[0]

    @pl.when(t == 0)
    def _():
        base_ref[...] = jnp.zeros_like(base_ref)

    mix = jnp.dot(a_ref[...].astype(BF16), w_ref[...], preferred_element_type=F32)
    xn = _layer_norm(alpha * x_ref[...] + mix, g_ref[...], b_ref[...])
    n_sub = xn.shape[1] // LANES
    for s in range(n_sub):
        xn_ref[pl.ds(s, tm, stride=n_sub), :] = xn[:, s * LANES:(s + 1) * LANES]

    x_hi = xn.astype(BF16)
    x_lo = (xn - x_hi.astype(F32)).astype(BF16)
    wr_hi = wrh_ref[...]
    logits = (lax.dot_general(wr_hi, x_hi, _NT, preferred_element_type=F32)
              + lax.dot_general(wr_hi, x_lo, _NT, preferred_element_type=F32)
              + lax.dot_general(wrl_ref[...], x_hi, _NT, preferred_element_type=F32)
              + br_ref[...])

    e_iota = lax.broadcasted_iota(I32, logits.shape, 0)
    work = logits
    tops, picks = [], []
    for k in range(EXPERT_TOPK):
        mx = jnp.max(work, axis=0, keepdims=True)
        em = jnp.min(jnp.where(work == mx, e_iota, n_exp), axis=0, keepdims=True)
        pick = e_iota == em
        ids_ref[k:k + 1, :] = em
        tops.append(mx)
        picks.append(pick)
        work = jnp.where(pick, -jnp.inf, work)

    ex = [jnp.exp(tk - tops[0]) for tk in tops]
    den = ex[0]
    for e in ex[1:]:
        den = den + e
    for k in range(EXPERT_TOPK):
        gates_ref[k:k + 1, :] = ex[k] / den

    onehot = jnp.zeros(logits.shape, F32)
    for pick in picks:
        onehot = onehot + pick.astype(F32)
    earlier = (lax.broadcasted_iota(I32, (tm, tm), 0)
               < lax.broadcasted_iota(I32, (tm, tm), 1)).astype(BF16)
    pos = jnp.dot(onehot.astype(BF16), earlier, preferred_element_type=F32) + base_ref[:, 0:1]
    for k in range(EXPERT_TOPK):
        rank_ref[k:k + 1, :] = jnp.sum(jnp.where(picks[k], pos, 0.0), axis=0,
                                       keepdims=True).astype(I32)
    total = base_ref[...] + jnp.sum(onehot, axis=1, keepdims=True)
    base_ref[...] = total
    counts_ref[...] = total


def _proj_ln_router(a, w, x, ln_g, ln_b, wr_hi, wr_lo, br, alpha):
    t, d = x.shape
    n_exp = wr_hi.shape[0]
    tm = min(TOKEN_TILE, t)
    kernel = functools.partial(_proj_ln_router_kernel, alpha=alpha, n_exp=n_exp)
    row = lambda i: (i, 0)
    fixed = lambda i: (0, 0)
    col = lambda i: (0, i)
    return pl.pallas_call(
        kernel,
        grid=(t // tm,),
        in_specs=[pl.BlockSpec((tm, d), row),
                  pl.BlockSpec((d, d), fixed),
                  pl.BlockSpec((tm, d), row),
                  pl.BlockSpec((1, d), fixed),
                  pl.BlockSpec((1, d), fixed),
                  pl.BlockSpec((n_exp, d), fixed),
                  pl.BlockSpec((n_exp, d), fixed),
                  pl.BlockSpec((n_exp, 1), fixed)],
        out_specs=[pl.BlockSpec((tm * (d // LANES), LANES), row),
                   pl.BlockSpec((EXPERT_TOPK, tm), col),
                   pl.BlockSpec((EXPERT_TOPK, tm), col),
                   pl.BlockSpec((EXPERT_TOPK, tm), col),
                   pl.BlockSpec((n_exp, 128), fixed)],
        out_shape=[jax.ShapeDtypeStruct((t * (d // LANES), LANES), F32),
                   jax.ShapeDtypeStruct((EXPERT_TOPK, t), I32),
                   jax.ShapeDtypeStruct((EXPERT_TOPK, t), F32),
                   jax.ShapeDtypeStruct((EXPERT_TOPK, t), I32),
                   jax.ShapeDtypeStruct((n_exp, 128), F32)],
        scratch_shapes=[pltpu.VMEM((n_exp, 128), F32)],
        compiler_params=_params(1),
        name="proj_ln_router",
    )(a, w, x, ln_g, ln_b, wr_hi, wr_lo, br)


def _moe_kernel(dest_ref, gate_ref, start_ref, count_ref, nch_ref, last_ref,
                x2_ref, wg_ref, wu_ref, wd_ref, bg_ref, bu_ref, bd_ref,
                out_ref, src_ref, xg_ref, yf_ref, *, n_tok, n_exp, tok_bits):
    e = pl.program_id(0)
    n_sub = wg_ref.shape[0] // LANES
    tm = xg_ref.shape[0] // n_sub
    spare = COMBINE_BATCH

    def tile(ref, idx):
        return ref.at[pl.ds(pl.multiple_of(idx * n_sub, n_sub), n_sub), :]

    tok_mask = (1 << tok_bits) - 1

    def gather_row(base, r):
        tok = jnp.minimum(src_ref[base + r] & tok_mask, n_tok - 1)
        tile(xg_ref, r)[...] = tile(x2_ref, tok)[...]

    def combine_batch(base, r0):
        codes = [src_ref[base + r0 + k] for k in range(spare)]
        toks = [c & tok_mask for c in codes]
        gs = [gate_ref[c] for c in codes]
        olds = [tile(out_ref, toks[k])[...] for k in range(spare)]
        for k in range(spare):
            tile(out_ref, toks[k])[...] = olds[k] + gs[k] * tile(yf_ref, r0 + k)[...]

    @pl.when(e == 0)
    def _():
        out_ref[...] = jnp.zeros_like(out_ref)
        yf_ref[...] = jnp.zeros_like(yf_ref)

        def fill(r, carry):
            src_ref[r] = n_tok + r % spare + (EXPERT_TOPK << tok_bits)
            return carry

        for ee in range(n_exp):
            lax.fori_loop(start_ref[ee] + count_ref[ee], start_ref[ee] + nch_ref[ee] * tm, fill, 0)

        def scatter(a, carry):
            for k in range(EXPERT_TOPK):
                src_ref[dest_ref[k * n_tok + a]] = a + (k << tok_bits)
            return carry

        lax.fori_loop(0, n_tok, scatter, 0, unroll=8)

        def first_rows(r, carry):
            gather_row(0, r)
            return carry

        lax.fori_loop(0, tm, first_rows, 0, unroll=8)

    def chunk(ch, carry):
        base = start_ref[e] + ch * tm
        x = jnp.concatenate(
            [xg_ref[pl.ds(s, tm, stride=n_sub), :].astype(BF16) for s in range(n_sub)], axis=1)

        nxt = jnp.minimum(base + tm, last_ref[0])
        for r in range(tm):
            gather_row(nxt, r)
        prev = jnp.maximum(base - tm, 0)
        for r0 in range(0, tm, spare):
            combine_batch(prev, r0)

        g = jnp.dot(x, wg_ref[...], preferred_element_type=F32) + bg_ref[...]
        u = jnp.dot(x, wu_ref[...], preferred_element_type=F32) + bu_ref[...]
        g = jnp.minimum(g, SWIGLU_LIMIT)
        u = jnp.clip(u, -SWIGLU_LIMIT, SWIGLU_LIMIT)
        h = g * _sigmoid(SWIGLU_ALPHA * g) * (u + 1.0)
        y = jnp.dot(h.astype(BF16), wd_ref[...], preferred_element_type=F32) + bd_ref[...]
        for s in range(n_sub):
            yf_ref[pl.ds(s, tm, stride=n_sub), :] = y[:, s * LANES:(s + 1) * LANES]
        return carry

    lax.fori_loop(0, nch_ref[e], chunk, 0)

    @pl.when(e == n_exp - 1)
    def _():
        def last_rows(b, carry):
            combine_batch(last_ref[0], b * spare)
            return carry

        lax.fori_loop(0, tm // spare, last_rows, 0)


def _moe_experts(x_tiles, t, d, tok_bits, dest, gates, start, count, nch, last, layer,
                 wg, wu, wd, bg, bu, bd):
    n_sub = d // LANES
    tm = EXPERT_ROWS
    n_exp, dff = wg.shape[1], wg.shape[3]
    n_rows = t * EXPERT_TOPK + n_exp * tm

    def w_map(e, *_):
        return (layer, e, 0, 0)

    out_rows = (t + COMBINE_BATCH) * n_sub
    grid_spec = pltpu.PrefetchScalarGridSpec(
        num_scalar_prefetch=6,
        grid=(n_exp,),
        in_specs=[pl.BlockSpec((t * n_sub, LANES), lambda e, *_: (0, 0)),
                  pl.BlockSpec((None, None, d, dff), w_map),
                  pl.BlockSpec((None, None, d, dff), w_map),
                  pl.BlockSpec((None, None, dff, d), w_map),
                  pl.BlockSpec((None, None, 1, dff), w_map),
                  pl.BlockSpec((None, None, 1, dff), w_map),
                  pl.BlockSpec((None, None, 1, d), w_map)],
        out_specs=pl.BlockSpec((out_rows, LANES), lambda e, *_: (0, 0)),
        scratch_shapes=[pltpu.SMEM((n_rows,), I32),
                        pltpu.VMEM((tm * n_sub, LANES), F32),
                        pltpu.VMEM((tm * n_sub, LANES), F32)],
    )
    return pl.pallas_call(
        functools.partial(_moe_kernel, n_tok=t, n_exp=n_exp, tok_bits=tok_bits),
        grid_spec=grid_spec,
        out_shape=jax.ShapeDtypeStruct((out_rows, LANES), F32),
        compiler_params=_params(1),
        name="moe_experts",
    )(dest, gates, start, count, nch, last, x_tiles, wg, wu, wd, bg, bu, bd)


def _add_ln_kernel(x_ref, f_ref, g_ref, b_ref, o_ref, *, alpha):
    tm, d = o_ref.shape
    n_sub = d // LANES
    z = jnp.concatenate(
        [alpha * x_ref[pl.ds(s, tm, stride=n_sub), :] + f_ref[pl.ds(s, tm, stride=n_sub), :]
         for s in range(n_sub)], axis=1)
    o_ref[...] = _layer_norm(z, g_ref[...], b_ref[...])


def _add_ln(x_tiles, ffn_tiles, t, d, ln_g, ln_b, alpha):
    n_sub = d // LANES
    tm = min(TOKEN_TILE, t)
    return pl.pallas_call(
        functools.partial(_add_ln_kernel, alpha=alpha),
        grid=(t // tm,),
        in_specs=[pl.BlockSpec((tm * n_sub, LANES), lambda i: (i, 0)),
                  pl.BlockSpec((tm * n_sub, LANES), lambda i: (i, 0)),
                  pl.BlockSpec((1, d), lambda i: (0, 0)),
                  pl.BlockSpec((1, d), lambda i: (0, 0))],
        out_specs=pl.BlockSpec((tm, d), lambda i: (i, 0)),
        out_shape=jax.ShapeDtypeStruct((t, d), F32),
        compiler_params=_params(1),
        name="add_ln",
    )(x_tiles, ffn_tiles, ln_g, ln_b)


def _lru_in_kernel(x_ref, w_ref, y_ref, r_ref):
    d = y_ref.shape[1]
    yz = jnp.dot(x_ref[...].astype(BF16), w_ref[...], preferred_element_type=F32)
    y = yz[:, :d]
    y_ref[...] = 0.5 * y * (1.0 + jnp.tanh(math.sqrt(2.0 / math.pi) * (y + 0.044715 * (y * y * y))))
    r_ref[...] = yz[:, d:]


def _lru_in(x, w_in):
    t, d = x.shape
    n = w_in.shape[1] // 2
    tm = min(TOKEN_TILE, t)
    return pl.pallas_call(
        _lru_in_kernel,
        grid=(t // tm,),
        in_specs=[pl.BlockSpec((tm, d), lambda i: (i, 0)),
                  pl.BlockSpec((d, 2 * n), lambda i: (0, 0))],
        out_specs=[pl.BlockSpec((tm, n), lambda i: (i, 0)),
                   pl.BlockSpec((tm, n), lambda i: (i, 0))],
        out_shape=[jax.ShapeDtypeStruct((t, n), F32), jax.ShapeDtypeStruct((t, n), F32)],
        compiler_params=_params(1),
        name="lru_in",
    )(x, w_in)


def _lru_core_kernel(r_ref, y_ref, cw_ref, cb_ref, wgx_ref, wga_ref, bgx_ref, bga_ref, ap_ref,
                     hy_ref, rprev_ref, carry_ref, a_ref, u_ref):
    t = pl.program_id(0)
    tt, d = r_ref.shape
    n_blk, bw, _ = wgx_ref.shape
    cwid = cw_ref.shape[0]

    @pl.when(t == 0)
    def _():
        rprev_ref[...] = jnp.zeros_like(rprev_ref)
        carry_ref[...] = jnp.zeros_like(carry_ref)

    r = r_ref[...]
    rext = jnp.concatenate([rprev_ref[...], r], axis=0)
    rc = jnp.broadcast_to(cb_ref[...], (tt, d))
    for w in range(cwid):
        off = 8 - (cwid - 1) + w
        rc = rc + cw_ref[w:w + 1, :] * rext[off:off + tt, :]
    rprev_ref[...] = r[tt - 8:, :]

    rcb = rc.astype(BF16)
    gx = jnp.concatenate(
        [jnp.dot(rcb[:, n * bw:(n + 1) * bw], wgx_ref[n], preferred_element_type=F32)
         for n in range(n_blk)], axis=1)
    ga = jnp.concatenate(
        [jnp.dot(rcb[:, n * bw:(n + 1) * bw], wga_ref[n], preferred_element_type=F32)
         for n in range(n_blk)], axis=1)
    gate_x = _sigmoid(gx + bgx_ref[...])
    gate_a = _sigmoid(ga + bga_ref[...])
    z = -ap_ref[...]
    softplus = jnp.maximum(z, 0.0) + jnp.log(1.0 + jnp.exp(-jnp.abs(z)))
    log_a = -LRU_C * gate_a * softplus
    a_ref[...] = jnp.exp(log_a)
    u_ref[...] = jnp.sqrt(1.0 - jnp.exp(2.0 * log_a)) * (gate_x * rc)

    row = lax.broadcasted_iota(I32, (SCAN_ROWS, d), 0)

    def slab(s, carry):
        off = pl.multiple_of(s * SCAN_ROWS, SCAN_ROWS)
        a = a_ref[pl.ds(off, SCAN_ROWS), :]
        b = u_ref[pl.ds(off, SCAN_ROWS), :]
        step = 1
        while step < SCAN_ROWS:
            a_sh = pltpu.roll(a, step, 0)
            b_sh = pltpu.roll(b, step, 0)
            live = row >= step
            b = jnp.where(live, a * b_sh + b, b)
            a = jnp.where(live, a * a_sh, a)
            step *= 2
        h = b + a * carry
        u_ref[pl.ds(off, SCAN_ROWS), :] = h
        return h[SCAN_ROWS - 1:SCAN_ROWS, :]

    carry_ref[...] = lax.fori_loop(0, tt // SCAN_ROWS, slab, carry_ref[...])
    hy_ref[...] = (u_ref[...] * y_ref[...]).astype(hy_ref.dtype)


def _lru_core(r, y, conv_w, conv_b, wgx, wga, bgx, bga, a_param):
    t, d = r.shape
    tt = min(TOKEN_TILE, t)
    row = lambda i: (i, 0)
    fixed2 = lambda i: (0, 0)
    fixed3 = lambda i: (0, 0, 0)
    return pl.pallas_call(
        _lru_core_kernel,
        grid=(t // tt,),
        in_specs=[pl.BlockSpec((tt, d), row),
                  pl.BlockSpec((tt, d), row),
                  pl.BlockSpec(conv_w.shape, fixed2),
                  pl.BlockSpec((1, d), fixed2),
                  pl.BlockSpec(wgx.shape, fixed3),
                  pl.BlockSpec(wga.shape, fixed3),
                  pl.BlockSpec((1, d), fixed2),
                  pl.BlockSpec((1, d), fixed2),
                  pl.BlockSpec((1, d), fixed2)],
        out_specs=pl.BlockSpec((tt, d), row),
        out_shape=jax.ShapeDtypeStruct((t, d), BF16),
        scratch_shapes=[pltpu.VMEM((8, d), F32),
                        pltpu.VMEM((1, d), F32),
                        pltpu.VMEM((tt, d), F32),
                        pltpu.VMEM((tt, d), F32)],
        compiler_params=_params(1),
        name="lru_core",
    )(r, y, conv_w, conv_b, wgx, wga, bgx, bga, a_param)


def _moe(xn_tiles, t, d, ids, gates, rank, counts128, layer, wg, wu, wd, bg, bu, bd):
    tm = EXPERT_ROWS
    n_exp = wg.shape[1]
    counts = counts128[:, 0].astype(I32)
    nch = (counts + tm - 1) // tm
    start = (jnp.cumsum(nch) - nch).astype(I32) * tm
    experts = jnp.arange(n_exp, dtype=I32)
    dest = rank + jnp.sum(jnp.where(ids[:, :, None] == experts, start, 0), axis=-1)
    last = ((jnp.sum(nch) - 1) * tm).astype(I32).reshape(1)
    tok_bits = (t + COMBINE_BATCH - 1).bit_length()
    gates_ext = jnp.pad(gates, ((0, 1), (0, (1 << tok_bits) - t))).reshape(-1)
    return _moe_experts(xn_tiles, t, d, tok_bits, dest.reshape(-1).astype(I32), gates_ext,
                        start, counts, nch.astype(I32), last, layer, wg, wu, wd, bg, bu, bd)


def kernel(x, rel_bias_table, w_qkv, w_attn_out, w_lru_in, lru_conv_w, lru_conv_b, w_lru_gate_x, b_lru_gate_x, w_lru_gate_a, b_lru_gate_a, lru_a_param, w_lru_out, w_router, b_router, w_exp_gate, b_exp_gate, w_exp_up, b_exp_up, w_exp_down, b_exp_down, ln_mix_g, ln_mix_b, ln_ffn_g, ln_ffn_b):
    bsz, t, d = x.shape
    depth = w_router.shape[0]
    n_mixers = 2
    alpha = (2 * depth) ** 0.25
    assert t % MOBA_BLOCK == 0 and t % TOKEN_TILE == 0

    biases = _attn_biases(rel_bias_table, t)
    w_qkv_b = w_qkv.astype(BF16)
    w_attn_out_b = w_attn_out.astype(BF16)
    w_lru_in_b = w_lru_in.astype(BF16)
    w_lru_out_b = w_lru_out.astype(BF16)
    wgx_b = w_lru_gate_x.astype(BF16)
    wga_b = w_lru_gate_a.astype(BF16)
    wg_b = w_exp_gate.astype(BF16)
    wu_b = w_exp_up.astype(BF16)
    wd_b = w_exp_down.astype(BF16)
    wr_t = jnp.swapaxes(w_router, 1, 2)
    wr_hi = wr_t.astype(BF16)
    wr_lo = (wr_t - wr_hi.astype(F32)).astype(BF16)

    bg4 = b_exp_gate[:, :, None, :]
    bu4 = b_exp_up[:, :, None, :]
    bd4 = b_exp_down[:, :, None, :]

    outs = []
    for b in range(bsz):
        xs = x[b]
        for i in range(depth):
            j = i // n_mixers
            if i % n_mixers == 0:
                qkv = _matmul(xs, w_qkv_b[j], BF16)
                a = _moba_attention(qkv, biases)
                w_out = w_attn_out_b[j]
            else:
                y, r = _lru_in(xs, w_lru_in_b[j])
                a = _lru_core(r, y, lru_conv_w[j], lru_conv_b[j][None], wgx_b[j], wga_b[j],
                              b_lru_gate_x[j][None], b_lru_gate_a[j][None], lru_a_param[j][None])
                w_out = w_lru_out_b[j]
            xn, ids, gates, rank, counts = _proj_ln_router(
                a, w_out, xs, ln_mix_g[i][None], ln_mix_b[i][None],
                wr_hi[i], wr_lo[i], b_router[i][:, None], alpha)
            ffn = _moe(xn, t, d, ids, gates, rank, counts, i, wg_b, wu_b, wd_b, bg4, bu4, bd4)
            xs = _add_ln(xn, ffn, t, d, ln_ffn_g[i][None], ln_ffn_b[i][None], alpha)
        outs.append(xs)
    return jnp.stack(outs, axis=0)
```

```python
import functools
import math

import numpy as np
import jax
import jax.numpy as jnp
from jax import lax
from jax.experimental import pallas as pl
from jax.experimental.pallas import tpu as pltpu

F32 = jnp.float32
BF16 = jnp.bfloat16
I32 = jnp.int32

N_HEADS = 8
MOBA_BLOCK = 256
MOBA_TOPK = 3
NUM_BUCKETS = 32
REL_MAX_DISTANCE = 128
LRU_C = 8.0
EXPERT_TOPK = 4
SWIGLU_LIMIT = 7.0
SWIGLU_ALPHA = 1.702
LN_EPS = 1e-5
LOG2_E = 1.4426950408889634

EXPERT_ROWS = 256
TOKEN_TILE = 512
SCAN_ROWS = 8
LANES = 128
HEAD_GROUP = 8
COMBINE_BATCH = 8
V7X_VMEM_LIMIT = 56 * 1024 * 1024

_NT = (((1,), (1,)), ((), ()))


def _params(n_axes, vmem=V7X_VMEM_LIMIT):
    return pltpu.CompilerParams(
        dimension_semantics=("arbitrary",) * n_axes, vmem_limit_bytes=vmem)


def _sigmoid(x):
    return 0.5 * jnp.tanh(0.5 * x) + 0.5


def _layer_norm(z, g, b):
    mu = jnp.mean(z, axis=-1, keepdims=True)
    zc = z - mu
    var = jnp.mean(zc * zc, axis=-1, keepdims=True)
    return zc * lax.rsqrt(var + LN_EPS) * g + b


def _matmul_kernel(x_ref, w_ref, o_ref):
    o_ref[...] = jnp.dot(x_ref[...].astype(BF16), w_ref[...],
                         preferred_element_type=F32).astype(o_ref.dtype)


def _matmul(x, w, out_dtype):
    m, k = x.shape
    n = w.shape[1]
    tm = min(TOKEN_TILE, m)
    return pl.pallas_call(
        _matmul_kernel,
        grid=(m // tm,),
        in_specs=[pl.BlockSpec((tm, k), lambda i: (i, 0)),
                  pl.BlockSpec((k, n), lambda i: (0, 0))],
        out_specs=pl.BlockSpec((tm, n), lambda i: (i, 0)),
        out_shape=jax.ShapeDtypeStruct((m, n), out_dtype),
        compiler_params=_params(1),
        name="dense_proj",
    )(x, w)


def _attn_kernel(q_ref, k_ref, v_ref, bself_ref, badj_ref, bfar_ref, o_ref,
                 kmean_ref, vt_ref, pen_ref, m_ref, l_ref, acc_ref, *, nblk, scale):
    blk = MOBA_BLOCK
    hg = bself_ref.shape[0]
    dh = q_ref.shape[1] // hg
    i = pl.program_id(1)

    def head(ref, h):
        return ref.at[:, h * dh:(h + 1) * dh]

    @pl.when(i == 0)
    def _():
        for h in range(hg):
            for j in range(nblk):
                kj = head(k_ref, h)[j * blk:(j + 1) * blk, :].astype(F32)
                kmean_ref[h, j:j + 1, :] = jnp.sum(kj, axis=0, keepdims=True) * (1.0 / blk)
                vt_ref[h, j] = head(v_ref, h)[j * blk:(j + 1) * blk, :].astype(F32).T.astype(BF16)

    qs = []
    for h in range(hg):
        q = head(q_ref, h)[...]

        km = kmean_ref[h]
        km_hi = km.astype(BF16)
        km_lo = (km - km_hi.astype(F32)).astype(BF16)
        gate = (lax.dot_general(km_hi, q, _NT, preferred_element_type=F32)
                + lax.dot_general(km_lo, q, _NT, preferred_element_type=F32))
        jj = lax.broadcasted_iota(I32, gate.shape, 0)
        past = jj < i
        g = jnp.where(past, gate, -jnp.inf)
        sel = jnp.zeros(gate.shape, jnp.bool_)
        for _ in range(min(MOBA_TOPK, nblk)):
            mx = jnp.max(g, axis=0, keepdims=True)
            jm = jnp.min(jnp.where(g == mx, jj, nblk), axis=0, keepdims=True)
            pick = jj == jm
            sel = jnp.logical_or(sel, pick)
            g = jnp.where(pick, -jnp.inf, g)
        pen_ref[h] = jnp.where(jnp.logical_and(sel, past), 0.0, -jnp.inf)
        qs.append((q.astype(F32) * (scale * LOG2_E)).astype(BF16))

    def scores(h, j):
        kb = head(k_ref, h)[pl.ds(pl.multiple_of(j * blk, blk), blk), :]
        return lax.dot_general(kb, qs[h], _NT, preferred_element_type=F32)

    def fold(x, op):
        return op(x.reshape(blk // 8, 8, blk), axis=0)

    def adj_bias(h):
        return badj_ref[h] + pen_ref[h, pl.ds(jnp.maximum(i - 1, 0), 1), :]

    def far_bias(h, j):
        return bfar_ref[h, 0:1, 0:1] + pen_ref[h, pl.ds(j, 1), :]

    n_far = jnp.maximum(i - 1, 0)


    ss = [scores(h, i) + bself_ref[h] for h in range(hg)]
    for h in range(hg):
        m0 = jnp.max(fold(ss[h], jnp.max), axis=0, keepdims=True)
        p = jnp.exp2(ss[h] - m0)
        m_ref[h] = m0
        l_ref[h] = fold(p, jnp.sum)
        acc_ref[h] = jnp.dot(vt_ref[h, i], p.astype(BF16), preferred_element_type=F32)

    def update(h, j, s, s_max, bias):
        if bias is not None:
            s_max = s_max + bias
        m_old = m_ref[h]
        m_new = jnp.maximum(m_old, jnp.max(s_max, axis=0, keepdims=True))
        alpha = jnp.exp2(m_old - m_new)
        p = jnp.exp2(s - m_new if bias is None else s + (bias - m_new))
        m_ref[h] = m_new
        l_ref[h] = alpha * l_ref[h] + fold(p, jnp.sum)
        acc_ref[h] = alpha * acc_ref[h] + jnp.dot(
            vt_ref[h, j], p.astype(BF16), preferred_element_type=F32)

    @pl.when(i >= 1)
    def _():
        ss = [scores(h, i - 1) + adj_bias(h) for h in range(hg)]
        for h in range(hg):
            update(h, i - 1, ss[h], fold(ss[h], jnp.max), None)

    def far(j, carry):
        ss = [scores(h, j) for h in range(hg)]
        for h in range(hg):
            update(h, j, ss[h], fold(ss[h], jnp.max), far_bias(h, j))
        return carry

    lax.fori_loop(0, n_far, far, 0)

    for h in range(hg):
        o = acc_ref[h] / jnp.sum(l_ref[h], axis=0, keepdims=True)
        head(o_ref, h)[...] = o.T.astype(o_ref.dtype)


def _t5_bucket_np(dist):
    max_exact = NUM_BUCKETS // 2
    n = dist.astype(np.float32)
    large = max_exact + (np.log(np.maximum(n, np.float32(1.0)) / np.float32(max_exact))
                         / np.float32(math.log(REL_MAX_DISTANCE / max_exact))
                         * np.float32(NUM_BUCKETS - max_exact)).astype(np.int32)
    large = np.minimum(large, NUM_BUCKETS - 1)
    return np.where(dist < max_exact, dist, large)


def _t5_bucket(dist):
    max_exact = NUM_BUCKETS // 2
    n = dist.astype(F32)
    large = max_exact + (jnp.log(jnp.maximum(n, 1.0) / max_exact)
                         / math.log(REL_MAX_DISTANCE / max_exact)
                         * (NUM_BUCKETS - max_exact)).astype(I32)
    large = jnp.minimum(large, NUM_BUCKETS - 1)
    return jnp.where(dist < max_exact, dist, large)


def _attn_biases(rel_bias_table, t):
    blk = MOBA_BLOCK
    far_buckets = _t5_bucket_np(np.arange(blk + 1, max(t, 2 * blk) + 1, dtype=np.int32))
    assert (far_buckets == NUM_BUCKETS - 1).all()
    bbd = rel_bias_table.astype(F32)[_t5_bucket(jnp.arange(2 * blk + 1, dtype=I32))].T * LOG2_E
    n_heads = bbd.shape[0]

    def toeplitz(w):
        rep = jnp.tile(w, (1, blk))[:, :blk * (2 * blk - 1)].reshape(n_heads, blk, 2 * blk - 1)
        return rep[:, :, blk - 1:]

    w_self = jnp.concatenate([jnp.full((n_heads, blk - 1), -jnp.inf, F32), bbd[:, :blk + 1]], axis=1)
    w_adj = jnp.concatenate([bbd[:, 1:2 * blk], bbd[:, :1]], axis=1)
    bself = toeplitz(w_self)
    badj = toeplitz(w_adj)
    bfar = jnp.broadcast_to(bbd[:, 2 * blk][:, None, None], (n_heads, 8, 128))
    return bself, badj, bfar


def _moba_attention(qkv, biases):
    t, d3 = qkv.shape
    d = d3 // 3
    dh = d // N_HEADS
    blk = MOBA_BLOCK
    nblk = t // blk
    bself, badj, bfar = biases
    hg = HEAD_GROUP
    n_grp = N_HEADS // hg
    kernel = functools.partial(_attn_kernel, nblk=nblk, scale=dh ** -0.5)
    return pl.pallas_call(
        kernel,
        grid=(n_grp, nblk),
        in_specs=[pl.BlockSpec((blk, hg * dh), lambda g, i: (i, g)),
                  pl.BlockSpec((t, hg * dh), lambda g, i: (0, n_grp + g)),
                  pl.BlockSpec((t, hg * dh), lambda g, i: (0, 2 * n_grp + g)),
                  pl.BlockSpec((hg, blk, blk), lambda g, i: (g, 0, 0)),
                  pl.BlockSpec((hg, blk, blk), lambda g, i: (g, 0, 0)),
                  pl.BlockSpec((hg, 8, LANES), lambda g, i: (g, 0, 0))],
        out_specs=pl.BlockSpec((blk, hg * dh), lambda g, i: (i, g)),
        out_shape=jax.ShapeDtypeStruct((t, d), BF16),
        scratch_shapes=[pltpu.VMEM((hg, nblk, dh), F32),
                        pltpu.VMEM((hg, nblk, dh, blk), BF16),
                        pltpu.VMEM((hg, nblk, blk), F32),
                        pltpu.VMEM((hg, 1, blk), F32),
                        pltpu.VMEM((hg, 8, blk), F32),
                        pltpu.VMEM((hg, dh, blk), F32)],
        compiler_params=_params(2),
        name="moba_attention",
    )(qkv, qkv, qkv, bself, badj, bfar)


def _proj_ln_router_kernel(a_ref, w_ref, x_ref, g_ref, b_ref, wrh_ref, wrl_ref, br_ref,
                           xn_ref, ids_ref, gates_ref, rank_ref, counts_ref, base_ref,
                           *, alpha, n_exp):
    t = pl.program_id(0)
    tm = a_ref.shape[0]

    @pl.when(t == 0)
    def _():
        base_ref[...] = jnp.zeros_like(base_ref)

    mix = jnp.dot(a_ref[...].astype(BF16), w_ref[...], preferred_element_type=F32)
    xn = _layer_norm(alpha * x_ref[...] + mix, g_ref[...], b_ref[...])
    n_sub = xn.shape[1] // LANES
    for s in range(n_sub):
        xn_ref[pl.ds(s, tm, stride=n_sub), :] = xn[:, s * LANES:(s + 1) * LANES]

    x_hi = xn.astype(BF16)
    x_lo = (xn - x_hi.astype(F32)).astype(BF16)
    wr_hi = wrh_ref[...]
    logits = (lax.dot_general(wr_hi, x_hi, _NT, preferred_element_type=F32)
              + lax.dot_general(wr_hi, x_lo, _NT, preferred_element_type=F32)
              + lax.dot_general(wrl_ref[...], x_hi, _NT, preferred_element_type=F32)
              + br_ref[...])

    e_iota = lax.broadcasted_iota(I32, logits.shape, 0)
    work = logits
    tops, picks = [], []
    for k in range(EXPERT_TOPK):
        mx = jnp.max(work, axis=0, keepdims=True)
        em = jnp.min(jnp.where(work == mx, e_iota, n_exp), axis=0, keepdims=True)
        pick = e_iota == em
        ids_ref[k:k + 1, :] = em
        tops.append(mx)
        picks.append(pick)
        work = jnp.where(pick, -jnp.inf, work)

    ex = [jnp.exp(tk - tops[0]) for tk in tops]
    den = ex[0]
    for e in ex[1:]:
        den = den + e
    for k in range(EXPERT_TOPK):
        gates_ref[k:k + 1, :] = ex[k] / den

    onehot = jnp.zeros(logits.shape, F32)
    for pick in picks:
        onehot = onehot + pick.astype(F32)
    earlier = (lax.broadcasted_iota(I32, (tm, tm), 0)
               < lax.broadcasted_iota(I32, (tm, tm), 1)).astype(BF16)
    pos = jnp.dot(onehot.astype(BF16), earlier, preferred_element_type=F32) + base_ref[:, 0:1]
    for k in range(EXPERT_TOPK):
        rank_ref[k:k + 1, :] = jnp.sum(jnp.where(picks[k], pos, 0.0), axis=0,
                                       keepdims=True).astype(I32)
    total = base_ref[...] + jnp.sum(onehot, axis=1, keepdims=True)
    base_ref[...] = total
    counts_ref[...] = total


def _proj_ln_router(a, w, x, ln_g, ln_b, wr_hi, wr_lo, br, alpha):
    t, d = x.shape
    n_exp = wr_hi.shape[0]
    tm = min(TOKEN_TILE, t)
    kernel = functools.partial(_proj_ln_router_kernel, alpha=alpha, n_exp=n_exp)
    row = lambda i: (i, 0)
    fixed = lambda i: (0, 0)
    col = lambda i: (0, i)
    return pl.pallas_call(
        kernel,
        grid=(t // tm,),
        in_specs=[pl.BlockSpec((tm, d), row),
                  pl.BlockSpec((d, d), fixed),
                  pl.BlockSpec((tm, d), row),
                  pl.BlockSpec((1, d), fixed),
                  pl.BlockSpec((1, d), fixed),
                  pl.BlockSpec((n_exp, d), fixed),
                  pl.BlockSpec((n_exp, d), fixed),
                  pl.BlockSpec((n_exp, 1), fixed)],
        out_specs=[pl.BlockSpec((tm * (d // LANES), LANES), row),
                   pl.BlockSpec((EXPERT_TOPK, tm), col),
                   pl.BlockSpec((EXPERT_TOPK, tm), col),
                   pl.BlockSpec((EXPERT_TOPK, tm), col),
                   pl.BlockSpec((n_exp, 128), fixed)],
        out_shape=[jax.ShapeDtypeStruct((t * (d // LANES), LANES), F32),
                   jax.ShapeDtypeStruct((EXPERT_TOPK, t), I32),
                   jax.ShapeDtypeStruct((EXPERT_TOPK, t), F32),
                   jax.ShapeDtypeStruct((EXPERT_TOPK, t), I32),
                   jax.ShapeDtypeStruct((n_exp, 128), F32)],
        scratch_shapes=[pltpu.VMEM((n_exp, 128), F32)],
        compiler_params=_params(1),
        name="proj_ln_router",
    )(a, w, x, ln_g, ln_b, wr_hi, wr_lo, br)


def _moe_kernel(dest_ref, gate_ref, start_ref, count_ref, nch_ref, last_ref,
                x2_ref, wg_ref, wu_ref, wd_ref, bg_ref, bu_ref, bd_ref,
                out_ref, src_ref, xg_ref, yf_ref, *, n_tok, n_exp, tok_bits):
    e = pl.program_id(0)
    n_sub = wg_ref.shape[0] // LANES
    tm = xg_ref.shape[0] // n_sub
    spare = COMBINE_BATCH

    def tile(ref, idx):
        return ref.at[pl.ds(pl.multiple_of(idx * n_sub, n_sub), n_sub), :]

    tok_mask = (1 << tok_bits) - 1

    def gather_row(base, r):
        tok = jnp.minimum(src_ref[base + r] & tok_mask, n_tok - 1)
        tile(xg_ref, r)[...] = tile(x2_ref, tok)[...]

    def combine_batch(base, r0):
        codes = [src_ref[base + r0 + k] for k in range(spare)]
        toks = [c & tok_mask for c in codes]
        gs = [gate_ref[c] for c in codes]
        olds = [tile(out_ref, toks[k])[...] for k in range(spare)]
        for k in range(spare):
            tile(out_ref, toks[k])[...] = olds[k] + gs[k] * tile(yf_ref, r0 + k)[...]

    @pl.when(e == 0)
    def _():
        out_ref[...] = jnp.zeros_like(out_ref)
        yf_ref[...] = jnp.zeros_like(yf_ref)

        def fill(r, carry):
            src_ref[r] = n_tok + r % spare + (EXPERT_TOPK << tok_bits)
            return carry

        for ee in range(n_exp):
            lax.fori_loop(start_ref[ee] + count_ref[ee], start_ref[ee] + nch_ref[ee] * tm, fill, 0)

        def scatter(a, carry):
            for k in range(EXPERT_TOPK):
                src_ref[dest_ref[k * n_tok + a]] = a + (k << tok_bits)
            return carry

        lax.fori_loop(0, n_tok, scatter, 0, unroll=8)

        def first_rows(r, carry):
            gather_row(0, r)
            return carry

        lax.fori_loop(0, tm, first_rows, 0, unroll=8)

    def chunk(ch, carry):
        base = start_ref[e] + ch * tm
        x = jnp.concatenate(
            [xg_ref[pl.ds(s, tm, stride=n_sub), :].astype(BF16) for s in range(n_sub)], axis=1)

        nxt = jnp.minimum(base + tm, last_ref[0])
        for r in range(tm):
            gather_row(nxt, r)
        prev = jnp.maximum(base - tm, 0)
        for r0 in range(0, tm, spare):
            combine_batch(prev, r0)

        g = jnp.dot(x, wg_ref[...], preferred_element_type=F32) + bg_ref[...]
        u = jnp.dot(x, wu_ref[...], preferred_element_type=F32) + bu_ref[...]
        g = jnp.minimum(g, SWIGLU_LIMIT)
        u = jnp.clip(u, -SWIGLU_LIMIT, SWIGLU_LIMIT)
        h = g * _sigmoid(SWIGLU_ALPHA * g) * (u + 1.0)
        y = jnp.dot(h.astype(BF16), wd_ref[...], preferred_element_type=F32) + bd_ref[...]
        for s in range(n_sub):
            yf_ref[pl.ds(s, tm, stride=n_sub), :] = y[:, s * LANES:(s + 1) * LANES]
        return carry

    lax.fori_loop(0, nch_ref[e], chunk, 0)

    @pl.when(e == n_exp - 1)
    def _():
        def last_rows(b, carry):
            combine_batch(last_ref[0], b * spare)
            return carry

        lax.fori_loop(0, tm // spare, last_rows, 0)


def _moe_experts(x_tiles, t, d, tok_bits, dest, gates, start, count, nch, last, layer,
                 wg, wu, wd, bg, bu, bd):
    n_sub = d // LANES
    tm = EXPERT_ROWS
    n_exp, dff = wg.shape[1], wg.shape[3]
    n_rows = t * EXPERT_TOPK + n_exp * tm

    def w_map(e, *_):
        return (layer, e, 0, 0)

    out_rows = (t + COMBINE_BATCH) * n_sub
    grid_spec = pltpu.PrefetchScalarGridSpec(
        num_scalar_prefetch=6,
        grid=(n_exp,),
        in_specs=[pl.BlockSpec((t * n_sub, LANES), lambda e, *_: (0, 0)),
                  pl.BlockSpec((None, None, d, dff), w_map),
                  pl.BlockSpec((None, None, d, dff), w_map),
                  pl.BlockSpec((None, None, dff, d), w_map),
                  pl.BlockSpec((None, None, 1, dff), w_map),
                  pl.BlockSpec((None, None, 1, dff), w_map),
                  pl.BlockSpec((None, None, 1, d), w_map)],
        out_specs=pl.BlockSpec((out_rows, LANES), lambda e, *_: (0, 0)),
        scratch_shapes=[pltpu.SMEM((n_rows,), I32),
                        pltpu.VMEM((tm * n_sub, LANES), F32),
                        pltpu.VMEM((tm * n_sub, LANES), F32)],
    )
    return pl.pallas_call(
        functools.partial(_moe_kernel, n_tok=t, n_exp=n_exp, tok_bits=tok_bits),
        grid_spec=grid_spec,
        out_shape=jax.ShapeDtypeStruct((out_rows, LANES), F32),
        compiler_params=_params(1),
        name="moe_experts",
    )(dest, gates, start, count, nch, last, x_tiles, wg, wu, wd, bg, bu, bd)


def _add_ln_kernel(x_ref, f_ref, g_ref, b_ref, o_ref, *, alpha):
    tm, d = o_ref.shape
    n_sub = d // LANES
    z = jnp.concatenate(
        [alpha * x_ref[pl.ds(s, tm, stride=n_sub), :] + f_ref[pl.ds(s, tm, stride=n_sub), :]
         for s in range(n_sub)], axis=1)
    o_ref[...] = _layer_norm(z, g_ref[...], b_ref[...])


def _add_ln(x_tiles, ffn_tiles, t, d, ln_g, ln_b, alpha):
    n_sub = d // LANES
    tm = min(TOKEN_TILE, t)
    return pl.pallas_call(
        functools.partial(_add_ln_kernel, alpha=alpha),
        grid=(t // tm,),
        in_specs=[pl.BlockSpec((tm * n_sub, LANES), lambda i: (i, 0)),
                  pl.BlockSpec((tm * n_sub, LANES), lambda i: (i, 0)),
                  pl.BlockSpec((1, d), lambda i: (0, 0)),
                  pl.BlockSpec((1, d), lambda i: (0, 0))],
        out_specs=pl.BlockSpec((tm, d), lambda i: (i, 0)),
        out_shape=jax.ShapeDtypeStruct((t, d), F32),
        compiler_params=_params(1),
        name="add_ln",
    )(x_tiles, ffn_tiles, ln_g, ln_b)


def _lru_in_kernel(x_ref, w_ref, y_ref, r_ref):
    d = y_ref.shape[1]
    yz = jnp.dot(x_ref[...].astype(BF16), w_ref[...], preferred_element_type=F32)
    y = yz[:, :d]
    y_ref[...] = 0.5 * y * (1.0 + jnp.tanh(math.sqrt(2.0 / math.pi) * (y + 0.044715 * (y * y * y))))
    r_ref[...] = yz[:, d:]


def _lru_in(x, w_in):
    t, d = x.shape
    n = w_in.shape[1] // 2
    tm = min(TOKEN_TILE, t)
    return pl.pallas_call(
        _lru_in_kernel,
        grid=(t // tm,),
        in_specs=[pl.BlockSpec((tm, d), lambda i: (i, 0)),
                  pl.BlockSpec((d, 2 * n), lambda i: (0, 0))],
        out_specs=[pl.BlockSpec((tm, n), lambda i: (i, 0)),
                   pl.BlockSpec((tm, n), lambda i: (i, 0))],
        out_shape=[jax.ShapeDtypeStruct((t, n), F32), jax.ShapeDtypeStruct((t, n), F32)],
        compiler_params=_params(1),
        name="lru_in",
    )(x, w_in)


def _lru_core_kernel(r_ref, y_ref, cw_ref, cb_ref, wgx_ref, wga_ref, bgx_ref, bga_ref, ap_ref,
                     hy_ref, rprev_ref, carry_ref, a_ref, u_ref):
    t = pl.program_id(0)
    tt, d = r_ref.shape
    n_blk, bw, _ = wgx_ref.shape
    cwid = cw_ref.shape[0]

    @pl.when(t == 0)
    def _():
        rprev_ref[...] = jnp.zeros_like(rprev_ref)
        carry_ref[...] = jnp.zeros_like(carry_ref)

    r = r_ref[...]
    rext = jnp.concatenate([rprev_ref[...], r], axis=0)
    rc = jnp.broadcast_to(cb_ref[...], (tt, d))
    for w in range(cwid):
        off = 8 - (cwid - 1) + w
        rc = rc + cw_ref[w:w + 1, :] * rext[off:off + tt, :]
    rprev_ref[...] = r[tt - 8:, :]

    rcb = rc.astype(BF16)
    gx = jnp.concatenate(
        [jnp.dot(rcb[:, n * bw:(n + 1) * bw], wgx_ref[n], preferred_element_type=F32)
         for n in range(n_blk)], axis=1)
    ga = jnp.concatenate(
        [jnp.dot(rcb[:, n * bw:(n + 1) * bw], wga_ref[n], preferred_element_type=F32)
         for n in range(n_blk)], axis=1)
    gate_x = _sigmoid(gx + bgx_ref[...])
    gate_a = _sigmoid(ga + bga_ref[...])
    z = -ap_ref[...]
    softplus = jnp.maximum(z, 0.0) + jnp.log(1.0 + jnp.exp(-jnp.abs(z)))
    log_a = -LRU_C * gate_a * softplus
    a_ref[...] = jnp.exp(log_a)
    u_ref[...] = jnp.sqrt(1.0 - jnp.exp(2.0 * log_a)) * (gate_x * rc)

    row = lax.broadcasted_iota(I32, (SCAN_ROWS, d), 0)

    def slab(s, carry):
        off = pl.multiple_of(s * SCAN_ROWS, SCAN_ROWS)
        a = a_ref[pl.ds(off, SCAN_ROWS), :]
        b = u_ref[pl.ds(off, SCAN_ROWS), :]
        step = 1
        while step < SCAN_ROWS:
            a_sh = pltpu.roll(a, step, 0)
            b_sh = pltpu.roll(b, step, 0)
            live = row >= step
            b = jnp.where(live, a * b_sh + b, b)
            a = jnp.where(live, a * a_sh, a)
            step *= 2
        h = b + a * carry
        u_ref[pl.ds(off, SCAN_ROWS), :] = h
        return h[SCAN_ROWS - 1:SCAN_ROWS, :]

    carry_ref[...] = lax.fori_loop(0, tt // SCAN_ROWS, slab, carry_ref[...])
    hy_ref[...] = (u_ref[...] * y_ref[...]).astype(hy_ref.dtype)


def _lru_core(r, y, conv_w, conv_b, wgx, wga, bgx, bga, a_param):
    t, d = r.shape
    tt = min(TOKEN_TILE, t)
    row = lambda i: (i, 0)
    fixed2 = lambda i: (0, 0)
    fixed3 = lambda i: (0, 0, 0)
    return pl.pallas_call(
        _lru_core_kernel,
        grid=(t // tt,),
        in_specs=[pl.BlockSpec((tt, d), row),
                  pl.BlockSpec((tt, d), row),
                  pl.BlockSpec(conv_w.shape, fixed2),
                  pl.BlockSpec((1, d), fixed2),
                  pl.BlockSpec(wgx.shape, fixed3),
                  pl.BlockSpec(wga.shape, fixed3),
                  pl.BlockSpec((1, d), fixed2),
                  pl.BlockSpec((1, d), fixed2),
                  pl.BlockSpec((1, d), fixed2)],
        out_specs=pl.BlockSpec((tt, d), row),
        out_shape=jax.ShapeDtypeStruct((t, d), BF16),
        scratch_shapes=[pltpu.VMEM((8, d), F32),
                        pltpu.VMEM((1, d), F32),
                        pltpu.VMEM((tt, d), F32),
                        pltpu.VMEM((tt, d), F32)],
        compiler_params=_params(1),
        name="lru_core",
    )(r, y, conv_w, conv_b, wgx, wga, bgx, bga, a_param)


def _moe(xn_tiles, t, d, ids, gates, rank, counts128, layer, wg, wu, wd, bg, bu, bd):
    tm = EXPERT_ROWS
    n_exp = wg.shape[1]
    counts = counts128[:, 0].astype(I32)
    nch = (counts + tm - 1) // tm
    start = (jnp.cumsum(nch) - nch).astype(I32) * tm
    experts = jnp.arange(n_exp, dtype=I32)
    dest = rank + jnp.sum(jnp.where(ids[:, :, None] == experts, start, 0), axis=-1)
    last = ((jnp.sum(nch) - 1) * tm).astype(I32).reshape(1)
    tok_bits = (t + COMBINE_BATCH - 1).bit_length()
    gates_ext = jnp.pad(gates, ((0, 1), (0, (1 << tok_bits) - t))).reshape(-1)
    return _moe_experts(xn_tiles, t, d, tok_bits, dest.reshape(-1).astype(I32), gates_ext,
                        start, counts, nch.astype(I32), last, layer, wg, wu, wd, bg, bu, bd)


def kernel(x, rel_bias_table, w_qkv, w_attn_out, w_lru_in, lru_conv_w, lru_conv_b, w_lru_gate_x, b_lru_gate_x, w_lru_gate_a, b_lru_gate_a, lru_a_param, w_lru_out, w_router, b_router, w_exp_gate, b_exp_gate, w_exp_up, b_exp_up, w_exp_down, b_exp_down, ln_mix_g, ln_mix_b, ln_ffn_g, ln_ffn_b):
    bsz, t, d = x.shape
    depth = w_router.shape[0]
    n_mixers = 2
    alpha = (2 * depth) ** 0.25
    assert t % MOBA_BLOCK == 0 and t % TOKEN_TILE == 0

    biases = _attn_biases(rel_bias_table, t)
    w_qkv_b = w_qkv.astype(BF16)
    w_attn_out_b = w_attn_out.astype(BF16)
    w_lru_in_b = w_lru_in.astype(BF16)
    w_lru_out_b = w_lru_out.astype(BF16)
    wgx_b = w_lru_gate_x.astype(BF16)
    wga_b = w_lru_gate_a.astype(BF16)
    wg_b = w_exp_gate.astype(BF16)
    wu_b = w_exp_up.astype(BF16)
    wd_b = w_exp_down.astype(BF16)
    wr_t = jnp.swapaxes(w_router, 1, 2)
    wr_hi = wr_t.astype(BF16)
    wr_lo = (wr_t - wr_hi.astype(F32)).astype(BF16)

    bg4 = b_exp_gate[:, :, None, :]
    bu4 = b_exp_up[:, :, None, :]
    bd4 = b_exp_down[:, :, None, :]

    outs = []
    for b in range(bsz):
        xs = x[b]
        for i in range(depth):
            j = i // n_mixers
            if i % n_mixers == 0:
                qkv = _matmul(xs, w_qkv_b[j], BF16)
                a = _moba_attention(qkv, biases)
                w_out = w_attn_out_b[j]
            else:
                y, r = _lru_in(xs, w_lru_in_b[j])
                a = _lru_core(r, y, lru_conv_w[j], lru_conv_b[j][None], wgx_b[j], wga_b[j],
                              b_lru_gate_x[j][None], b_lru_gate_a[j][None], lru_a_param[j][None])
                w_out = w_lru_out_b[j]
            xn, ids, gates, rank, counts = _proj_ln_router(
                a, w_out, xs, ln_mix_g[i][None], ln_mix_b[i][None],
                wr_hi[i], wr_lo[i], b_router[i][:, None], alpha)
            ffn = _moe(xn, t, d, ids, gates, rank, counts, i, wg_b, wu_b, wd_b, bg4, bu4, bd4)
            xs = _add_ln(xn, ffn, t, d, ln_ffn_g[i][None], ln_ffn_b[i][None], alpha)
        outs.append(xs)
    return jnp.stack(outs, axis=0)
```

```python
import functools
import math

import numpy as np
import jax
import jax.numpy as jnp
from jax import lax
from jax.experimental import pallas as pl
from jax.experimental.pallas import tpu as pltpu

F32 = jnp.float32
BF16 = jnp.bfloat16
I32 = jnp.int32

N_HEADS = 8
MOBA_BLOCK = 256
MOBA_TOPK = 3
NUM_BUCKETS = 32
REL_MAX_DISTANCE = 128
LRU_C = 8.0
EXPERT_TOPK = 4
SWIGLU_LIMIT = 7.0
SWIGLU_ALPHA = 1.702
LN_EPS = 1e-5
LOG2_E = 1.4426950408889634

EXPERT_ROWS = 256
TOKEN_TILE = 512
SCAN_ROWS = 8
LANES = 128
HEAD_GROUP = 8
COMBINE_BATCH = 8
V7X_VMEM_LIMIT = 56 * 1024 * 1024

_NT = (((1,), (1,)), ((), ()))


def _params(n_axes, vmem=V7X_VMEM_LIMIT):
    return pltpu.CompilerParams(
        dimension_semantics=("arbitrary",) * n_axes, vmem_limit_bytes=vmem)


def _sigmoid(x):
    return 0.5 * jnp.tanh(0.5 * x) + 0.5


def _layer_norm(z, g, b):
    mu = jnp.mean(z, axis=-1, keepdims=True)
    zc = z - mu
    var = jnp.mean(zc * zc, axis=-1, keepdims=True)
    return zc * lax.rsqrt(var + LN_EPS) * g + b


def _matmul_kernel(x_ref, w_ref, o_ref):
    o_ref[...] = jnp.dot(x_ref[...].astype(BF16), w_ref[...],
                         preferred_element_type=F32).astype(o_ref.dtype)


def _matmul(x, w, out_dtype):
    m, k = x.shape
    n = w.shape[1]
    tm = min(TOKEN_TILE, m)
    return pl.pallas_call(
        _matmul_kernel,
        grid=(m // tm,),
        in_specs=[pl.BlockSpec((tm, k), lambda i: (i, 0)),
                  pl.BlockSpec((k, n), lambda i: (0, 0))],
        out_specs=pl.BlockSpec((tm, n), lambda i: (i, 0)),
        out_shape=jax.ShapeDtypeStruct((m, n), out_dtype),
        compiler_params=_params(1),
        name="dense_proj",
    )(x, w)


def _attn_kernel(q_ref, k_ref, v_ref, bself_ref, badj_ref, bfar_ref, o_ref,
                 kmean_ref, vt_ref, pen_ref, m_ref, l_ref, acc_ref, *, nblk, scale):
    blk = MOBA_BLOCK
    hg = bself_ref.shape[0]
    dh = q_ref.shape[1] // hg
    i = pl.program_id(1)

    def head(ref, h):
        return ref.at[:, h * dh:(h + 1) * dh]

    @pl.when(i == 0)
    def _():
        for h in range(hg):
            for j in range(nblk):
                kj = head(k_ref, h)[j * blk:(j + 1) * blk, :].astype(F32)
                kmean_ref[h, j:j + 1, :] = jnp.sum(kj, axis=0, keepdims=True) * (1.0 / blk)
                vt_ref[h, j] = head(v_ref, h)[j * blk:(j + 1) * blk, :].astype(F32).T.astype(BF16)

    qs = []
    for h in range(hg):
        q = head(q_ref, h)[...]

        km = kmean_ref[h]
        km_hi = km.astype(BF16)
        km_lo = (km - km_hi.astype(F32)).astype(BF16)
        gate = (lax.dot_general(km_hi, q, _NT, preferred_element_type=F32)
                + lax.dot_general(km_lo, q, _NT, preferred_element_type=F32))
        jj = lax.broadcasted_iota(I32, gate.shape, 0)
        past = jj < i
        g = jnp.where(past, gate, -jnp.inf)
        sel = jnp.zeros(gate.shape, jnp.bool_)
        for _ in range(min(MOBA_TOPK, nblk)):
            mx = jnp.max(g, axis=0, keepdims=True)
            jm = jnp.min(jnp.where(g == mx, jj, nblk), axis=0, keepdims=True)
            pick = jj == jm
            sel = jnp.logical_or(sel, pick)
            g = jnp.where(pick, -jnp.inf, g)
        pen_ref[h] = jnp.where(jnp.logical_and(sel, past), 0.0, -jnp.inf)
        qs.append((q.astype(F32) * (scale * LOG2_E)).astype(BF16))

    def scores(h, j):
        kb = head(k_ref, h)[pl.ds(pl.multiple_of(j * blk, blk), blk), :]
        return lax.dot_general(kb, qs[h], _NT, preferred_element_type=F32)

    def fold(x, op):
        return op(x.reshape(blk // 8, 8, blk), axis=0)

    def adj_bias(h):
        return badj_ref[h] + pen_ref[h, pl.ds(jnp.maximum(i - 1, 0), 1), :]

    def far_bias(h, j):
        return bfar_ref[h, 0:1, 0:1] + pen_ref[h, pl.ds(j, 1), :]

    n_far = jnp.maximum(i - 1, 0)


    ss = [scores(h, i) + bself_ref[h] for h in range(hg)]
    for h in range(hg):
        m0 = jnp.max(fold(ss[h], jnp.max), axis=0, keepdims=True)
        p = jnp.exp2(ss[h] - m0)
        m_ref[h] = m0
        l_ref[h] = fold(p, jnp.sum)
        acc_ref[h] = jnp.dot(vt_ref[h, i], p.astype(BF16), preferred_element_type=F32)

    def update(h, j, s, s_max, bias):
        if bias is not None:
            s_max = s_max + bias
        m_old = m_ref[h]
        m_new = jnp.maximum(m_old, jnp.max(s_max, axis=0, keepdims=True))
        alpha = jnp.exp2(m_old - m_new)
        p = jnp.exp2(s - m_new if bias is None else s + (bias - m_new))
        m_ref[h] = m_new
        l_ref[h] = alpha * l_ref[h] + fold(p, jnp.sum)
        acc_ref[h] = alpha * acc_ref[h] + jnp.dot(
            vt_ref[h, j], p.astype(BF16), preferred_element_type=F32)

    @pl.when(i >= 1)
    def _():
        ss = [scores(h, i - 1) + adj_bias(h) for h in range(hg)]
        for h in range(hg):
            update(h, i - 1, ss[h], fold(ss[h], jnp.max), None)

    def far(j, carry):
        ss = [scores(h, j) for h in range(hg)]
        for h in range(hg):
            update(h, j, ss[h], fold(ss[h], jnp.max), far_bias(h, j))
        return carry

    lax.fori_loop(0, n_far, far, 0)

    for h in range(hg):
        o = acc_ref[h] / jnp.sum(l_ref[h], axis=0, keepdims=True)
        head(o_ref, h)[...] = o.T.astype(o_ref.dtype)


def _t5_bucket_np(dist):
    max_exact = NUM_BUCKETS // 2
    n = dist.astype(np.float32)
    large = max_exact + (np.log(np.maximum(n, np.float32(1.0)) / np.float32(max_exact))
                         / np.float32(math.log(REL_MAX_DISTANCE / max_exact))
                         * np.float32(NUM_BUCKETS - max_exact)).astype(np.int32)
    large = np.minimum(large, NUM_BUCKETS - 1)
    return np.where(dist < max_exact, dist, large)


def _t5_bucket(dist):
    max_exact = NUM_BUCKETS // 2
    n = dist.astype(F32)
    large = max_exact + (jnp.log(jnp.maximum(n, 1.0) / max_exact)
                         / math.log(REL_MAX_DISTANCE / max_exact)
                         * (NUM_BUCKETS - max_exact)).astype(I32)
    large = jnp.minimum(large, NUM_BUCKETS - 1)
    return jnp.where(dist < max_exact, dist, large)


def _attn_biases(rel_bias_table, t):
    blk = MOBA_BLOCK
    far_buckets = _t5_bucket_np(np.arange(blk + 1, max(t, 2 * blk) + 1, dtype=np.int32))
    assert (far_buckets == NUM_BUCKETS - 1).all()
    bbd = rel_bias_table.astype(F32)[_t5_bucket(jnp.arange(2 * blk + 1, dtype=I32))].T * LOG2_E
    n_heads = bbd.shape[0]

    def toeplitz(w):
        rep = jnp.tile(w, (1, blk))[:, :blk * (2 * blk - 1)].reshape(n_heads, blk, 2 * blk - 1)
        return rep[:, :, blk - 1:]

    w_self = jnp.concatenate([jnp.full((n_heads, blk - 1), -jnp.inf, F32), bbd[:, :blk + 1]], axis=1)
    w_adj = jnp.concatenate([bbd[:, 1:2 * blk], bbd[:, :1]], axis=1)
    bself = toeplitz(w_self)
    badj = toeplitz(w_adj)
    bfar = jnp.broadcast_to(bbd[:, 2 * blk][:, None, None], (n_heads, 8, 128))
    return bself, badj, bfar


def _moba_attention(qkv, biases):
    t, d3 = qkv.shape
    d = d3 // 3
    dh = d // N_HEADS
    blk = MOBA_BLOCK
    nblk = t // blk
    bself, badj, bfar = biases
    hg = HEAD_GROUP
    n_grp = N_HEADS // hg
    kernel = functools.partial(_attn_kernel, nblk=nblk, scale=dh ** -0.5)
    return pl.pallas_call(
        kernel,
        grid=(n_grp, nblk),
        in_specs=[pl.BlockSpec((blk, hg * dh), lambda g, i: (i, g)),
                  pl.BlockSpec((t, hg * dh), lambda g, i: (0, n_grp + g)),
                  pl.BlockSpec((t, hg * dh), lambda g, i: (0, 2 * n_grp + g)),
                  pl.BlockSpec((hg, blk, blk), lambda g, i: (g, 0, 0)),
                  pl.BlockSpec((hg, blk, blk), lambda g, i: (g, 0, 0)),
                  pl.BlockSpec((hg, 8, LANES), lambda g, i: (g, 0, 0))],
        out_specs=pl.BlockSpec((blk, hg * dh), lambda g, i: (i, g)),
        out_shape=jax.ShapeDtypeStruct((t, d), BF16),
        scratch_shapes=[pltpu.VMEM((hg, nblk, dh), F32),
                        pltpu.VMEM((hg, nblk, dh, blk), BF16),
                        pltpu.VMEM((hg, nblk, blk), F32),
                        pltpu.VMEM((hg, 1, blk), F32),
                        pltpu.VMEM((hg, 8, blk), F32),
                        pltpu.VMEM((hg, dh, blk), F32)],
        compiler_params=_params(2),
        name="moba_attention",
    )(qkv, qkv, qkv, bself, badj, bfar)


def _proj_ln_router_kernel(a_ref, w_ref, x_ref, g_ref, b_ref, wrh_ref, wrl_ref, br_ref,
                           xn_ref, ids_ref, gates_ref, rank_ref, counts_ref, base_ref,
                           *, alpha, n_exp):
    t = pl.program_id(0)
    tm = a_ref.shape[0]

    @pl.when(t == 0)
    def _():
        base_ref[...] = jnp.zeros_like(base_ref)

    mix = jnp.dot(a_ref[...].astype(BF16), w_ref[...], preferred_element_type=F32)
    xn = _layer_norm(alpha * x_ref[...] + mix, g_ref[...], b_ref[...])
    n_sub = xn.shape[1] // LANES
    for s in range(n_sub):
        xn_ref[pl.ds(s, tm, stride=n_sub), :] = xn[:, s * LANES:(s + 1) * LANES]

    x_hi = xn.astype(BF16)
    x_lo = (xn - x_hi.astype(F32)).astype(BF16)
    wr_hi = wrh_ref[...]
    logits = (lax.dot_general(wr_hi, x_hi, _NT, preferred_element_type=F32)
              + lax.dot_general(wr_hi, x_lo, _NT, preferred_element_type=F32)
              + lax.dot_general(wrl_ref[...], x_hi, _NT, preferred_element_type=F32)
              + br_ref[...])

    e_iota = lax.broadcasted_iota(I32, logits.shape, 0)
    work = logits
    tops, picks = [], []
    for k in range(EXPERT_TOPK):
        mx = jnp.max(work, axis=0, keepdims=True)
        em = jnp.min(jnp.where(work == mx, e_iota, n_exp), axis=0, keepdims=True)
        pick = e_iota == em
        ids_ref[k:k + 1, :] = em
        tops.append(mx)
        picks.append(pick)
        work = jnp.where(pick, -jnp.inf, work)

    ex = [jnp.exp(tk - tops[0]) for tk in tops]
    den = ex[0]
    for e in ex[1:]:
        den = den + e
    for k in range(EXPERT_TOPK):
        gates_ref[k:k + 1, :] = ex[k] / den

    onehot = jnp.zeros(logits.shape, F32)
    for pick in picks:
        onehot = onehot + pick.astype(F32)
    earlier = (lax.broadcasted_iota(I32, (tm, tm), 0)
               < lax.broadcasted_iota(I32, (tm, tm), 1)).astype(BF16)
    pos = jnp.dot(onehot.astype(BF16), earlier, preferred_element_type=F32) + base_ref[:, 0:1]
    for k in range(EXPERT_TOPK):
        rank_ref[k:k + 1, :] = jnp.sum(jnp.where(picks[k], pos, 0.0), axis=0,
                                       keepdims=True).astype(I32)
    total = base_ref[...] + jnp.sum(onehot, axis=1, keepdims=True)
    base_ref[...] = total
    counts_ref[...] = total


def _proj_ln_router(a, w, x, ln_g, ln_b, wr_hi, wr_lo, br, alpha):
    t, d = x.shape
    n_exp = wr_hi.shape[0]
    tm = min(TOKEN_TILE, t)
    kernel = functools.partial(_proj_ln_router_kernel, alpha=alpha, n_exp=n_exp)
    row = lambda i: (i, 0)
    fixed = lambda i: (0, 0)
    col = lambda i: (0, i)
    return pl.pallas_call(
        kernel,
        grid=(t // tm,),
        in_specs=[pl.BlockSpec((tm, d), row),
                  pl.BlockSpec((d, d), fixed),
                  pl.BlockSpec((tm, d), row),
                  pl.BlockSpec((1, d), fixed),
                  pl.BlockSpec((1, d), fixed),
                  pl.BlockSpec((n_exp, d), fixed),
                  pl.BlockSpec((n_exp, d), fixed),
                  pl.BlockSpec((n_exp, 1), fixed)],
        out_specs=[pl.BlockSpec((tm * (d // LANES), LANES), row),
                   pl.BlockSpec((EXPERT_TOPK, tm), col),
                   pl.BlockSpec((EXPERT_TOPK, tm), col),
                   pl.BlockSpec((EXPERT_TOPK, tm), col),
                   pl.BlockSpec((n_exp, 128), fixed)],
        out_shape=[jax.ShapeDtypeStruct((t * (d // LANES), LANES), F32),
                   jax.ShapeDtypeStruct((EXPERT_TOPK, t), I32),
                   jax.ShapeDtypeStruct((EXPERT_TOPK, t), F32),
                   jax.ShapeDtypeStruct((EXPERT_TOPK, t), I32),
                   jax.ShapeDtypeStruct((n_exp, 128), F32)],
        scratch_shapes=[pltpu.VMEM((n_exp, 128), F32)],
        compiler_params=_params(1),
        name="proj_ln_router",
    )(a, w, x, ln_g, ln_b, wr_hi, wr_lo, br)


def _moe_kernel(dest_ref, gate_ref, start_ref, count_ref, nch_ref, last_ref,
                x2_ref, wg_ref, wu_ref, wd_ref, bg_ref, bu_ref, bd_ref,
                out_ref, src_ref, xg_ref, yf_ref, *, n_tok, n_exp, tok_bits):
    e = pl.program_id(0)
    n_sub = wg_ref.shape[0] // LANES
    tm = xg_ref.shape[0] // n_sub
    spare = COMBINE_BATCH

    def tile(ref, idx):
        return ref.at[pl.ds(pl.multiple_of(idx * n_sub, n_sub), n_sub), :]

    tok_mask = (1 << tok_bits) - 1

    def gather_row(base, r):
        tok = jnp.minimum(src_ref[base + r] & tok_mask, n_tok - 1)
        tile(xg_ref, r)[...] = tile(x2_ref, tok)[...]

    def combine_batch(base, r0):
        codes = [src_ref[base + r0 + k] for k in range(spare)]
        toks = [c & tok_mask for c in codes]
        gs = [gate_ref[c] for c in codes]
        olds = [tile(out_ref, toks[k])[...] for k in range(spare)]
        for k in range(spare):
            tile(out_ref, toks[k])[...] = olds[k] + gs[k] * tile(yf_ref, r0 + k)[...]

    @pl.when(e == 0)
    def _():
        out_ref[...] = jnp.zeros_like(out_ref)
        yf_ref[...] = jnp.zeros_like(yf_ref)

        def fill(r, carry):
            src_ref[r] = n_tok + r % spare + (EXPERT_TOPK << tok_bits)
            return carry

        for ee in range(n_exp):
            lax.fori_loop(start_ref[ee] + count_ref[ee], start_ref[ee] + nch_ref[ee] * tm, fill, 0)

        def scatter(a, carry):
            for k in range(EXPERT_TOPK):
                src_ref[dest_ref[k * n_tok + a]] = a + (k << tok_bits)
            return carry

        lax.fori_loop(0, n_tok, scatter, 0, unroll=8)

        def first_rows(r, carry):
            gather_row(0, r)
            return carry

        lax.fori_loop(0, tm, first_rows, 0, unroll=8)

    def visit(base, rows):
        x = jnp.concatenate(
            [xg_ref[pl.ds(s, rows, stride=n_sub), :].astype(BF16) for s in range(n_sub)], axis=1)

        nxt = jnp.minimum(base + tm, last_ref[0])
        for r in range(tm):
            gather_row(nxt, r)
        prev = jnp.maximum(base - tm, 0)
        for r0 in range(0, tm, spare):
            combine_batch(prev, r0)

        g = jnp.dot(x, wg_ref[...], preferred_element_type=F32) + bg_ref[...]
        u = jnp.dot(x, wu_ref[...], preferred_element_type=F32) + bu_ref[...]
        g = jnp.minimum(g, SWIGLU_LIMIT)
        u = jnp.clip(u, -SWIGLU_LIMIT, SWIGLU_LIMIT)
        h = g * _sigmoid(SWIGLU_ALPHA * g) * (u + 1.0)
        y = jnp.dot(h.astype(BF16), wd_ref[...], preferred_element_type=F32) + bd_ref[...]
        for s in range(n_sub):
            yf_ref[pl.ds(s, rows, stride=n_sub), :] = y[:, s * LANES:(s + 1) * LANES]

    def chunk(ch, carry):
        base = start_ref[e] + ch * tm
        n_real = count_ref[e] - ch * tm

        @pl.when(n_real > tm // 2)
        def _():
            visit(base, tm)

        @pl.when(n_real <= tm // 2)
        def _():
            visit(base, tm // 2)

        return carry

    lax.fori_loop(0, nch_ref[e], chunk, 0)

    @pl.when(e == n_exp - 1)
    def _():
        def last_rows(b, carry):
            combine_batch(last_ref[0], b * spare)
            return carry

        lax.fori_loop(0, tm // spare, last_rows, 0)


def _moe_experts(x_tiles, t, d, tok_bits, dest, gates, start, count, nch, last, layer,
                 wg, wu, wd, bg, bu, bd):
    n_sub = d // LANES
    tm = EXPERT_ROWS
    n_exp, dff = wg.shape[1], wg.shape[3]
    n_rows = t * EXPERT_TOPK + n_exp * tm

    def w_map(e, *_):
        return (layer, e, 0, 0)

    out_rows = (t + COMBINE_BATCH) * n_sub
    grid_spec = pltpu.PrefetchScalarGridSpec(
        num_scalar_prefetch=6,
        grid=(n_exp,),
        in_specs=[pl.BlockSpec((t * n_sub, LANES), lambda e, *_: (0, 0)),
                  pl.BlockSpec((None, None, d, dff), w_map),
                  pl.BlockSpec((None, None, d, dff), w_map),
                  pl.BlockSpec((None, None, dff, d), w_map),
                  pl.BlockSpec((None, None, 1, dff), w_map),
                  pl.BlockSpec((None, None, 1, dff), w_map),
                  pl.BlockSpec((None, None, 1, d), w_map)],
        out_specs=pl.BlockSpec((out_rows, LANES), lambda e, *_: (0, 0)),
        scratch_shapes=[pltpu.SMEM((n_rows,), I32),
                        pltpu.VMEM((tm * n_sub, LANES), F32),
                        pltpu.VMEM((tm * n_sub, LANES), F32)],
    )
    return pl.pallas_call(
        functools.partial(_moe_kernel, n_tok=t, n_exp=n_exp, tok_bits=tok_bits),
        grid_spec=grid_spec,
        out_shape=jax.ShapeDtypeStruct((out_rows, LANES), F32),
        compiler_params=_params(1),
        name="moe_experts",
    )(dest, gates, start, count, nch, last, x_tiles, wg, wu, wd, bg, bu, bd)


def _add_ln_kernel(x_ref, f_ref, g_ref, b_ref, o_ref, *, alpha):
    tm, d = o_ref.shape
    n_sub = d // LANES
    z = jnp.concatenate(
        [alpha * x_ref[pl.ds(s, tm, stride=n_sub), :] + f_ref[pl.ds(s, tm, stride=n_sub), :]
         for s in range(n_sub)], axis=1)
    o_ref[...] = _layer_norm(z, g_ref[...], b_ref[...])


def _add_ln(x_tiles, ffn_tiles, t, d, ln_g, ln_b, alpha):
    n_sub = d // LANES
    tm = min(TOKEN_TILE, t)
    return pl.pallas_call(
        functools.partial(_add_ln_kernel, alpha=alpha),
        grid=(t // tm,),
        in_specs=[pl.BlockSpec((tm * n_sub, LANES), lambda i: (i, 0)),
                  pl.BlockSpec((tm * n_sub, LANES), lambda i: (i, 0)),
                  pl.BlockSpec((1, d), lambda i: (0, 0)),
                  pl.BlockSpec((1, d), lambda i: (0, 0))],
        out_specs=pl.BlockSpec((tm, d), lambda i: (i, 0)),
        out_shape=jax.ShapeDtypeStruct((t, d), F32),
        compiler_params=_params(1),
        name="add_ln",
    )(x_tiles, ffn_tiles, ln_g, ln_b)


def _lru_in_kernel(x_ref, w_ref, y_ref, r_ref):
    d = y_ref.shape[1]
    yz = jnp.dot(x_ref[...].astype(BF16), w_ref[...], preferred_element_type=F32)
    y = yz[:, :d]
    y_ref[...] = 0.5 * y * (1.0 + jnp.tanh(math.sqrt(2.0 / math.pi) * (y + 0.044715 * (y * y * y))))
    r_ref[...] = yz[:, d:]


def _lru_in(x, w_in):
    t, d = x.shape
    n = w_in.shape[1] // 2
    tm = min(TOKEN_TILE, t)
    return pl.pallas_call(
        _lru_in_kernel,
        grid=(t // tm,),
        in_specs=[pl.BlockSpec((tm, d), lambda i: (i, 0)),
                  pl.BlockSpec((d, 2 * n), lambda i: (0, 0))],
        out_specs=[pl.BlockSpec((tm, n), lambda i: (i, 0)),
                   pl.BlockSpec((tm, n), lambda i: (i, 0))],
        out_shape=[jax.ShapeDtypeStruct((t, n), F32), jax.ShapeDtypeStruct((t, n), F32)],
        compiler_params=_params(1),
        name="lru_in",
    )(x, w_in)


def _lru_core_kernel(r_ref, y_ref, cw_ref, cb_ref, wgx_ref, wga_ref, bgx_ref, bga_ref, ap_ref,
                     hy_ref, rprev_ref, carry_ref, a_ref, u_ref):
    t = pl.program_id(0)
    tt, d = r_ref.shape
    n_blk, bw, _ = wgx_ref.shape
    cwid = cw_ref.shape[0]

    @pl.when(t == 0)
    def _():
        rprev_ref[...] = jnp.zeros_like(rprev_ref)
        carry_ref[...] = jnp.zeros_like(carry_ref)

    r = r_ref[...]
    rext = jnp.concatenate([rprev_ref[...], r], axis=0)
    rc = jnp.broadcast_to(cb_ref[...], (tt, d))
    for w in range(cwid):
        off = 8 - (cwid - 1) + w
        rc = rc + cw_ref[w:w + 1, :] * rext[off:off + tt, :]
    rprev_ref[...] = r[tt - 8:, :]

    rcb = rc.astype(BF16)
    gx = jnp.concatenate(
        [jnp.dot(rcb[:, n * bw:(n + 1) * bw], wgx_ref[n], preferred_element_type=F32)
         for n in range(n_blk)], axis=1)
    ga = jnp.concatenate(
        [jnp.dot(rcb[:, n * bw:(n + 1) * bw], wga_ref[n], preferred_element_type=F32)
         for n in range(n_blk)], axis=1)
    gate_x = _sigmoid(gx + bgx_ref[...])
    gate_a = _sigmoid(ga + bga_ref[...])
    z = -ap_ref[...]
    softplus = jnp.maximum(z, 0.0) + jnp.log(1.0 + jnp.exp(-jnp.abs(z)))
    log_a = -LRU_C * gate_a * softplus
    a_ref[...] = jnp.exp(log_a)
    u_ref[...] = jnp.sqrt(1.0 - jnp.exp(2.0 * log_a)) * (gate_x * rc)

    row = lax.broadcasted_iota(I32, (SCAN_ROWS, d), 0)

    def slab(s, carry):
        off = pl.multiple_of(s * SCAN_ROWS, SCAN_ROWS)
        a = a_ref[pl.ds(off, SCAN_ROWS), :]
        b = u_ref[pl.ds(off, SCAN_ROWS), :]
        step = 1
        while step < SCAN_ROWS:
            a_sh = pltpu.roll(a, step, 0)
            b_sh = pltpu.roll(b, step, 0)
            live = row >= step
            b = jnp.where(live, a * b_sh + b, b)
            a = jnp.where(live, a * a_sh, a)
            step *= 2
        h = b + a * carry
        u_ref[pl.ds(off, SCAN_ROWS), :] = h
        return h[SCAN_ROWS - 1:SCAN_ROWS, :]

    carry_ref[...] = lax.fori_loop(0, tt // SCAN_ROWS, slab, carry_ref[...])
    hy_ref[...] = (u_ref[...] * y_ref[...]).astype(hy_ref.dtype)


def _lru_core(r, y, conv_w, conv_b, wgx, wga, bgx, bga, a_param):
    t, d = r.shape
    tt = min(TOKEN_TILE, t)
    row = lambda i: (i, 0)
    fixed2 = lambda i: (0, 0)
    fixed3 = lambda i: (0, 0, 0)
    return pl.pallas_call(
        _lru_core_kernel,
        grid=(t // tt,),
        in_specs=[pl.BlockSpec((tt, d), row),
                  pl.BlockSpec((tt, d), row),
                  pl.BlockSpec(conv_w.shape, fixed2),
                  pl.BlockSpec((1, d), fixed2),
                  pl.BlockSpec(wgx.shape, fixed3),
                  pl.BlockSpec(wga.shape, fixed3),
                  pl.BlockSpec((1, d), fixed2),
                  pl.BlockSpec((1, d), fixed2),
                  pl.BlockSpec((1, d), fixed2)],
        out_specs=pl.BlockSpec((tt, d), row),
        out_shape=jax.ShapeDtypeStruct((t, d), BF16),
        scratch_shapes=[pltpu.VMEM((8, d), F32),
                        pltpu.VMEM((1, d), F32),
                        pltpu.VMEM((tt, d), F32),
                        pltpu.VMEM((tt, d), F32)],
        compiler_params=_params(1),
        name="lru_core",
    )(r, y, conv_w, conv_b, wgx, wga, bgx, bga, a_param)


def _moe(xn_tiles, t, d, ids, gates, rank, counts128, layer, wg, wu, wd, bg, bu, bd):
    tm = EXPERT_ROWS
    n_exp = wg.shape[1]
    counts = counts128[:, 0].astype(I32)
    nch = (counts + tm - 1) // tm
    start = (jnp.cumsum(nch) - nch).astype(I32) * tm
    experts = jnp.arange(n_exp, dtype=I32)
    dest = rank + jnp.sum(jnp.where(ids[:, :, None] == experts, start, 0), axis=-1)
    last = ((jnp.sum(nch) - 1) * tm).astype(I32).reshape(1)
    tok_bits = (t + COMBINE_BATCH - 1).bit_length()
    gates_ext = jnp.pad(gates, ((0, 1), (0, (1 << tok_bits) - t))).reshape(-1)
    return _moe_experts(xn_tiles, t, d, tok_bits, dest.reshape(-1).astype(I32), gates_ext,
                        start, counts, nch.astype(I32), last, layer, wg, wu, wd, bg, bu, bd)


def kernel(x, rel_bias_table, w_qkv, w_attn_out, w_lru_in, lru_conv_w, lru_conv_b, w_lru_gate_x, b_lru_gate_x, w_lru_gate_a, b_lru_gate_a, lru_a_param, w_lru_out, w_router, b_router, w_exp_gate, b_exp_gate, w_exp_up, b_exp_up, w_exp_down, b_exp_down, ln_mix_g, ln_mix_b, ln_ffn_g, ln_ffn_b):
    bsz, t, d = x.shape
    depth = w_router.shape[0]
    n_mixers = 2
    alpha = (2 * depth) ** 0.25
    assert t % MOBA_BLOCK == 0 and t % TOKEN_TILE == 0

    biases = _attn_biases(rel_bias_table, t)
    w_qkv_b = w_qkv.astype(BF16)
    w_attn_out_b = w_attn_out.astype(BF16)
    w_lru_in_b = w_lru_in.astype(BF16)
    w_lru_out_b = w_lru_out.astype(BF16)
    wgx_b = w_lru_gate_x.astype(BF16)
    wga_b = w_lru_gate_a.astype(BF16)
    wg_b = w_exp_gate.astype(BF16)
    wu_b = w_exp_up.astype(BF16)
    wd_b = w_exp_down.astype(BF16)
    wr_t = jnp.swapaxes(w_router, 1, 2)
    wr_hi = wr_t.astype(BF16)
    wr_lo = (wr_t - wr_hi.astype(F32)).astype(BF16)

    bg4 = b_exp_gate[:, :, None, :]
    bu4 = b_exp_up[:, :, None, :]
    bd4 = b_exp_down[:, :, None, :]

    outs = []
    for b in range(bsz):
        xs = x[b]
        for i in range(depth):
            j = i // n_mixers
            if i % n_mixers == 0:
                qkv = _matmul(xs, w_qkv_b[j], BF16)
                a = _moba_attention(qkv, biases)
                w_out = w_attn_out_b[j]
            else:
                y, r = _lru_in(xs, w_lru_in_b[j])
                a = _lru_core(r, y, lru_conv_w[j], lru_conv_b[j][None], wgx_b[j], wga_b[j],
                              b_lru_gate_x[j][None], b_lru_gate_a[j][None], lru_a_param[j][None])
                w_out = w_lru_out_b[j]
            xn, ids, gates, rank, counts = _proj_ln_router(
                a, w_out, xs, ln_mix_g[i][None], ln_mix_b[i][None],
                wr_hi[i], wr_lo[i], b_router[i][:, None], alpha)
            ffn = _moe(xn, t, d, ids, gates, rank, counts, i, wg_b, wu_b, wd_b, bg4, bu4, bd4)
            xs = _add_ln(xn, ffn, t, d, ln_ffn_g[i][None], ln_ffn_b[i][None], alpha)
        outs.append(xs)
    return jnp.stack(outs, axis=0)
```

```python
import functools
import math

import numpy as np
import jax
import jax.numpy as jnp
from jax import lax
from jax.experimental import pallas as pl
from jax.experimental.pallas import tpu as pltpu

F32 = jnp.float32
BF16 = jnp.bfloat16
I32 = jnp.int32

N_HEADS = 8
MOBA_BLOCK = 256
MOBA_TOPK = 3
NUM_BUCKETS = 32
REL_MAX_DISTANCE = 128
LRU_C = 8.0
EXPERT_TOPK = 4
SWIGLU_LIMIT = 7.0
SWIGLU_ALPHA = 1.702
LN_EPS = 1e-5
LOG2_E = 1.4426950408889634

EXPERT_ROWS = 256
TOKEN_TILE = 512
SCAN_ROWS = 8
LANES = 128
HEAD_GROUP = 8
COMBINE_BATCH = 8
V7X_VMEM_LIMIT = 56 * 1024 * 1024

_NT = (((1,), (1,)), ((), ()))


def _params(n_axes, vmem=V7X_VMEM_LIMIT):
    return pltpu.CompilerParams(
        dimension_semantics=("arbitrary",) * n_axes, vmem_limit_bytes=vmem)


def _sigmoid(x):
    return 0.5 * jnp.tanh(0.5 * x) + 0.5


def _layer_norm(z, g, b):
    mu = jnp.mean(z, axis=-1, keepdims=True)
    zc = z - mu
    var = jnp.mean(zc * zc, axis=-1, keepdims=True)
    return zc * lax.rsqrt(var + LN_EPS) * g + b


def _matmul_kernel(x_ref, w_ref, o_ref):
    o_ref[...] = jnp.dot(x_ref[...].astype(BF16), w_ref[...],
                         preferred_element_type=F32).astype(o_ref.dtype)


def _matmul(x, w, out_dtype):
    m, k = x.shape
    n = w.shape[1]
    tm = min(TOKEN_TILE, m)
    return pl.pallas_call(
        _matmul_kernel,
        grid=(m // tm,),
        in_specs=[pl.BlockSpec((tm, k), lambda i: (i, 0)),
                  pl.BlockSpec((k, n), lambda i: (0, 0))],
        out_specs=pl.BlockSpec((tm, n), lambda i: (i, 0)),
        out_shape=jax.ShapeDtypeStruct((m, n), out_dtype),
        compiler_params=_params(1),
        name="dense_proj",
    )(x, w)


def _attn_kernel(q_ref, k_ref, v_ref, bself_ref, badj_ref, bfar_ref, o_ref,
                 kmean_ref, vt_ref, pen_ref, m_ref, l_ref, acc_ref, sa_ref, sb_ref, *, nblk, scale):
    blk = MOBA_BLOCK
    hg = bself_ref.shape[0]
    dh = q_ref.shape[1] // hg
    i = pl.program_id(1)

    def head(ref, h):
        return ref.at[:, h * dh:(h + 1) * dh]

    @pl.when(i == 0)
    def _():
        for h in range(hg):
            for j in range(nblk):
                kj = head(k_ref, h)[j * blk:(j + 1) * blk, :].astype(F32)
                kmean_ref[h, j:j + 1, :] = jnp.sum(kj, axis=0, keepdims=True) * (1.0 / blk)
                vt_ref[h, j] = head(v_ref, h)[j * blk:(j + 1) * blk, :].astype(F32).T.astype(BF16)

    qs = []
    for h in range(hg):
        q = head(q_ref, h)[...]

        km = kmean_ref[h]
        km_hi = km.astype(BF16)
        km_lo = (km - km_hi.astype(F32)).astype(BF16)
        gate = (lax.dot_general(km_hi, q, _NT, preferred_element_type=F32)
                + lax.dot_general(km_lo, q, _NT, preferred_element_type=F32))
        jj = lax.broadcasted_iota(I32, gate.shape, 0)
        past = jj < i
        g = jnp.where(past, gate, -jnp.inf)
        sel = jnp.zeros(gate.shape, jnp.bool_)
        for _ in range(min(MOBA_TOPK, nblk)):
            mx = jnp.max(g, axis=0, keepdims=True)
            jm = jnp.min(jnp.where(g == mx, jj, nblk), axis=0, keepdims=True)
            pick = jj == jm
            sel = jnp.logical_or(sel, pick)
            g = jnp.where(pick, -jnp.inf, g)
        pen_ref[h] = jnp.where(jnp.logical_and(sel, past), 0.0, -jnp.inf)
        qs.append((q.astype(F32) * (scale * LOG2_E)).astype(BF16))

    def scores(h, j):
        kb = head(k_ref, h)[pl.ds(pl.multiple_of(j * blk, blk), blk), :]
        return lax.dot_general(kb, qs[h], _NT, preferred_element_type=F32)

    def fold(x, op):
        return op(x.reshape(blk // 8, 8, blk), axis=0)

    def adj_bias(h):
        return badj_ref[h] + pen_ref[h, pl.ds(jnp.maximum(i - 1, 0), 1), :]

    def far_bias(h, j):
        return bfar_ref[h, 0:1, 0:1] + pen_ref[h, pl.ds(j, 1), :]

    n_far = jnp.maximum(i - 1, 0)


    ss = [scores(h, i) + bself_ref[h] for h in range(hg)]
    for h in range(hg):
        m0 = jnp.max(fold(ss[h], jnp.max), axis=0, keepdims=True)
        p = jnp.exp2(ss[h] - m0)
        m_ref[h] = m0
        l_ref[h] = fold(p, jnp.sum)
        acc_ref[h] = jnp.dot(vt_ref[h, i], p.astype(BF16), preferred_element_type=F32)

    def update(h, j, s, s_max, bias):
        if bias is not None:
            s_max = s_max + bias
        m_old = m_ref[h]
        m_new = jnp.maximum(m_old, jnp.max(s_max, axis=0, keepdims=True))
        alpha = jnp.exp2(m_old - m_new)
        p = jnp.exp2(s - m_new if bias is None else s + (bias - m_new))
        m_ref[h] = m_new
        l_ref[h] = alpha * l_ref[h] + fold(p, jnp.sum)
        acc_ref[h] = alpha * acc_ref[h] + jnp.dot(
            vt_ref[h, j], p.astype(BF16), preferred_element_type=F32)

    @pl.when(i >= 1)
    def _():
        ss = [scores(h, i - 1) + adj_bias(h) for h in range(hg)]
        for h in range(hg):
            update(h, i - 1, ss[h], fold(ss[h], jnp.max), None)

    @pl.when(n_far >= 1)
    def _():
        for h in range(hg):
            sa_ref[h] = scores(h, 0)

    def stage(j, cur_ref, nxt_ref):
        nxt = jnp.minimum(j + 1, n_far - 1)
        for h in range(hg):
            nxt_ref[h] = scores(h, nxt)
        for h in range(hg):
            s = cur_ref[h]
            update(h, j, s, fold(s, jnp.max), far_bias(h, j))

    def far_pair(p, carry):
        stage(2 * p, sa_ref, sb_ref)

        @pl.when(2 * p + 1 < n_far)
        def _():
            stage(2 * p + 1, sb_ref, sa_ref)

        return carry

    lax.fori_loop(0, (n_far + 1) // 2, far_pair, 0)

    for h in range(hg):
        o = acc_ref[h] / jnp.sum(l_ref[h], axis=0, keepdims=True)
        head(o_ref, h)[...] = o.T.astype(o_ref.dtype)


def _t5_bucket_np(dist):
    max_exact = NUM_BUCKETS // 2
    n = dist.astype(np.float32)
    large = max_exact + (np.log(np.maximum(n, np.float32(1.0)) / np.float32(max_exact))
                         / np.float32(math.log(REL_MAX_DISTANCE / max_exact))
                         * np.float32(NUM_BUCKETS - max_exact)).astype(np.int32)
    large = np.minimum(large, NUM_BUCKETS - 1)
    return np.where(dist < max_exact, dist, large)


def _t5_bucket(dist):
    max_exact = NUM_BUCKETS // 2
    n = dist.astype(F32)
    large = max_exact + (jnp.log(jnp.maximum(n, 1.0) / max_exact)
                         / math.log(REL_MAX_DISTANCE / max_exact)
                         * (NUM_BUCKETS - max_exact)).astype(I32)
    large = jnp.minimum(large, NUM_BUCKETS - 1)
    return jnp.where(dist < max_exact, dist, large)


def _attn_biases(rel_bias_table, t):
    blk = MOBA_BLOCK
    far_buckets = _t5_bucket_np(np.arange(blk + 1, max(t, 2 * blk) + 1, dtype=np.int32))
    assert (far_buckets == NUM_BUCKETS - 1).all()
    bbd = rel_bias_table.astype(F32)[_t5_bucket(jnp.arange(2 * blk + 1, dtype=I32))].T * LOG2_E
    n_heads = bbd.shape[0]

    def toeplitz(w):
        rep = jnp.tile(w, (1, blk))[:, :blk * (2 * blk - 1)].reshape(n_heads, blk, 2 * blk - 1)
        return rep[:, :, blk - 1:]

    w_self = jnp.concatenate([jnp.full((n_heads, blk - 1), -jnp.inf, F32), bbd[:, :blk + 1]], axis=1)
    w_adj = jnp.concatenate([bbd[:, 1:2 * blk], bbd[:, :1]], axis=1)
    bself = toeplitz(w_self)
    badj = toeplitz(w_adj)
    bfar = jnp.broadcast_to(bbd[:, 2 * blk][:, None, None], (n_heads, 8, 128))
    return bself, badj, bfar


def _moba_attention(qkv, biases):
    t, d3 = qkv.shape
    d = d3 // 3
    dh = d // N_HEADS
    blk = MOBA_BLOCK
    nblk = t // blk
    bself, badj, bfar = biases
    hg = HEAD_GROUP
    n_grp = N_HEADS // hg
    kernel = functools.partial(_attn_kernel, nblk=nblk, scale=dh ** -0.5)
    return pl.pallas_call(
        kernel,
        grid=(n_grp, nblk),
        in_specs=[pl.BlockSpec((blk, hg * dh), lambda g, i: (i, g)),
                  pl.BlockSpec((t, hg * dh), lambda g, i: (0, n_grp + g)),
                  pl.BlockSpec((t, hg * dh), lambda g, i: (0, 2 * n_grp + g)),
                  pl.BlockSpec((hg, blk, blk), lambda g, i: (g, 0, 0)),
                  pl.BlockSpec((hg, blk, blk), lambda g, i: (g, 0, 0)),
                  pl.BlockSpec((hg, 8, LANES), lambda g, i: (g, 0, 0))],
        out_specs=pl.BlockSpec((blk, hg * dh), lambda g, i: (i, g)),
        out_shape=jax.ShapeDtypeStruct((t, d), BF16),
        scratch_shapes=[pltpu.VMEM((hg, nblk, dh), F32),
                        pltpu.VMEM((hg, nblk, dh, blk), BF16),
                        pltpu.VMEM((hg, nblk, blk), F32),
                        pltpu.VMEM((hg, 1, blk), F32),
                        pltpu.VMEM((hg, 8, blk), F32),
                        pltpu.VMEM((hg, dh, blk), F32),
                        pltpu.VMEM((hg, blk, blk), F32),
                        pltpu.VMEM((hg, blk, blk), F32)],
        compiler_params=_params(2),
        name="moba_attention",
    )(qkv, qkv, qkv, bself, badj, bfar)


def _proj_ln_router_kernel(a_ref, w_ref, x_ref, g_ref, b_ref, wrh_ref, wrl_ref, br_ref,
                           xn_ref, ids_ref, gates_ref, rank_ref, counts_ref, base_ref,
                           *, alpha, n_exp):
    t = pl.program_id(0)
    tm = a_ref.shape[0]

    @pl.when(t == 0)
    def _():
        base_ref[...] = jnp.zeros_like(base_ref)

    mix = jnp.dot(a_ref[...].astype(BF16), w_ref[...], preferred_element_type=F32)
    xn = _layer_norm(alpha * x_ref[...] + mix, g_ref[...], b_ref[...])
    n_sub = xn.shape[1] // LANES
    for s in range(n_sub):
        xn_ref[pl.ds(s, tm, stride=n_sub), :] = xn[:, s * LANES:(s + 1) * LANES]

    x_hi = xn.astype(BF16)
    x_lo = (xn - x_hi.astype(F32)).astype(BF16)
    wr_hi = wrh_ref[...]
    logits = (lax.dot_general(wr_hi, x_hi, _NT, preferred_element_type=F32)
              + lax.dot_general(wr_hi, x_lo, _NT, preferred_element_type=F32)
              + lax.dot_general(wrl_ref[...], x_hi, _NT, preferred_element_type=F32)
              + br_ref[...])

    e_iota = lax.broadcasted_iota(I32, logits.shape, 0)
    work = logits
    tops, picks = [], []
    for k in range(EXPERT_TOPK):
        mx = jnp.max(work, axis=0, keepdims=True)
        em = jnp.min(jnp.where(work == mx, e_iota, n_exp), axis=0, keepdims=True)
        pick = e_iota == em
        ids_ref[k:k + 1, :] = em
        tops.append(mx)
        picks.append(pick)
        work = jnp.where(pick, -jnp.inf, work)

    ex = [jnp.exp(tk - tops[0]) for tk in tops]
    den = ex[0]
    for e in ex[1:]:
        den = den + e
    for k in range(EXPERT_TOPK):
        gates_ref[k:k + 1, :] = ex[k] / den

    onehot = jnp.zeros(logits.shape, F32)
    for pick in picks:
        onehot = onehot + pick.astype(F32)
    earlier = (lax.broadcasted_iota(I32, (tm, tm), 0)
               < lax.broadcasted_iota(I32, (tm, tm), 1)).astype(BF16)
    pos = jnp.dot(onehot.astype(BF16), earlier, preferred_element_type=F32) + base_ref[:, 0:1]
    for k in range(EXPERT_TOPK):
        rank_ref[k:k + 1, :] = jnp.sum(jnp.where(picks[k], pos, 0.0), axis=0,
                                       keepdims=True).astype(I32)
    total = base_ref[...] + jnp.sum(onehot, axis=1, keepdims=True)
    base_ref[...] = total
    counts_ref[...] = total


def _proj_ln_router(a, w, x, ln_g, ln_b, wr_hi, wr_lo, br, alpha):
    t, d = x.shape
    n_exp = wr_hi.shape[0]
    tm = min(TOKEN_TILE, t)
    kernel = functools.partial(_proj_ln_router_kernel, alpha=alpha, n_exp=n_exp)
    row = lambda i: (i, 0)
    fixed = lambda i: (0, 0)
    col = lambda i: (0, i)
    return pl.pallas_call(
        kernel,
        grid=(t // tm,),
        in_specs=[pl.BlockSpec((tm, d), row),
                  pl.BlockSpec((d, d), fixed),
                  pl.BlockSpec((tm, d), row),
                  pl.BlockSpec((1, d), fixed),
                  pl.BlockSpec((1, d), fixed),
                  pl.BlockSpec((n_exp, d), fixed),
                  pl.BlockSpec((n_exp, d), fixed),
                  pl.BlockSpec((n_exp, 1), fixed)],
        out_specs=[pl.BlockSpec((tm * (d // LANES), LANES), row),
                   pl.BlockSpec((EXPERT_TOPK, tm), col),
                   pl.BlockSpec((EXPERT_TOPK, tm), col),
                   pl.BlockSpec((EXPERT_TOPK, tm), col),
                   pl.BlockSpec((n_exp, 128), fixed)],
        out_shape=[jax.ShapeDtypeStruct((t * (d // LANES), LANES), F32),
                   jax.ShapeDtypeStruct((EXPERT_TOPK, t), I32),
                   jax.ShapeDtypeStruct((EXPERT_TOPK, t), F32),
                   jax.ShapeDtypeStruct((EXPERT_TOPK, t), I32),
                   jax.ShapeDtypeStruct((n_exp, 128), F32)],
        scratch_shapes=[pltpu.VMEM((n_exp, 128), F32)],
        compiler_params=_params(1),
        name="proj_ln_router",
    )(a, w, x, ln_g, ln_b, wr_hi, wr_lo, br)


def _moe_kernel(dest_ref, gate_ref, start_ref, count_ref, nch_ref, last_ref,
                x2_ref, wg_ref, wu_ref, wd_ref, bg_ref, bu_ref, bd_ref,
                out_ref, src_ref, xg_ref, yf_ref, *, n_tok, n_exp, tok_bits):
    e = pl.program_id(0)
    n_sub = wg_ref.shape[0] // LANES
    tm = xg_ref.shape[0] // n_sub
    spare = COMBINE_BATCH

    def tile(ref, idx):
        return ref.at[pl.ds(pl.multiple_of(idx * n_sub, n_sub), n_sub), :]

    tok_mask = (1 << tok_bits) - 1

    def gather_row(base, r):
        tok = jnp.minimum(src_ref[base + r] & tok_mask, n_tok - 1)
        tile(xg_ref, r)[...] = tile(x2_ref, tok)[...]

    def combine_batch(base, r0):
        codes = [src_ref[base + r0 + k] for k in range(spare)]
        toks = [c & tok_mask for c in codes]
        gs = [gate_ref[c] for c in codes]
        olds = [tile(out_ref, toks[k])[...] for k in range(spare)]
        for k in range(spare):
            tile(out_ref, toks[k])[...] = olds[k] + gs[k] * tile(yf_ref, r0 + k)[...]

    @pl.when(e == 0)
    def _():
        out_ref[...] = jnp.zeros_like(out_ref)
        yf_ref[...] = jnp.zeros_like(yf_ref)

        def fill(r, carry):
            src_ref[r] = n_tok + r % spare + (EXPERT_TOPK << tok_bits)
            return carry

        for ee in range(n_exp):
            lax.fori_loop(start_ref[ee] + count_ref[ee], start_ref[ee] + nch_ref[ee] * tm, fill, 0)

        def scatter(a, carry):
            for k in range(EXPERT_TOPK):
                src_ref[dest_ref[k * n_tok + a]] = a + (k << tok_bits)
            return carry

        lax.fori_loop(0, n_tok, scatter, 0, unroll=8)

        def first_rows(r, carry):
            gather_row(0, r)
            return carry

        lax.fori_loop(0, tm, first_rows, 0, unroll=8)

    def visit(base, rows):
        x = jnp.concatenate(
            [xg_ref[pl.ds(s, rows, stride=n_sub), :].astype(BF16) for s in range(n_sub)], axis=1)

        nxt = jnp.minimum(base + tm, last_ref[0])
        for r in range(tm):
            gather_row(nxt, r)
        prev = jnp.maximum(base - tm, 0)
        for r0 in range(0, tm, spare):
            combine_batch(prev, r0)

        g = jnp.dot(x, wg_ref[...], preferred_element_type=F32) + bg_ref[...]
        u = jnp.dot(x, wu_ref[...], preferred_element_type=F32) + bu_ref[...]
        g = jnp.minimum(g, SWIGLU_LIMIT)
        u = jnp.clip(u, -SWIGLU_LIMIT, SWIGLU_LIMIT)
        h = g * _sigmoid(SWIGLU_ALPHA * g) * (u + 1.0)
        y = jnp.dot(h.astype(BF16), wd_ref[...], preferred_element_type=F32) + bd_ref[...]
        for s in range(n_sub):
            yf_ref[pl.ds(s, rows, stride=n_sub), :] = y[:, s * LANES:(s + 1) * LANES]

    def chunk(ch, carry):
        base = start_ref[e] + ch * tm
        n_real = count_ref[e] - ch * tm

        @pl.when(n_real > tm // 2)
        def _():
            visit(base, tm)

        @pl.when(n_real <= tm // 2)
        def _():
            visit(base, tm // 2)

        return carry

    lax.fori_loop(0, nch_ref[e], chunk, 0)

    @pl.when(e == n_exp - 1)
    def _():
        def last_rows(b, carry):
            combine_batch(last_ref[0], b * spare)
            return carry

        lax.fori_loop(0, tm // spare, last_rows, 0)


def _moe_experts(x_tiles, t, d, tok_bits, dest, gates, start, count, nch, last, layer,
                 wg, wu, wd, bg, bu, bd):
    n_sub = d // LANES
    tm = EXPERT_ROWS
    n_exp, dff = wg.shape[1], wg.shape[3]
    n_rows = t * EXPERT_TOPK + n_exp * tm

    def w_map(e, *_):
        return (layer, e, 0, 0)

    out_rows = (t + COMBINE_BATCH) * n_sub
    grid_spec = pltpu.PrefetchScalarGridSpec(
        num_scalar_prefetch=6,
        grid=(n_exp,),
        in_specs=[pl.BlockSpec((t * n_sub, LANES), lambda e, *_: (0, 0)),
                  pl.BlockSpec((None, None, d, dff), w_map),
                  pl.BlockSpec((None, None, d, dff), w_map),
                  pl.BlockSpec((None, None, dff, d), w_map),
                  pl.BlockSpec((None, None, 1, dff), w_map),
                  pl.BlockSpec((None, None, 1, dff), w_map),
                  pl.BlockSpec((None, None, 1, d), w_map)],
        out_specs=pl.BlockSpec((out_rows, LANES), lambda e, *_: (0, 0)),
        scratch_shapes=[pltpu.SMEM((n_rows,), I32),
                        pltpu.VMEM((tm * n_sub, LANES), F32),
                        pltpu.VMEM((tm * n_sub, LANES), F32)],
    )
    return pl.pallas_call(
        functools.partial(_moe_kernel, n_tok=t, n_exp=n_exp, tok_bits=tok_bits),
        grid_spec=grid_spec,
        out_shape=jax.ShapeDtypeStruct((out_rows, LANES), F32),
        compiler_params=_params(1),
        name="moe_experts",
    )(dest, gates, start, count, nch, last, x_tiles, wg, wu, wd, bg, bu, bd)


def _add_ln_kernel(x_ref, f_ref, g_ref, b_ref, o_ref, *, alpha):
    tm, d = o_ref.shape
    n_sub = d // LANES
    z = jnp.concatenate(
        [alpha * x_ref[pl.ds(s, tm, stride=n_sub), :] + f_ref[pl.ds(s, tm, stride=n_sub), :]
         for s in range(n_sub)], axis=1)
    o_ref[...] = _layer_norm(z, g_ref[...], b_ref[...])


def _add_ln(x_tiles, ffn_tiles, t, d, ln_g, ln_b, alpha):
    n_sub = d // LANES
    tm = min(TOKEN_TILE, t)
    return pl.pallas_call(
        functools.partial(_add_ln_kernel, alpha=alpha),
        grid=(t // tm,),
        in_specs=[pl.BlockSpec((tm * n_sub, LANES), lambda i: (i, 0)),
                  pl.BlockSpec((tm * n_sub, LANES), lambda i: (i, 0)),
                  pl.BlockSpec((1, d), lambda i: (0, 0)),
                  pl.BlockSpec((1, d), lambda i: (0, 0))],
        out_specs=pl.BlockSpec((tm, d), lambda i: (i, 0)),
        out_shape=jax.ShapeDtypeStruct((t, d), F32),
        compiler_params=_params(1),
        name="add_ln",
    )(x_tiles, ffn_tiles, ln_g, ln_b)


def _lru_in_kernel(x_ref, w_ref, y_ref, r_ref):
    d = y_ref.shape[1]
    yz = jnp.dot(x_ref[...].astype(BF16), w_ref[...], preferred_element_type=F32)
    y = yz[:, :d]
    y_ref[...] = 0.5 * y * (1.0 + jnp.tanh(math.sqrt(2.0 / math.pi) * (y + 0.044715 * (y * y * y))))
    r_ref[...] = yz[:, d:]


def _lru_in(x, w_in):
    t, d = x.shape
    n = w_in.shape[1] // 2
    tm = min(TOKEN_TILE, t)
    return pl.pallas_call(
        _lru_in_kernel,
        grid=(t // tm,),
        in_specs=[pl.BlockSpec((tm, d), lambda i: (i, 0)),
                  pl.BlockSpec((d, 2 * n), lambda i: (0, 0))],
        out_specs=[pl.BlockSpec((tm, n), lambda i: (i, 0)),
                   pl.BlockSpec((tm, n), lambda i: (i, 0))],
        out_shape=[jax.ShapeDtypeStruct((t, n), F32), jax.ShapeDtypeStruct((t, n), F32)],
        compiler_params=_params(1),
        name="lru_in",
    )(x, w_in)


def _lru_core_kernel(r_ref, y_ref, cw_ref, cb_ref, wgx_ref, wga_ref, bgx_ref, bga_ref, ap_ref,
                     hy_ref, rprev_ref, carry_ref, a_ref, u_ref):
    t = pl.program_id(0)
    tt, d = r_ref.shape
    n_blk, bw, _ = wgx_ref.shape
    cwid = cw_ref.shape[0]

    @pl.when(t == 0)
    def _():
        rprev_ref[...] = jnp.zeros_like(rprev_ref)
        carry_ref[...] = jnp.zeros_like(carry_ref)

    r = r_ref[...]
    rext = jnp.concatenate([rprev_ref[...], r], axis=0)
    rc = jnp.broadcast_to(cb_ref[...], (tt, d))
    for w in range(cwid):
        off = 8 - (cwid - 1) + w
        rc = rc + cw_ref[w:w + 1, :] * rext[off:off + tt, :]
    rprev_ref[...] = r[tt - 8:, :]

    rcb = rc.astype(BF16)
    gx = jnp.concatenate(
        [jnp.dot(rcb[:, n * bw:(n + 1) * bw], wgx_ref[n], preferred_element_type=F32)
         for n in range(n_blk)], axis=1)
    ga = jnp.concatenate(
        [jnp.dot(rcb[:, n * bw:(n + 1) * bw], wga_ref[n], preferred_element_type=F32)
         for n in range(n_blk)], axis=1)
    gate_x = _sigmoid(gx + bgx_ref[...])
    gate_a = _sigmoid(ga + bga_ref[...])
    z = -ap_ref[...]
    softplus = jnp.maximum(z, 0.0) + jnp.log(1.0 + jnp.exp(-jnp.abs(z)))
    log_a = -LRU_C * gate_a * softplus
    a_ref[...] = jnp.exp(log_a)
    u_ref[...] = jnp.sqrt(1.0 - jnp.exp(2.0 * log_a)) * (gate_x * rc)

    row = lax.broadcasted_iota(I32, (SCAN_ROWS, d), 0)

    def slab(s, carry):
        off = pl.multiple_of(s * SCAN_ROWS, SCAN_ROWS)
        a = a_ref[pl.ds(off, SCAN_ROWS), :]
        b = u_ref[pl.ds(off, SCAN_ROWS), :]
        step = 1
        while step < SCAN_ROWS:
            a_sh = pltpu.roll(a, step, 0)
            b_sh = pltpu.roll(b, step, 0)
            live = row >= step
            b = jnp.where(live, a * b_sh + b, b)
            a = jnp.where(live, a * a_sh, a)
            step *= 2
        h = b + a * carry
        u_ref[pl.ds(off, SCAN_ROWS), :] = h
        return h[SCAN_ROWS - 1:SCAN_ROWS, :]

    carry_ref[...] = lax.fori_loop(0, tt // SCAN_ROWS, slab, carry_ref[...])
    hy_ref[...] = (u_ref[...] * y_ref[...]).astype(hy_ref.dtype)


def _lru_core(r, y, conv_w, conv_b, wgx, wga, bgx, bga, a_param):
    t, d = r.shape
    tt = min(TOKEN_TILE, t)
    row = lambda i: (i, 0)
    fixed2 = lambda i: (0, 0)
    fixed3 = lambda i: (0, 0, 0)
    return pl.pallas_call(
        _lru_core_kernel,
        grid=(t // tt,),
        in_specs=[pl.BlockSpec((tt, d), row),
                  pl.BlockSpec((tt, d), row),
                  pl.BlockSpec(conv_w.shape, fixed2),
                  pl.BlockSpec((1, d), fixed2),
                  pl.BlockSpec(wgx.shape, fixed3),
                  pl.BlockSpec(wga.shape, fixed3),
                  pl.BlockSpec((1, d), fixed2),
                  pl.BlockSpec((1, d), fixed2),
                  pl.BlockSpec((1, d), fixed2)],
        out_specs=pl.BlockSpec((tt, d), row),
        out_shape=jax.ShapeDtypeStruct((t, d), BF16),
        scratch_shapes=[pltpu.VMEM((8, d), F32),
                        pltpu.VMEM((1, d), F32),
                        pltpu.VMEM((tt, d), F32),
                        pltpu.VMEM((tt, d), F32)],
        compiler_params=_params(1),
        name="lru_core",
    )(r, y, conv_w, conv_b, wgx, wga, bgx, bga, a_param)


def _moe(xn_tiles, t, d, ids, gates, rank, counts128, layer, wg, wu, wd, bg, bu, bd):
    tm = EXPERT_ROWS
    n_exp = wg.shape[1]
    counts = counts128[:, 0].astype(I32)
    nch = (counts + tm - 1) // tm
    start = (jnp.cumsum(nch) - nch).astype(I32) * tm
    experts = jnp.arange(n_exp, dtype=I32)
    dest = rank + jnp.sum(jnp.where(ids[:, :, None] == experts, start, 0), axis=-1)
    last = ((jnp.sum(nch) - 1) * tm).astype(I32).reshape(1)
    tok_bits = (t + COMBINE_BATCH - 1).bit_length()
    gates_ext = jnp.pad(gates, ((0, 1), (0, (1 << tok_bits) - t))).reshape(-1)
    return _moe_experts(xn_tiles, t, d, tok_bits, dest.reshape(-1).astype(I32), gates_ext,
                        start, counts, nch.astype(I32), last, layer, wg, wu, wd, bg, bu, bd)


def kernel(x, rel_bias_table, w_qkv, w_attn_out, w_lru_in, lru_conv_w, lru_conv_b, w_lru_gate_x, b_lru_gate_x, w_lru_gate_a, b_lru_gate_a, lru_a_param, w_lru_out, w_router, b_router, w_exp_gate, b_exp_gate, w_exp_up, b_exp_up, w_exp_down, b_exp_down, ln_mix_g, ln_mix_b, ln_ffn_g, ln_ffn_b):
    bsz, t, d = x.shape
    depth = w_router.shape[0]
    n_mixers = 2
    alpha = (2 * depth) ** 0.25
    assert t % MOBA_BLOCK == 0 and t % TOKEN_TILE == 0

    biases = _attn_biases(rel_bias_table, t)
    w_qkv_b = w_qkv.astype(BF16)
    w_attn_out_b = w_attn_out.astype(BF16)
    w_lru_in_b = w_lru_in.astype(BF16)
    w_lru_out_b = w_lru_out.astype(BF16)
    wgx_b = w_lru_gate_x.astype(BF16)
    wga_b = w_lru_gate_a.astype(BF16)
    wg_b = w_exp_gate.astype(BF16)
    wu_b = w_exp_up.astype(BF16)
    wd_b = w_exp_down.astype(BF16)
    wr_t = jnp.swapaxes(w_router, 1, 2)
    wr_hi = wr_t.astype(BF16)
    wr_lo = (wr_t - wr_hi.astype(F32)).astype(BF16)

    bg4 = b_exp_gate[:, :, None, :]
    bu4 = b_exp_up[:, :, None, :]
    bd4 = b_exp_down[:, :, None, :]

    outs = []
    for b in range(bsz):
        xs = x[b]
        for i in range(depth):
            j = i // n_mixers
            if i % n_mixers == 0:
                qkv = _matmul(xs, w_qkv_b[j], BF16)
                a = _moba_attention(qkv, biases)
                w_out = w_attn_out_b[j]
            else:
                y, r = _lru_in(xs, w_lru_in_b[j])
                a = _lru_core(r, y, lru_conv_w[j], lru_conv_b[j][None], wgx_b[j], wga_b[j],
                              b_lru_gate_x[j][None], b_lru_gate_a[j][None], lru_a_param[j][None])
                w_out = w_lru_out_b[j]
            xn, ids, gates, rank, counts = _proj_ln_router(
                a, w_out, xs, ln_mix_g[i][None], ln_mix_b[i][None],
                wr_hi[i], wr_lo[i], b_router[i][:, None], alpha)
            ffn = _moe(xn, t, d, ids, gates, rank, counts, i, wg_b, wu_b, wd_b, bg4, bu4, bd4)
            xs = _add_ln(xn, ffn, t, d, ln_ffn_g[i][None], ln_ffn_b[i][None], alpha)
        outs.append(xs)
    return jnp.stack(outs, axis=0)
```

```python
import functools
import math

import numpy as np
import jax
import jax.numpy as jnp
from jax import lax
from jax.experimental import pallas as pl
from jax.experimental.pallas import tpu as pltpu

F32 = jnp.float32
BF16 = jnp.bfloat16
I32 = jnp.int32

N_HEADS = 8
MOBA_BLOCK = 256
MOBA_TOPK = 3
NUM_BUCKETS = 32
REL_MAX_DISTANCE = 128
LRU_C = 8.0
EXPERT_TOPK = 4
SWIGLU_LIMIT = 7.0
SWIGLU_ALPHA = 1.702
LN_EPS = 1e-5
LOG2_E = 1.4426950408889634

EXPERT_ROWS = 256
TOKEN_TILE = 512
SCAN_ROWS = 8
LANES = 128
HEAD_GROUP = 8
COMBINE_BATCH = 8
V7X_VMEM_LIMIT = 56 * 1024 * 1024

_NT = (((1,), (1,)), ((), ()))


def _params(n_axes, vmem=V7X_VMEM_LIMIT):
    return pltpu.CompilerParams(
        dimension_semantics=("arbitrary",) * n_axes, vmem_limit_bytes=vmem)


def _sigmoid(x):
    return 0.5 * jnp.tanh(0.5 * x) + 0.5


def _layer_norm(z, g, b):
    mu = jnp.mean(z, axis=-1, keepdims=True)
    zc = z - mu
    var = jnp.mean(zc * zc, axis=-1, keepdims=True)
    return zc * lax.rsqrt(var + LN_EPS) * g + b


def _rows_from_tiles(ref, n_sub, first=0, rows=None, dtype=None):
    rows = ref.shape[0] // n_sub if rows is None else rows
    cols = [ref[pl.ds(first + s, rows, stride=n_sub), :] for s in range(n_sub)]
    if dtype is not None:
        cols = [c.astype(dtype) for c in cols]
    return jnp.concatenate(cols, axis=1)


def _rows_to_tiles(ref, value, first=0):
    n_sub = value.shape[1] // LANES
    for s in range(n_sub):
        ref[pl.ds(first + s, value.shape[0], stride=n_sub), :] = value[:, s * LANES:(s + 1) * LANES]


def _matmul_kernel(x_ref, w_ref, o_ref):
    x = _rows_from_tiles(x_ref, w_ref.shape[0] // LANES, dtype=BF16)
    o_ref[...] = jnp.dot(x, w_ref[...], preferred_element_type=F32).astype(o_ref.dtype)


def _matmul(x_tiles, w, out_dtype):
    k, n = w.shape
    n_sub = k // LANES
    m = x_tiles.shape[0] // n_sub - COMBINE_BATCH
    tm = min(TOKEN_TILE, m)
    return pl.pallas_call(
        _matmul_kernel,
        grid=(m // tm,),
        in_specs=[pl.BlockSpec((tm * n_sub, LANES), lambda i: (i, 0)),
                  pl.BlockSpec((k, n), lambda i: (0, 0))],
        out_specs=pl.BlockSpec((tm, n), lambda i: (i, 0)),
        out_shape=jax.ShapeDtypeStruct((m, n), out_dtype),
        compiler_params=_params(1),
        name="dense_proj",
    )(x_tiles, w)


def _attn_kernel(q_ref, k_ref, v_ref, bself_ref, badj_ref, bfar_ref, o_ref,
                 kmean_ref, vt_ref, pen_ref, m_ref, l_ref, acc_ref, sa_ref, sb_ref, *, nblk, scale):
    blk = MOBA_BLOCK
    hg = bself_ref.shape[0]
    dh = q_ref.shape[1] // hg
    i = pl.program_id(1)

    def head(ref, h):
        return ref.at[:, h * dh:(h + 1) * dh]

    @pl.when(i == 0)
    def _():
        for h in range(hg):
            for j in range(nblk):
                kj = head(k_ref, h)[j * blk:(j + 1) * blk, :].astype(F32)
                kmean_ref[h, j:j + 1, :] = jnp.sum(kj, axis=0, keepdims=True) * (1.0 / blk)
                vt_ref[h, j] = head(v_ref, h)[j * blk:(j + 1) * blk, :].astype(F32).T.astype(BF16)

    qs = []
    for h in range(hg):
        q = head(q_ref, h)[...]

        km = kmean_ref[h]
        km_hi = km.astype(BF16)
        km_lo = (km - km_hi.astype(F32)).astype(BF16)
        gate = (lax.dot_general(km_hi, q, _NT, preferred_element_type=F32)
                + lax.dot_general(km_lo, q, _NT, preferred_element_type=F32))
        jj = lax.broadcasted_iota(I32, gate.shape, 0)
        past = jj < i
        g = jnp.where(past, gate, -jnp.inf)
        sel = jnp.zeros(gate.shape, jnp.bool_)
        for _ in range(min(MOBA_TOPK, nblk)):
            mx = jnp.max(g, axis=0, keepdims=True)
            jm = jnp.min(jnp.where(g == mx, jj, nblk), axis=0, keepdims=True)
            pick = jj == jm
            sel = jnp.logical_or(sel, pick)
            g = jnp.where(pick, -jnp.inf, g)
        pen_ref[h] = jnp.where(jnp.logical_and(sel, past), 0.0, -jnp.inf)
        qs.append((q.astype(F32) * (scale * LOG2_E)).astype(BF16))

    def scores(h, j):
        kb = head(k_ref, h)[pl.ds(pl.multiple_of(j * blk, blk), blk), :]
        return lax.dot_general(kb, qs[h], _NT, preferred_element_type=F32)

    def fold(x, op):
        return op(x.reshape(blk // 8, 8, blk), axis=0)

    def adj_bias(h):
        return badj_ref[h] + pen_ref[h, pl.ds(jnp.maximum(i - 1, 0), 1), :]

    def far_bias(h, j):
        return bfar_ref[h, 0:1, 0:1] + pen_ref[h, pl.ds(j, 1), :]

    n_far = jnp.maximum(i - 1, 0)


    ss = [scores(h, i) + bself_ref[h] for h in range(hg)]
    for h in range(hg):
        m0 = jnp.max(fold(ss[h], jnp.max), axis=0, keepdims=True)
        p = jnp.exp2(ss[h] - m0)
        m_ref[h] = m0
        l_ref[h] = fold(p, jnp.sum)
        acc_ref[h] = jnp.dot(vt_ref[h, i], p.astype(BF16), preferred_element_type=F32)

    def update(h, j, s, s_max, bias):
        if bias is not None:
            s_max = s_max + bias
        m_old = m_ref[h]
        m_new = jnp.maximum(m_old, jnp.max(s_max, axis=0, keepdims=True))
        alpha = jnp.exp2(m_old - m_new)
        p = jnp.exp2(s - m_new if bias is None else s + (bias - m_new))
        m_ref[h] = m_new
        l_ref[h] = alpha * l_ref[h] + fold(p, jnp.sum)
        acc_ref[h] = alpha * acc_ref[h] + jnp.dot(
            vt_ref[h, j], p.astype(BF16), preferred_element_type=F32)

    @pl.when(i >= 1)
    def _():
        ss = [scores(h, i - 1) + adj_bias(h) for h in range(hg)]
        for h in range(hg):
            update(h, i - 1, ss[h], fold(ss[h], jnp.max), None)

    @pl.when(n_far >= 1)
    def _():
        for h in range(hg):
            sa_ref[h] = scores(h, 0)

    def stage(j, cur_ref, nxt_ref):
        nxt = jnp.minimum(j + 1, n_far - 1)
        for h in range(hg):
            nxt_ref[h] = scores(h, nxt)
        for h in range(hg):
            s = cur_ref[h]
            update(h, j, s, fold(s, jnp.max), far_bias(h, j))

    def far_pair(p, carry):
        stage(2 * p, sa_ref, sb_ref)

        @pl.when(2 * p + 1 < n_far)
        def _():
            stage(2 * p + 1, sb_ref, sa_ref)

        return carry

    lax.fori_loop(0, (n_far + 1) // 2, far_pair, 0)

    for h in range(hg):
        o = acc_ref[h] / jnp.sum(l_ref[h], axis=0, keepdims=True)
        head(o_ref, h)[...] = o.T.astype(o_ref.dtype)


def _t5_bucket_np(dist):
    max_exact = NUM_BUCKETS // 2
    n = dist.astype(np.float32)
    large = max_exact + (np.log(np.maximum(n, np.float32(1.0)) / np.float32(max_exact))
                         / np.float32(math.log(REL_MAX_DISTANCE / max_exact))
                         * np.float32(NUM_BUCKETS - max_exact)).astype(np.int32)
    large = np.minimum(large, NUM_BUCKETS - 1)
    return np.where(dist < max_exact, dist, large)


def _t5_bucket(dist):
    max_exact = NUM_BUCKETS // 2
    n = dist.astype(F32)
    large = max_exact + (jnp.log(jnp.maximum(n, 1.0) / max_exact)
                         / math.log(REL_MAX_DISTANCE / max_exact)
                         * (NUM_BUCKETS - max_exact)).astype(I32)
    large = jnp.minimum(large, NUM_BUCKETS - 1)
    return jnp.where(dist < max_exact, dist, large)


def _attn_biases(rel_bias_table, t):
    blk = MOBA_BLOCK
    far_buckets = _t5_bucket_np(np.arange(blk + 1, max(t, 2 * blk) + 1, dtype=np.int32))
    assert (far_buckets == NUM_BUCKETS - 1).all()
    bbd = rel_bias_table.astype(F32)[_t5_bucket(jnp.arange(2 * blk + 1, dtype=I32))].T * LOG2_E
    n_heads = bbd.shape[0]

    def toeplitz(w):
        rep = jnp.tile(w, (1, blk))[:, :blk * (2 * blk - 1)].reshape(n_heads, blk, 2 * blk - 1)
        return rep[:, :, blk - 1:]

    w_self = jnp.concatenate([jnp.full((n_heads, blk - 1), -jnp.inf, F32), bbd[:, :blk + 1]], axis=1)
    w_adj = jnp.concatenate([bbd[:, 1:2 * blk], bbd[:, :1]], axis=1)
    bself = toeplitz(w_self)
    badj = toeplitz(w_adj)
    bfar = jnp.broadcast_to(bbd[:, 2 * blk][:, None, None], (n_heads, 8, 128))
    return bself, badj, bfar


def _moba_attention(qkv, biases):
    t, d3 = qkv.shape
    d = d3 // 3
    dh = d // N_HEADS
    blk = MOBA_BLOCK
    nblk = t // blk
    bself, badj, bfar = biases
    hg = HEAD_GROUP
    n_grp = N_HEADS // hg
    kernel = functools.partial(_attn_kernel, nblk=nblk, scale=dh ** -0.5)
    return pl.pallas_call(
        kernel,
        grid=(n_grp, nblk),
        in_specs=[pl.BlockSpec((blk, hg * dh), lambda g, i: (i, g)),
                  pl.BlockSpec((t, hg * dh), lambda g, i: (0, n_grp + g)),
                  pl.BlockSpec((t, hg * dh), lambda g, i: (0, 2 * n_grp + g)),
                  pl.BlockSpec((hg, blk, blk), lambda g, i: (g, 0, 0)),
                  pl.BlockSpec((hg, blk, blk), lambda g, i: (g, 0, 0)),
                  pl.BlockSpec((hg, 8, LANES), lambda g, i: (g, 0, 0))],
        out_specs=pl.BlockSpec((blk, hg * dh), lambda g, i: (i, g)),
        out_shape=jax.ShapeDtypeStruct((t, d), BF16),
        scratch_shapes=[pltpu.VMEM((hg, nblk, dh), F32),
                        pltpu.VMEM((hg, nblk, dh, blk), BF16),
                        pltpu.VMEM((hg, nblk, blk), F32),
                        pltpu.VMEM((hg, 1, blk), F32),
                        pltpu.VMEM((hg, 8, blk), F32),
                        pltpu.VMEM((hg, dh, blk), F32),
                        pltpu.VMEM((hg, blk, blk), F32),
                        pltpu.VMEM((hg, blk, blk), F32)],
        compiler_params=_params(2),
        name="moba_attention",
    )(qkv, qkv, qkv, bself, badj, bfar)


def _proj_ln_router_kernel(a_ref, w_ref, x_ref, g_ref, b_ref, wrh_ref, wrl_ref, br_ref,
                           xn_ref, ids_ref, gates_ref, rank_ref, counts_ref, base_ref,
                           *, alpha, n_exp):
    t = pl.program_id(0)
    tm = a_ref.shape[0]

    @pl.when(t == 0)
    def _():
        base_ref[...] = jnp.zeros_like(base_ref)

    mix = jnp.dot(a_ref[...].astype(BF16), w_ref[...], preferred_element_type=F32)
    x_res = _rows_from_tiles(x_ref, w_ref.shape[1] // LANES)
    xn = _layer_norm(alpha * x_res + mix, g_ref[...], b_ref[...])
    _rows_to_tiles(xn_ref, xn)

    x_hi = xn.astype(BF16)
    x_lo = (xn - x_hi.astype(F32)).astype(BF16)
    wr_hi = wrh_ref[...]
    logits = (lax.dot_general(wr_hi, x_hi, _NT, preferred_element_type=F32)
              + lax.dot_general(wr_hi, x_lo, _NT, preferred_element_type=F32)
              + lax.dot_general(wrl_ref[...], x_hi, _NT, preferred_element_type=F32)
              + br_ref[...])

    e_iota = lax.broadcasted_iota(I32, logits.shape, 0)
    work = logits
    tops, picks = [], []
    for k in range(EXPERT_TOPK):
        mx = jnp.max(work, axis=0, keepdims=True)
        em = jnp.min(jnp.where(work == mx, e_iota, n_exp), axis=0, keepdims=True)
        pick = e_iota == em
        ids_ref[k:k + 1, :] = em
        tops.append(mx)
        picks.append(pick)
        work = jnp.where(pick, -jnp.inf, work)

    ex = [jnp.exp(tk - tops[0]) for tk in tops]
    den = ex[0]
    for e in ex[1:]:
        den = den + e
    for k in range(EXPERT_TOPK):
        gates_ref[k:k + 1, :] = ex[k] / den

    onehot = jnp.zeros(logits.shape, F32)
    for pick in picks:
        onehot = onehot + pick.astype(F32)
    earlier = (lax.broadcasted_iota(I32, (tm, tm), 0)
               < lax.broadcasted_iota(I32, (tm, tm), 1)).astype(BF16)
    pos = jnp.dot(onehot.astype(BF16), earlier, preferred_element_type=F32) + base_ref[:, 0:1]
    for k in range(EXPERT_TOPK):
        rank_ref[k:k + 1, :] = jnp.sum(jnp.where(picks[k], pos, 0.0), axis=0,
                                       keepdims=True).astype(I32)
    total = base_ref[...] + jnp.sum(onehot, axis=1, keepdims=True)
    base_ref[...] = total
    counts_ref[...] = total


def _proj_ln_router(a, w, x_tiles, ln_g, ln_b, wr_hi, wr_lo, br, alpha):
    t, d = a.shape
    n_exp = wr_hi.shape[0]
    tm = min(TOKEN_TILE, t)
    kernel = functools.partial(_proj_ln_router_kernel, alpha=alpha, n_exp=n_exp)
    row = lambda i: (i, 0)
    fixed = lambda i: (0, 0)
    col = lambda i: (0, i)
    return pl.pallas_call(
        kernel,
        grid=(t // tm,),
        in_specs=[pl.BlockSpec((tm, d), row),
                  pl.BlockSpec((d, d), fixed),
                  pl.BlockSpec((tm * (d // LANES), LANES), row),
                  pl.BlockSpec((1, d), fixed),
                  pl.BlockSpec((1, d), fixed),
                  pl.BlockSpec((n_exp, d), fixed),
                  pl.BlockSpec((n_exp, d), fixed),
                  pl.BlockSpec((n_exp, 1), fixed)],
        out_specs=[pl.BlockSpec((tm * (d // LANES), LANES), row),
                   pl.BlockSpec((EXPERT_TOPK, tm), col),
                   pl.BlockSpec((EXPERT_TOPK, tm), col),
                   pl.BlockSpec((EXPERT_TOPK, tm), col),
                   pl.BlockSpec((n_exp, 128), fixed)],
        out_shape=[jax.ShapeDtypeStruct((t * (d // LANES), LANES), F32),
                   jax.ShapeDtypeStruct((EXPERT_TOPK, t), I32),
                   jax.ShapeDtypeStruct((EXPERT_TOPK, t), F32),
                   jax.ShapeDtypeStruct((EXPERT_TOPK, t), I32),
                   jax.ShapeDtypeStruct((n_exp, 128), F32)],
        scratch_shapes=[pltpu.VMEM((n_exp, 128), F32)],
        compiler_params=_params(1),
        name="proj_ln_router",
    )(a, w, x_tiles, ln_g, ln_b, wr_hi, wr_lo, br)


def _moe_kernel(dest_ref, gate_ref, start_ref, count_ref, nch_ref, last_ref,
                x2_ref, wg_ref, wu_ref, wd_ref, bg_ref, bu_ref, bd_ref, lng_ref, lnb_ref,
                out_ref, src_ref, xg_ref, yf_ref, *, n_tok, n_exp, tok_bits, ln_alpha):
    e = pl.program_id(0)
    n_sub = wg_ref.shape[0] // LANES
    tm = xg_ref.shape[0] // n_sub
    spare = COMBINE_BATCH

    def tile(ref, idx):
        return ref.at[pl.ds(pl.multiple_of(idx * n_sub, n_sub), n_sub), :]

    tok_mask = (1 << tok_bits) - 1

    def gather_row(base, r):
        tok = jnp.minimum(src_ref[base + r] & tok_mask, n_tok - 1)
        tile(xg_ref, r)[...] = tile(x2_ref, tok)[...]

    def combine_batch(base, r0):
        codes = [src_ref[base + r0 + k] for k in range(spare)]
        toks = [c & tok_mask for c in codes]
        gs = [gate_ref[c] for c in codes]
        olds = [tile(out_ref, toks[k])[...] for k in range(spare)]
        for k in range(spare):
            tile(out_ref, toks[k])[...] = olds[k] + gs[k] * tile(yf_ref, r0 + k)[...]

    @pl.when(e == 0)
    def _():
        out_ref[...] = jnp.zeros_like(out_ref)
        yf_ref[...] = jnp.zeros_like(yf_ref)

        def fill(r, carry):
            src_ref[r] = n_tok + r % spare + (EXPERT_TOPK << tok_bits)
            return carry

        for ee in range(n_exp):
            lax.fori_loop(start_ref[ee] + count_ref[ee], start_ref[ee] + nch_ref[ee] * tm, fill, 0)

        def scatter(a, carry):
            for k in range(EXPERT_TOPK):
                src_ref[dest_ref[k * n_tok + a]] = a + (k << tok_bits)
            return carry

        lax.fori_loop(0, n_tok, scatter, 0, unroll=8)

        def first_rows(r, carry):
            gather_row(0, r)
            return carry

        lax.fori_loop(0, tm, first_rows, 0, unroll=8)

    def visit(base, rows):
        x = jnp.concatenate(
            [xg_ref[pl.ds(s, rows, stride=n_sub), :].astype(BF16) for s in range(n_sub)], axis=1)

        nxt = jnp.minimum(base + tm, last_ref[0])
        for r in range(tm):
            gather_row(nxt, r)
        prev = jnp.maximum(base - tm, 0)
        for r0 in range(0, tm, spare):
            combine_batch(prev, r0)

        g = jnp.dot(x, wg_ref[...], preferred_element_type=F32) + bg_ref[...]
        u = jnp.dot(x, wu_ref[...], preferred_element_type=F32) + bu_ref[...]
        g = jnp.minimum(g, SWIGLU_LIMIT)
        u = jnp.clip(u, -SWIGLU_LIMIT, SWIGLU_LIMIT)
        h = g * _sigmoid(SWIGLU_ALPHA * g) * (u + 1.0)
        y = jnp.dot(h.astype(BF16), wd_ref[...], preferred_element_type=F32) + bd_ref[...]
        for s in range(n_sub):
            yf_ref[pl.ds(s, rows, stride=n_sub), :] = y[:, s * LANES:(s + 1) * LANES]

    def chunk(ch, carry):
        base = start_ref[e] + ch * tm
        n_real = count_ref[e] - ch * tm

        @pl.when(n_real > tm // 2)
        def _():
            visit(base, tm)

        @pl.when(n_real <= tm // 2)
        def _():
            visit(base, tm // 2)

        return carry

    lax.fori_loop(0, nch_ref[e], chunk, 0)

    @pl.when(e == n_exp - 1)
    def _():
        def last_rows(b, carry):
            combine_batch(last_ref[0], b * spare)
            return carry

        lax.fori_loop(0, tm // spare, last_rows, 0)

        ln_rows = min(EXPERT_ROWS, n_tok)

        def normalize(b, carry):
            first = pl.multiple_of(b * (ln_rows * n_sub), ln_rows * n_sub)
            z = (ln_alpha * _rows_from_tiles(x2_ref, n_sub, first, ln_rows)
                 + _rows_from_tiles(out_ref, n_sub, first, ln_rows))
            _rows_to_tiles(out_ref, _layer_norm(z, lng_ref[...], lnb_ref[...]), first)
            return carry

        lax.fori_loop(0, n_tok // ln_rows, normalize, 0)


def _moe_experts(x_tiles, t, d, tok_bits, dest, gates, start, count, nch, last, layer,
                 wg, wu, wd, bg, bu, bd, ln_g, ln_b, alpha):
    n_sub = d // LANES
    tm = EXPERT_ROWS
    n_exp, dff = wg.shape[1], wg.shape[3]
    n_rows = t * EXPERT_TOPK + n_exp * tm

    def w_map(e, *_):
        return (layer, e, 0, 0)

    out_rows = (t + COMBINE_BATCH) * n_sub
    grid_spec = pltpu.PrefetchScalarGridSpec(
        num_scalar_prefetch=6,
        grid=(n_exp,),
        in_specs=[pl.BlockSpec((t * n_sub, LANES), lambda e, *_: (0, 0)),
                  pl.BlockSpec((None, None, d, dff), w_map),
                  pl.BlockSpec((None, None, d, dff), w_map),
                  pl.BlockSpec((None, None, dff, d), w_map),
                  pl.BlockSpec((None, None, 1, dff), w_map),
                  pl.BlockSpec((None, None, 1, dff), w_map),
                  pl.BlockSpec((None, None, 1, d), w_map),
                  pl.BlockSpec((1, d), lambda e, *_: (0, 0)),
                  pl.BlockSpec((1, d), lambda e, *_: (0, 0))],
        out_specs=pl.BlockSpec((out_rows, LANES), lambda e, *_: (0, 0)),
        scratch_shapes=[pltpu.SMEM((n_rows,), I32),
                        pltpu.VMEM((tm * n_sub, LANES), F32),
                        pltpu.VMEM((tm * n_sub, LANES), F32)],
    )
    return pl.pallas_call(
        functools.partial(_moe_kernel, n_tok=t, n_exp=n_exp, tok_bits=tok_bits, ln_alpha=alpha),
        grid_spec=grid_spec,
        out_shape=jax.ShapeDtypeStruct((out_rows, LANES), F32),
        compiler_params=_params(1),
        name="moe_experts",
    )(dest, gates, start, count, nch, last, x_tiles, wg, wu, wd, bg, bu, bd, ln_g, ln_b)


def _lru_in_kernel(x_ref, w_ref, y_ref, r_ref):
    d = y_ref.shape[1]
    x = _rows_from_tiles(x_ref, w_ref.shape[0] // LANES, dtype=BF16)
    yz = jnp.dot(x, w_ref[...], preferred_element_type=F32)
    y = yz[:, :d]
    y_ref[...] = 0.5 * y * (1.0 + jnp.tanh(math.sqrt(2.0 / math.pi) * (y + 0.044715 * (y * y * y))))
    r_ref[...] = yz[:, d:]


def _lru_in(x_tiles, w_in):
    d = w_in.shape[0]
    n_sub = d // LANES
    t = x_tiles.shape[0] // n_sub - COMBINE_BATCH
    n = w_in.shape[1] // 2
    tm = min(TOKEN_TILE, t)
    return pl.pallas_call(
        _lru_in_kernel,
        grid=(t // tm,),
        in_specs=[pl.BlockSpec((tm * n_sub, LANES), lambda i: (i, 0)),
                  pl.BlockSpec((d, 2 * n), lambda i: (0, 0))],
        out_specs=[pl.BlockSpec((tm, n), lambda i: (i, 0)),
                   pl.BlockSpec((tm, n), lambda i: (i, 0))],
        out_shape=[jax.ShapeDtypeStruct((t, n), F32), jax.ShapeDtypeStruct((t, n), F32)],
        compiler_params=_params(1),
        name="lru_in",
    )(x_tiles, w_in)


def _lru_core_kernel(r_ref, y_ref, cw_ref, cb_ref, wgx_ref, wga_ref, bgx_ref, bga_ref, ap_ref,
                     hy_ref, rprev_ref, carry_ref, a_ref, u_ref):
    t = pl.program_id(0)
    tt, d = r_ref.shape
    n_blk, bw, _ = wgx_ref.shape
    cwid = cw_ref.shape[0]

    @pl.when(t == 0)
    def _():
        rprev_ref[...] = jnp.zeros_like(rprev_ref)
        carry_ref[...] = jnp.zeros_like(carry_ref)

    r = r_ref[...]
    rext = jnp.concatenate([rprev_ref[...], r], axis=0)
    rc = jnp.broadcast_to(cb_ref[...], (tt, d))
    for w in range(cwid):
        off = 8 - (cwid - 1) + w
        rc = rc + cw_ref[w:w + 1, :] * rext[off:off + tt, :]
    rprev_ref[...] = r[tt - 8:, :]

    rcb = rc.astype(BF16)
    gx = jnp.concatenate(
        [jnp.dot(rcb[:, n * bw:(n + 1) * bw], wgx_ref[n], preferred_element_type=F32)
         for n in range(n_blk)], axis=1)
    ga = jnp.concatenate(
        [jnp.dot(rcb[:, n * bw:(n + 1) * bw], wga_ref[n], preferred_element_type=F32)
         for n in range(n_blk)], axis=1)
    gate_x = _sigmoid(gx + bgx_ref[...])
    gate_a = _sigmoid(ga + bga_ref[...])
    z = -ap_ref[...]
    softplus = jnp.maximum(z, 0.0) + jnp.log(1.0 + jnp.exp(-jnp.abs(z)))
    log_a = -LRU_C * gate_a * softplus
    a_ref[...] = jnp.exp(log_a)
    u_ref[...] = jnp.sqrt(1.0 - jnp.exp(2.0 * log_a)) * (gate_x * rc)

    row = lax.broadcasted_iota(I32, (SCAN_ROWS, d), 0)

    def slab(s, carry):
        off = pl.multiple_of(s * SCAN_ROWS, SCAN_ROWS)
        a = a_ref[pl.ds(off, SCAN_ROWS), :]
        b = u_ref[pl.ds(off, SCAN_ROWS), :]
        step = 1
        while step < SCAN_ROWS:
            a_sh = pltpu.roll(a, step, 0)
            b_sh = pltpu.roll(b, step, 0)
            live = row >= step
            b = jnp.where(live, a * b_sh + b, b)
            a = jnp.where(live, a * a_sh, a)
            step *= 2
        h = b + a * carry
        u_ref[pl.ds(off, SCAN_ROWS), :] = h
        return h[SCAN_ROWS - 1:SCAN_ROWS, :]

    carry_ref[...] = lax.fori_loop(0, tt // SCAN_ROWS, slab, carry_ref[...])
    hy_ref[...] = (u_ref[...] * y_ref[...]).astype(hy_ref.dtype)


def _lru_core(r, y, conv_w, conv_b, wgx, wga, bgx, bga, a_param):
    t, d = r.shape
    tt = min(TOKEN_TILE, t)
    row = lambda i: (i, 0)
    fixed2 = lambda i: (0, 0)
    fixed3 = lambda i: (0, 0, 0)
    return pl.pallas_call(
        _lru_core_kernel,
        grid=(t // tt,),
        in_specs=[pl.BlockSpec((tt, d), row),
                  pl.BlockSpec((tt, d), row),
                  pl.BlockSpec(conv_w.shape, fixed2),
                  pl.BlockSpec((1, d), fixed2),
                  pl.BlockSpec(wgx.shape, fixed3),
                  pl.BlockSpec(wga.shape, fixed3),
                  pl.BlockSpec((1, d), fixed2),
                  pl.BlockSpec((1, d), fixed2),
                  pl.BlockSpec((1, d), fixed2)],
        out_specs=pl.BlockSpec((tt, d), row),
        out_shape=jax.ShapeDtypeStruct((t, d), BF16),
        scratch_shapes=[pltpu.VMEM((8, d), F32),
                        pltpu.VMEM((1, d), F32),
                        pltpu.VMEM((tt, d), F32),
                        pltpu.VMEM((tt, d), F32)],
        compiler_params=_params(1),
        name="lru_core",
    )(r, y, conv_w, conv_b, wgx, wga, bgx, bga, a_param)


def _moe(xn_tiles, t, d, ids, gates, rank, counts128, layer, wg, wu, wd, bg, bu, bd,
         ln_g, ln_b, alpha):
    tm = EXPERT_ROWS
    n_exp = wg.shape[1]
    counts = counts128[:, 0].astype(I32)
    nch = (counts + tm - 1) // tm
    start = (jnp.cumsum(nch) - nch).astype(I32) * tm
    experts = jnp.arange(n_exp, dtype=I32)
    dest = rank + jnp.sum(jnp.where(ids[:, :, None] == experts, start, 0), axis=-1)
    last = ((jnp.sum(nch) - 1) * tm).astype(I32).reshape(1)
    tok_bits = (t + COMBINE_BATCH - 1).bit_length()
    gates_ext = jnp.pad(gates, ((0, 1), (0, (1 << tok_bits) - t))).reshape(-1)
    return _moe_experts(xn_tiles, t, d, tok_bits, dest.reshape(-1).astype(I32), gates_ext,
                        start, counts, nch.astype(I32), last, layer, wg, wu, wd, bg, bu, bd,
                        ln_g, ln_b, alpha)


def kernel(x, rel_bias_table, w_qkv, w_attn_out, w_lru_in, lru_conv_w, lru_conv_b, w_lru_gate_x, b_lru_gate_x, w_lru_gate_a, b_lru_gate_a, lru_a_param, w_lru_out, w_router, b_router, w_exp_gate, b_exp_gate, w_exp_up, b_exp_up, w_exp_down, b_exp_down, ln_mix_g, ln_mix_b, ln_ffn_g, ln_ffn_b):
    bsz, t, d = x.shape
    depth = w_router.shape[0]
    n_mixers = 2
    alpha = (2 * depth) ** 0.25
    assert t % MOBA_BLOCK == 0 and t % TOKEN_TILE == 0

    biases = _attn_biases(rel_bias_table, t)
    w_qkv_b = w_qkv.astype(BF16)
    w_attn_out_b = w_attn_out.astype(BF16)
    w_lru_in_b = w_lru_in.astype(BF16)
    w_lru_out_b = w_lru_out.astype(BF16)
    wgx_b = w_lru_gate_x.astype(BF16)
    wga_b = w_lru_gate_a.astype(BF16)
    wg_b = w_exp_gate.astype(BF16)
    wu_b = w_exp_up.astype(BF16)
    wd_b = w_exp_down.astype(BF16)
    wr_t = jnp.swapaxes(w_router, 1, 2)
    wr_hi = wr_t.astype(BF16)
    wr_lo = (wr_t - wr_hi.astype(F32)).astype(BF16)

    bg4 = b_exp_gate[:, :, None, :]
    bu4 = b_exp_up[:, :, None, :]
    bd4 = b_exp_down[:, :, None, :]

    n_sub = d // LANES
    outs = []
    for b in range(bsz):
        xs = jnp.pad(x[b], ((0, COMBINE_BATCH), (0, 0))).reshape((t + COMBINE_BATCH) * n_sub, LANES)
        for i in range(depth):
            j = i // n_mixers
            if i % n_mixers == 0:
                qkv = _matmul(xs, w_qkv_b[j], BF16)
                a = _moba_attention(qkv, biases)
                w_out = w_attn_out_b[j]
            else:
                y, r = _lru_in(xs, w_lru_in_b[j])
                a = _lru_core(r, y, lru_conv_w[j], lru_conv_b[j][None], wgx_b[j], wga_b[j],
                              b_lru_gate_x[j][None], b_lru_gate_a[j][None], lru_a_param[j][None])
                w_out = w_lru_out_b[j]
            xn, ids, gates, rank, counts = _proj_ln_router(
                a, w_out, xs, ln_mix_g[i][None], ln_mix_b[i][None],
                wr_hi[i], wr_lo[i], b_router[i][:, None], alpha)
            xs = _moe(xn, t, d, ids, gates, rank, counts, i, wg_b, wu_b, wd_b, bg4, bu4, bd4,
                      ln_ffn_g[i][None], ln_ffn_b[i][None], alpha)
        outs.append(xs[:t * n_sub].reshape(t, d))
    return jnp.stack(outs, axis=0)
```

```python
import functools
import math

import numpy as np
import jax
import jax.numpy as jnp
from jax import lax
from jax.experimental import pallas as pl
from jax.experimental.pallas import tpu as pltpu

F32 = jnp.float32
BF16 = jnp.bfloat16
I32 = jnp.int32

N_HEADS = 8
MOBA_BLOCK = 256
MOBA_TOPK = 3
NUM_BUCKETS = 32
REL_MAX_DISTANCE = 128
LRU_C = 8.0
EXPERT_TOPK = 4
SWIGLU_LIMIT = 7.0
SWIGLU_ALPHA = 1.702
LN_EPS = 1e-5
LOG2_E = 1.4426950408889634

EXPERT_ROWS = 256
TOKEN_TILE = 512
SCAN_ROWS = 8
LANES = 128
HEAD_GROUP = 8
COMBINE_BATCH = 8
V7X_VMEM_LIMIT = 56 * 1024 * 1024

_NT = (((1,), (1,)), ((), ()))


def _params(n_axes, vmem=V7X_VMEM_LIMIT):
    return pltpu.CompilerParams(
        dimension_semantics=("arbitrary",) * n_axes, vmem_limit_bytes=vmem)


def _sigmoid(x):
    return 0.5 * jnp.tanh(0.5 * x) + 0.5


def _layer_norm(z, g, b):
    mu = jnp.mean(z, axis=-1, keepdims=True)
    zc = z - mu
    var = jnp.mean(zc * zc, axis=-1, keepdims=True)
    return zc * lax.rsqrt(var + LN_EPS) * g + b


def _rows_from_tiles(ref, n_sub, first=0, rows=None, dtype=None):
    if ref.shape[1] == n_sub * LANES:
        return ref[...] if dtype is None else ref[...].astype(dtype)
    rows = ref.shape[0] // n_sub if rows is None else rows
    cols = [ref[pl.ds(first + s, rows, stride=n_sub), :] for s in range(n_sub)]
    if dtype is not None:
        cols = [c.astype(dtype) for c in cols]
    return jnp.concatenate(cols, axis=1)


def _rows_to_tiles(ref, value, first=0):
    n_sub = value.shape[1] // LANES
    for s in range(n_sub):
        ref[pl.ds(first + s, value.shape[0], stride=n_sub), :] = value[:, s * LANES:(s + 1) * LANES]


def _matmul_kernel(x_ref, w_ref, o_ref):
    x = _rows_from_tiles(x_ref, w_ref.shape[0] // LANES, dtype=BF16)
    o_ref[...] = jnp.dot(x, w_ref[...], preferred_element_type=F32).astype(o_ref.dtype)


def _x_block(x, d, m, tm):
    if x.shape == (m, d):
        return pl.BlockSpec((tm, d), lambda i: (i, 0))
    return pl.BlockSpec((tm * (d // LANES), LANES), lambda i: (i, 0))


def _matmul(x, m, w, out_dtype):
    k, n = w.shape
    tm = min(TOKEN_TILE, m)
    return pl.pallas_call(
        _matmul_kernel,
        grid=(m // tm,),
        in_specs=[_x_block(x, k, m, tm),
                  pl.BlockSpec((k, n), lambda i: (0, 0))],
        out_specs=pl.BlockSpec((tm, n), lambda i: (i, 0)),
        out_shape=jax.ShapeDtypeStruct((m, n), out_dtype),
        compiler_params=_params(1),
        name="dense_proj",
    )(x, w)


def _attn_kernel(q_ref, k_ref, v_ref, bself_ref, badj_ref, bfar_ref, o_ref,
                 kmean_ref, vt_ref, pen_ref, m_ref, l_ref, acc_ref, sa_ref, sb_ref, *, nblk, scale):
    blk = MOBA_BLOCK
    hg = bself_ref.shape[0]
    dh = q_ref.shape[1] // hg
    i = pl.program_id(1)

    def head(ref, h):
        return ref.at[:, h * dh:(h + 1) * dh]

    @pl.when(i == 0)
    def _():
        for h in range(hg):
            for j in range(nblk):
                kj = head(k_ref, h)[j * blk:(j + 1) * blk, :].astype(F32)
                kmean_ref[h, j:j + 1, :] = jnp.sum(kj, axis=0, keepdims=True) * (1.0 / blk)
                vt_ref[h, j] = head(v_ref, h)[j * blk:(j + 1) * blk, :].astype(F32).T.astype(BF16)

    qs = []
    for h in range(hg):
        q = head(q_ref, h)[...]

        km = kmean_ref[h]
        km_hi = km.astype(BF16)
        km_lo = (km - km_hi.astype(F32)).astype(BF16)
        gate = (lax.dot_general(km_hi, q, _NT, preferred_element_type=F32)
                + lax.dot_general(km_lo, q, _NT, preferred_element_type=F32))
        jj = lax.broadcasted_iota(I32, gate.shape, 0)
        past = jj < i
        g = jnp.where(past, gate, -jnp.inf)
        sel = jnp.zeros(gate.shape, jnp.bool_)
        for _ in range(min(MOBA_TOPK, nblk)):
            mx = jnp.max(g, axis=0, keepdims=True)
            jm = jnp.min(jnp.where(g == mx, jj, nblk), axis=0, keepdims=True)
            pick = jj == jm
            sel = jnp.logical_or(sel, pick)
            g = jnp.where(pick, -jnp.inf, g)
        pen_ref[h] = jnp.where(jnp.logical_and(sel, past), 0.0, -jnp.inf)
        qs.append((q.astype(F32) * (scale * LOG2_E)).astype(BF16))

    def scores(h, j):
        kb = head(k_ref, h)[pl.ds(pl.multiple_of(j * blk, blk), blk), :]
        return lax.dot_general(kb, qs[h], _NT, preferred_element_type=F32)

    def fold(x, op):
        return op(x.reshape(blk // 8, 8, blk), axis=0)

    def adj_bias(h):
        return badj_ref[h] + pen_ref[h, pl.ds(jnp.maximum(i - 1, 0), 1), :]

    def far_bias(h, j):
        return bfar_ref[h, 0:1, 0:1] + pen_ref[h, pl.ds(j, 1), :]

    n_far = jnp.maximum(i - 1, 0)


    ss = [scores(h, i) + bself_ref[h] for h in range(hg)]
    for h in range(hg):
        m0 = jnp.max(fold(ss[h], jnp.max), axis=0, keepdims=True)
        p = jnp.exp2(ss[h] - m0)
        m_ref[h] = m0
        l_ref[h] = fold(p, jnp.sum)
        acc_ref[h] = jnp.dot(vt_ref[h, i], p.astype(BF16), preferred_element_type=F32)

    def update(h, j, s, s_max, bias):
        if bias is not None:
            s_max = s_max + bias
        m_old = m_ref[h]
        m_new = jnp.maximum(m_old, jnp.max(s_max, axis=0, keepdims=True))
        alpha = jnp.exp2(m_old - m_new)
        p = jnp.exp2(s - m_new if bias is None else s + (bias - m_new))
        m_ref[h] = m_new
        l_ref[h] = alpha * l_ref[h] + fold(p, jnp.sum)
        acc_ref[h] = alpha * acc_ref[h] + jnp.dot(
            vt_ref[h, j], p.astype(BF16), preferred_element_type=F32)

    @pl.when(i >= 1)
    def _():
        ss = [scores(h, i - 1) + adj_bias(h) for h in range(hg)]
        for h in range(hg):
            update(h, i - 1, ss[h], fold(ss[h], jnp.max), None)

    @pl.when(n_far >= 1)
    def _():
        for h in range(hg):
            sa_ref[h] = scores(h, 0)

    def stage(j, cur_ref, nxt_ref):
        nxt = jnp.minimum(j + 1, n_far - 1)
        for h in range(hg):
            nxt_ref[h] = scores(h, nxt)
        for h in range(hg):
            s = cur_ref[h]
            update(h, j, s, fold(s, jnp.max), far_bias(h, j))

    def far_pair(p, carry):
        stage(2 * p, sa_ref, sb_ref)

        @pl.when(2 * p + 1 < n_far)
        def _():
            stage(2 * p + 1, sb_ref, sa_ref)

        return carry

    lax.fori_loop(0, (n_far + 1) // 2, far_pair, 0)

    for h in range(hg):
        o = acc_ref[h] / jnp.sum(l_ref[h], axis=0, keepdims=True)
        head(o_ref, h)[...] = o.T.astype(o_ref.dtype)


def _t5_bucket_np(dist):
    max_exact = NUM_BUCKETS // 2
    n = dist.astype(np.float32)
    large = max_exact + (np.log(np.maximum(n, np.float32(1.0)) / np.float32(max_exact))
                         / np.float32(math.log(REL_MAX_DISTANCE / max_exact))
                         * np.float32(NUM_BUCKETS - max_exact)).astype(np.int32)
    large = np.minimum(large, NUM_BUCKETS - 1)
    return np.where(dist < max_exact, dist, large)


def _t5_bucket(dist):
    max_exact = NUM_BUCKETS // 2
    n = dist.astype(F32)
    large = max_exact + (jnp.log(jnp.maximum(n, 1.0) / max_exact)
                         / math.log(REL_MAX_DISTANCE / max_exact)
                         * (NUM_BUCKETS - max_exact)).astype(I32)
    large = jnp.minimum(large, NUM_BUCKETS - 1)
    return jnp.where(dist < max_exact, dist, large)


def _attn_biases(rel_bias_table, t):
    blk = MOBA_BLOCK
    far_buckets = _t5_bucket_np(np.arange(blk + 1, max(t, 2 * blk) + 1, dtype=np.int32))
    assert (far_buckets == NUM_BUCKETS - 1).all()
    bbd = rel_bias_table.astype(F32)[_t5_bucket(jnp.arange(2 * blk + 1, dtype=I32))].T * LOG2_E
    n_heads = bbd.shape[0]

    def toeplitz(w):
        rep = jnp.tile(w, (1, blk))[:, :blk * (2 * blk - 1)].reshape(n_heads, blk, 2 * blk - 1)
        return rep[:, :, blk - 1:]

    w_self = jnp.concatenate([jnp.full((n_heads, blk - 1), -jnp.inf, F32), bbd[:, :blk + 1]], axis=1)
    w_adj = jnp.concatenate([bbd[:, 1:2 * blk], bbd[:, :1]], axis=1)
    bself = toeplitz(w_self)
    badj = toeplitz(w_adj)
    bfar = jnp.broadcast_to(bbd[:, 2 * blk][:, None, None], (n_heads, 8, 128))
    return bself, badj, bfar


def _moba_attention(qkv, biases):
    t, d3 = qkv.shape
    d = d3 // 3
    dh = d // N_HEADS
    blk = MOBA_BLOCK
    nblk = t // blk
    bself, badj, bfar = biases
    hg = HEAD_GROUP
    n_grp = N_HEADS // hg
    kernel = functools.partial(_attn_kernel, nblk=nblk, scale=dh ** -0.5)
    return pl.pallas_call(
        kernel,
        grid=(n_grp, nblk),
        in_specs=[pl.BlockSpec((blk, hg * dh), lambda g, i: (i, g)),
                  pl.BlockSpec((t, hg * dh), lambda g, i: (0, n_grp + g)),
                  pl.BlockSpec((t, hg * dh), lambda g, i: (0, 2 * n_grp + g)),
                  pl.BlockSpec((hg, blk, blk), lambda g, i: (g, 0, 0)),
                  pl.BlockSpec((hg, blk, blk), lambda g, i: (g, 0, 0)),
                  pl.BlockSpec((hg, 8, LANES), lambda g, i: (g, 0, 0))],
        out_specs=pl.BlockSpec((blk, hg * dh), lambda g, i: (i, g)),
        out_shape=jax.ShapeDtypeStruct((t, d), BF16),
        scratch_shapes=[pltpu.VMEM((hg, nblk, dh), F32),
                        pltpu.VMEM((hg, nblk, dh, blk), BF16),
                        pltpu.VMEM((hg, nblk, blk), F32),
                        pltpu.VMEM((hg, 1, blk), F32),
                        pltpu.VMEM((hg, 8, blk), F32),
                        pltpu.VMEM((hg, dh, blk), F32),
                        pltpu.VMEM((hg, blk, blk), F32),
                        pltpu.VMEM((hg, blk, blk), F32)],
        compiler_params=_params(2),
        name="moba_attention",
    )(qkv, qkv, qkv, bself, badj, bfar)


def _proj_ln_router_kernel(a_ref, w_ref, x_ref, g_ref, b_ref, wrh_ref, wrl_ref, br_ref,
                           xn_ref, ids_ref, gates_ref, rank_ref, counts_ref, base_ref,
                           *, alpha, n_exp):
    t = pl.program_id(0)
    tm = a_ref.shape[0]

    @pl.when(t == 0)
    def _():
        base_ref[...] = jnp.zeros_like(base_ref)

    mix = jnp.dot(a_ref[...].astype(BF16), w_ref[...], preferred_element_type=F32)
    x_res = _rows_from_tiles(x_ref, w_ref.shape[1] // LANES)
    xn = _layer_norm(alpha * x_res + mix, g_ref[...], b_ref[...])
    _rows_to_tiles(xn_ref, xn)

    x_hi = xn.astype(BF16)
    x_lo = (xn - x_hi.astype(F32)).astype(BF16)
    wr_hi = wrh_ref[...]
    logits = (lax.dot_general(wr_hi, x_hi, _NT, preferred_element_type=F32)
              + lax.dot_general(wr_hi, x_lo, _NT, preferred_element_type=F32)
              + lax.dot_general(wrl_ref[...], x_hi, _NT, preferred_element_type=F32)
              + br_ref[...])

    e_iota = lax.broadcasted_iota(I32, logits.shape, 0)
    work = logits
    tops, picks = [], []
    for k in range(EXPERT_TOPK):
        mx = jnp.max(work, axis=0, keepdims=True)
        em = jnp.min(jnp.where(work == mx, e_iota, n_exp), axis=0, keepdims=True)
        pick = e_iota == em
        ids_ref[k:k + 1, :] = em
        tops.append(mx)
        picks.append(pick)
        work = jnp.where(pick, -jnp.inf, work)

    ex = [jnp.exp(tk - tops[0]) for tk in tops]
    den = ex[0]
    for e in ex[1:]:
        den = den + e
    for k in range(EXPERT_TOPK):
        gates_ref[k:k + 1, :] = ex[k] / den

    onehot = jnp.zeros(logits.shape, F32)
    for pick in picks:
        onehot = onehot + pick.astype(F32)
    earlier = (lax.broadcasted_iota(I32, (tm, tm), 0)
               < lax.broadcasted_iota(I32, (tm, tm), 1)).astype(BF16)
    pos = jnp.dot(onehot.astype(BF16), earlier, preferred_element_type=F32) + base_ref[:, 0:1]
    for k in range(EXPERT_TOPK):
        rank_ref[k:k + 1, :] = jnp.sum(jnp.where(picks[k], pos, 0.0), axis=0,
                                       keepdims=True).astype(I32)
    total = base_ref[...] + jnp.sum(onehot, axis=1, keepdims=True)
    base_ref[...] = total
    counts_ref[...] = total


def _proj_ln_router(a, w, x_tiles, ln_g, ln_b, wr_hi, wr_lo, br, alpha):
    t, d = a.shape
    n_exp = wr_hi.shape[0]
    tm = min(TOKEN_TILE, t)
    kernel = functools.partial(_proj_ln_router_kernel, alpha=alpha, n_exp=n_exp)
    row = lambda i: (i, 0)
    fixed = lambda i: (0, 0)
    col = lambda i: (0, i)
    return pl.pallas_call(
        kernel,
        grid=(t // tm,),
        in_specs=[pl.BlockSpec((tm, d), row),
                  pl.BlockSpec((d, d), fixed),
                  _x_block(x_tiles, d, t, tm),
                  pl.BlockSpec((1, d), fixed),
                  pl.BlockSpec((1, d), fixed),
                  pl.BlockSpec((n_exp, d), fixed),
                  pl.BlockSpec((n_exp, d), fixed),
                  pl.BlockSpec((n_exp, 1), fixed)],
        out_specs=[pl.BlockSpec((tm * (d // LANES), LANES), row),
                   pl.BlockSpec((EXPERT_TOPK, tm), col),
                   pl.BlockSpec((EXPERT_TOPK, tm), col),
                   pl.BlockSpec((EXPERT_TOPK, tm), col),
                   pl.BlockSpec((n_exp, 128), fixed)],
        out_shape=[jax.ShapeDtypeStruct((t * (d // LANES), LANES), F32),
                   jax.ShapeDtypeStruct((EXPERT_TOPK, t), I32),
                   jax.ShapeDtypeStruct((EXPERT_TOPK, t), F32),
                   jax.ShapeDtypeStruct((EXPERT_TOPK, t), I32),
                   jax.ShapeDtypeStruct((n_exp, 128), F32)],
        scratch_shapes=[pltpu.VMEM((n_exp, 128), F32)],
        compiler_params=_params(1),
        name="proj_ln_router",
    )(a, w, x_tiles, ln_g, ln_b, wr_hi, wr_lo, br)


def _moe_kernel(dest_ref, gate_ref, start_ref, count_ref, nch_ref, last_ref,
                x2_ref, wg_ref, wu_ref, wd_ref, bg_ref, bu_ref, bd_ref, lng_ref, lnb_ref,
                out_ref, src_ref, xg_ref, yf_ref, *, n_tok, n_exp, tok_bits, ln_alpha):
    e = pl.program_id(0)
    n_sub = wg_ref.shape[0] // LANES
    tm = xg_ref.shape[0] // n_sub
    spare = COMBINE_BATCH

    def tile(ref, idx):
        return ref.at[pl.ds(pl.multiple_of(idx * n_sub, n_sub), n_sub), :]

    tok_mask = (1 << tok_bits) - 1

    def gather_row(base, r):
        tok = jnp.minimum(src_ref[base + r] & tok_mask, n_tok - 1)
        tile(xg_ref, r)[...] = tile(x2_ref, tok)[...]

    def combine_batch(base, r0):
        codes = [src_ref[base + r0 + k] for k in range(spare)]
        toks = [c & tok_mask for c in codes]
        gs = [gate_ref[c] for c in codes]
        olds = [tile(out_ref, toks[k])[...] for k in range(spare)]
        for k in range(spare):
            tile(out_ref, toks[k])[...] = olds[k] + gs[k] * tile(yf_ref, r0 + k)[...]

    @pl.when(e == 0)
    def _():
        out_ref[...] = jnp.zeros_like(out_ref)
        yf_ref[...] = jnp.zeros_like(yf_ref)

        def fill(r, carry):
            src_ref[r] = n_tok + r % spare + (EXPERT_TOPK << tok_bits)
            return carry

        for ee in range(n_exp):
            lax.fori_loop(start_ref[ee] + count_ref[ee], start_ref[ee] + nch_ref[ee] * tm, fill, 0)

        def scatter(a, carry):
            for k in range(EXPERT_TOPK):
                src_ref[dest_ref[k * n_tok + a]] = a + (k << tok_bits)
            return carry

        lax.fori_loop(0, n_tok, scatter, 0, unroll=8)

        def first_rows(r, carry):
            gather_row(0, r)
            return carry

        lax.fori_loop(0, tm, first_rows, 0, unroll=8)

    def visit(base, rows):
        x = jnp.concatenate(
            [xg_ref[pl.ds(s, rows, stride=n_sub), :].astype(BF16) for s in range(n_sub)], axis=1)

        nxt = jnp.minimum(base + tm, last_ref[0])
        for r in range(tm):
            gather_row(nxt, r)
        prev = jnp.maximum(base - tm, 0)
        for r0 in range(0, tm, spare):
            combine_batch(prev, r0)

        g = jnp.dot(x, wg_ref[...], preferred_element_type=F32) + bg_ref[...]
        u = jnp.dot(x, wu_ref[...], preferred_element_type=F32) + bu_ref[...]
        g = jnp.minimum(g, SWIGLU_LIMIT)
        u = jnp.clip(u, -SWIGLU_LIMIT, SWIGLU_LIMIT)
        h = g * _sigmoid(SWIGLU_ALPHA * g) * (u + 1.0)
        y = jnp.dot(h.astype(BF16), wd_ref[...], preferred_element_type=F32) + bd_ref[...]
        for s in range(n_sub):
            yf_ref[pl.ds(s, rows, stride=n_sub), :] = y[:, s * LANES:(s + 1) * LANES]

    def chunk(ch, carry):
        base = start_ref[e] + ch * tm
        n_real = count_ref[e] - ch * tm

        @pl.when(n_real > tm // 2)
        def _():
            visit(base, tm)

        @pl.when(n_real <= tm // 2)
        def _():
            visit(base, tm // 2)

        return carry

    lax.fori_loop(0, nch_ref[e], chunk, 0)

    @pl.when(e == n_exp - 1)
    def _():
        def last_rows(b, carry):
            combine_batch(last_ref[0], b * spare)
            return carry

        lax.fori_loop(0, tm // spare, last_rows, 0)

        ln_rows = min(EXPERT_ROWS, n_tok)

        def normalize(b, carry):
            first = pl.multiple_of(b * (ln_rows * n_sub), ln_rows * n_sub)
            z = (ln_alpha * _rows_from_tiles(x2_ref, n_sub, first, ln_rows)
                 + _rows_from_tiles(out_ref, n_sub, first, ln_rows))
            _rows_to_tiles(out_ref, _layer_norm(z, lng_ref[...], lnb_ref[...]), first)
            return carry

        lax.fori_loop(0, n_tok // ln_rows, normalize, 0)


def _moe_experts(x_tiles, t, d, tok_bits, dest, gates, start, count, nch, last, layer,
                 wg, wu, wd, bg, bu, bd, ln_g, ln_b, alpha):
    n_sub = d // LANES
    tm = EXPERT_ROWS
    n_exp, dff = wg.shape[1], wg.shape[3]
    n_rows = t * EXPERT_TOPK + n_exp * tm

    def w_map(e, *_):
        return (layer, e, 0, 0)

    out_rows = (t + COMBINE_BATCH) * n_sub
    grid_spec = pltpu.PrefetchScalarGridSpec(
        num_scalar_prefetch=6,
        grid=(n_exp,),
        in_specs=[pl.BlockSpec((t * n_sub, LANES), lambda e, *_: (0, 0)),
                  pl.BlockSpec((None, None, d, dff), w_map),
                  pl.BlockSpec((None, None, d, dff), w_map),
                  pl.BlockSpec((None, None, dff, d), w_map),
                  pl.BlockSpec((None, None, 1, dff), w_map),
                  pl.BlockSpec((None, None, 1, dff), w_map),
                  pl.BlockSpec((None, None, 1, d), w_map),
                  pl.BlockSpec((1, d), lambda e, *_: (0, 0)),
                  pl.BlockSpec((1, d), lambda e, *_: (0, 0))],
        out_specs=pl.BlockSpec((out_rows, LANES), lambda e, *_: (0, 0)),
        scratch_shapes=[pltpu.SMEM((n_rows,), I32),
                        pltpu.VMEM((tm * n_sub, LANES), F32),
                        pltpu.VMEM((tm * n_sub, LANES), F32)],
    )
    return pl.pallas_call(
        functools.partial(_moe_kernel, n_tok=t, n_exp=n_exp, tok_bits=tok_bits, ln_alpha=alpha),
        grid_spec=grid_spec,
        out_shape=jax.ShapeDtypeStruct((out_rows, LANES), F32),
        compiler_params=_params(1),
        name="moe_experts",
    )(dest, gates, start, count, nch, last, x_tiles, wg, wu, wd, bg, bu, bd, ln_g, ln_b)


def _lru_in_kernel(x_ref, w_ref, y_ref, r_ref):
    d = y_ref.shape[1]
    x = _rows_from_tiles(x_ref, w_ref.shape[0] // LANES, dtype=BF16)
    yz = jnp.dot(x, w_ref[...], preferred_element_type=F32)
    y = yz[:, :d]
    y_ref[...] = 0.5 * y * (1.0 + jnp.tanh(math.sqrt(2.0 / math.pi) * (y + 0.044715 * (y * y * y))))
    r_ref[...] = yz[:, d:]


def _lru_in(x_tiles, t, w_in):
    d = w_in.shape[0]
    n = w_in.shape[1] // 2
    tm = min(TOKEN_TILE, t)
    return pl.pallas_call(
        _lru_in_kernel,
        grid=(t // tm,),
        in_specs=[_x_block(x_tiles, d, t, tm),
                  pl.BlockSpec((d, 2 * n), lambda i: (0, 0))],
        out_specs=[pl.BlockSpec((tm, n), lambda i: (i, 0)),
                   pl.BlockSpec((tm, n), lambda i: (i, 0))],
        out_shape=[jax.ShapeDtypeStruct((t, n), F32), jax.ShapeDtypeStruct((t, n), F32)],
        compiler_params=_params(1),
        name="lru_in",
    )(x_tiles, w_in)


def _lru_core_kernel(r_ref, y_ref, cw_ref, cb_ref, wgx_ref, wga_ref, bgx_ref, bga_ref, ap_ref,
                     hy_ref, rprev_ref, carry_ref, a_ref, u_ref):
    t = pl.program_id(0)
    tt, d = r_ref.shape
    n_blk, bw, _ = wgx_ref.shape
    cwid = cw_ref.shape[0]

    @pl.when(t == 0)
    def _():
        rprev_ref[...] = jnp.zeros_like(rprev_ref)
        carry_ref[...] = jnp.zeros_like(carry_ref)

    r = r_ref[...]
    rext = jnp.concatenate([rprev_ref[...], r], axis=0)
    rc = jnp.broadcast_to(cb_ref[...], (tt, d))
    for w in range(cwid):
        off = 8 - (cwid - 1) + w
        rc = rc + cw_ref[w:w + 1, :] * rext[off:off + tt, :]
    rprev_ref[...] = r[tt - 8:, :]

    rcb = rc.astype(BF16)
    gx = jnp.concatenate(
        [jnp.dot(rcb[:, n * bw:(n + 1) * bw], wgx_ref[n], preferred_element_type=F32)
         for n in range(n_blk)], axis=1)
    ga = jnp.concatenate(
        [jnp.dot(rcb[:, n * bw:(n + 1) * bw], wga_ref[n], preferred_element_type=F32)
         for n in range(n_blk)], axis=1)
    gate_x = _sigmoid(gx + bgx_ref[...])
    gate_a = _sigmoid(ga + bga_ref[...])
    z = -ap_ref[...]
    softplus = jnp.maximum(z, 0.0) + jnp.log(1.0 + jnp.exp(-jnp.abs(z)))
    log_a = -LRU_C * gate_a * softplus
    a_ref[...] = jnp.exp(log_a)
    u_ref[...] = jnp.sqrt(1.0 - jnp.exp(2.0 * log_a)) * (gate_x * rc)

    row = lax.broadcasted_iota(I32, (SCAN_ROWS, d), 0)

    def slab(s, carry):
        off = pl.multiple_of(s * SCAN_ROWS, SCAN_ROWS)
        a = a_ref[pl.ds(off, SCAN_ROWS), :]
        b = u_ref[pl.ds(off, SCAN_ROWS), :]
        step = 1
        while step < SCAN_ROWS:
            a_sh = pltpu.roll(a, step, 0)
            b_sh = pltpu.roll(b, step, 0)
            live = row >= step
            b = jnp.where(live, a * b_sh + b, b)
            a = jnp.where(live, a * a_sh, a)
            step *= 2
        h = b + a * carry
        u_ref[pl.ds(off, SCAN_ROWS), :] = h
        return h[SCAN_ROWS - 1:SCAN_ROWS, :]

    carry_ref[...] = lax.fori_loop(0, tt // SCAN_ROWS, slab, carry_ref[...])
    hy_ref[...] = (u_ref[...] * y_ref[...]).astype(hy_ref.dtype)


def _lru_core(r, y, conv_w, conv_b, wgx, wga, bgx, bga, a_param):
    t, d = r.shape
    tt = min(TOKEN_TILE, t)
    row = lambda i: (i, 0)
    fixed2 = lambda i: (0, 0)
    fixed3 = lambda i: (0, 0, 0)
    return pl.pallas_call(
        _lru_core_kernel,
        grid=(t // tt,),
        in_specs=[pl.BlockSpec((tt, d), row),
                  pl.BlockSpec((tt, d), row),
                  pl.BlockSpec(conv_w.shape, fixed2),
                  pl.BlockSpec((1, d), fixed2),
                  pl.BlockSpec(wgx.shape, fixed3),
                  pl.BlockSpec(wga.shape, fixed3),
                  pl.BlockSpec((1, d), fixed2),
                  pl.BlockSpec((1, d), fixed2),
                  pl.BlockSpec((1, d), fixed2)],
        out_specs=pl.BlockSpec((tt, d), row),
        out_shape=jax.ShapeDtypeStruct((t, d), BF16),
        scratch_shapes=[pltpu.VMEM((8, d), F32),
                        pltpu.VMEM((1, d), F32),
                        pltpu.VMEM((tt, d), F32),
                        pltpu.VMEM((tt, d), F32)],
        compiler_params=_params(1),
        name="lru_core",
    )(r, y, conv_w, conv_b, wgx, wga, bgx, bga, a_param)


def _moe(xn_tiles, t, d, ids, gates, rank, counts128, layer, wg, wu, wd, bg, bu, bd,
         ln_g, ln_b, alpha):
    tm = EXPERT_ROWS
    n_exp = wg.shape[1]
    counts = counts128[:, 0].astype(I32)
    nch = (counts + tm - 1) // tm
    start = (jnp.cumsum(nch) - nch).astype(I32) * tm
    experts = jnp.arange(n_exp, dtype=I32)
    dest = rank + jnp.sum(jnp.where(ids[:, :, None] == experts, start, 0), axis=-1)
    last = ((jnp.sum(nch) - 1) * tm).astype(I32).reshape(1)
    tok_bits = (t + COMBINE_BATCH - 1).bit_length()
    gates_ext = jnp.pad(gates, ((0, 1), (0, (1 << tok_bits) - t))).reshape(-1)
    return _moe_experts(xn_tiles, t, d, tok_bits, dest.reshape(-1).astype(I32), gates_ext,
                        start, counts, nch.astype(I32), last, layer, wg, wu, wd, bg, bu, bd,
                        ln_g, ln_b, alpha)


def kernel(x, rel_bias_table, w_qkv, w_attn_out, w_lru_in, lru_conv_w, lru_conv_b, w_lru_gate_x, b_lru_gate_x, w_lru_gate_a, b_lru_gate_a, lru_a_param, w_lru_out, w_router, b_router, w_exp_gate, b_exp_gate, w_exp_up, b_exp_up, w_exp_down, b_exp_down, ln_mix_g, ln_mix_b, ln_ffn_g, ln_ffn_b):
    bsz, t, d = x.shape
    depth = w_router.shape[0]
    n_mixers = 2
    alpha = (2 * depth) ** 0.25
    assert t % MOBA_BLOCK == 0 and t % TOKEN_TILE == 0

    biases = _attn_biases(rel_bias_table, t)
    w_qkv_b = w_qkv.astype(BF16)
    w_attn_out_b = w_attn_out.astype(BF16)
    w_lru_in_b = w_lru_in.astype(BF16)
    w_lru_out_b = w_lru_out.astype(BF16)
    wgx_b = w_lru_gate_x.astype(BF16)
    wga_b = w_lru_gate_a.astype(BF16)
    wg_b = w_exp_gate.astype(BF16)
    wu_b = w_exp_up.astype(BF16)
    wd_b = w_exp_down.astype(BF16)
    wr_t = jnp.swapaxes(w_router, 1, 2)
    wr_hi = wr_t.astype(BF16)
    wr_lo = (wr_t - wr_hi.astype(F32)).astype(BF16)

    bg4 = b_exp_gate[:, :, None, :]
    bu4 = b_exp_up[:, :, None, :]
    bd4 = b_exp_down[:, :, None, :]

    outs = []
    for b in range(bsz):
        xs = x[b]
        for i in range(depth):
            j = i // n_mixers
            if i % n_mixers == 0:
                qkv = _matmul(xs, t, w_qkv_b[j], BF16)
                a = _moba_attention(qkv, biases)
                w_out = w_attn_out_b[j]
            else:
                y, r = _lru_in(xs, t, w_lru_in_b[j])
                a = _lru_core(r, y, lru_conv_w[j], lru_conv_b[j][None], wgx_b[j], wga_b[j],
                              b_lru_gate_x[j][None], b_lru_gate_a[j][None], lru_a_param[j][None])
                w_out = w_lru_out_b[j]
            xn, ids, gates, rank, counts = _proj_ln_router(
                a, w_out, xs, ln_mix_g[i][None], ln_mix_b[i][None],
                wr_hi[i], wr_lo[i], b_router[i][:, None], alpha)
            xs = _moe(xn, t, d, ids, gates, rank, counts, i, wg_b, wu_b, wd_b, bg4, bu4, bd4,
                      ln_ffn_g[i][None], ln_ffn_b[i][None], alpha)
        outs.append(xs.reshape(t + COMBINE_BATCH, d)[:t])
    return jnp.stack(outs, axis=0)
```

```python
import functools
import math

import numpy as np
import jax
import jax.numpy as jnp
from jax import lax
from jax.experimental import pallas as pl
from jax.experimental.pallas import tpu as pltpu

F32 = jnp.float32
BF16 = jnp.bfloat16
I32 = jnp.int32

N_HEADS = 8
MOBA_BLOCK = 256
MOBA_TOPK = 3
NUM_BUCKETS = 32
REL_MAX_DISTANCE = 128
LRU_C = 8.0
EXPERT_TOPK = 4
SWIGLU_LIMIT = 7.0
SWIGLU_ALPHA = 1.702
LN_EPS = 1e-5
LOG2_E = 1.4426950408889634

EXPERT_ROWS = 256
TOKEN_TILE = 512
SCAN_ROWS = 8
LANES = 128
HEAD_GROUP = 8
COMBINE_BATCH = 8
V7X_VMEM_LIMIT = 56 * 1024 * 1024

_NT = (((1,), (1,)), ((), ()))


def _params(n_axes, vmem=V7X_VMEM_LIMIT):
    return pltpu.CompilerParams(
        dimension_semantics=("arbitrary",) * n_axes, vmem_limit_bytes=vmem)


def _sigmoid(x):
    return 0.5 * jnp.tanh(0.5 * x) + 0.5


def _layer_norm(z, g, b):
    mu = jnp.mean(z, axis=-1, keepdims=True)
    zc = z - mu
    var = jnp.mean(zc * zc, axis=-1, keepdims=True)
    return zc * lax.rsqrt(var + LN_EPS) * g + b


def _rows_from_tiles(ref, n_sub, first=0, rows=None, dtype=None):
    if ref.shape[1] == n_sub * LANES:
        return ref[...] if dtype is None else ref[...].astype(dtype)
    rows = ref.shape[0] // n_sub if rows is None else rows
    cols = [ref[pl.ds(first + s, rows, stride=n_sub), :] for s in range(n_sub)]
    if dtype is not None:
        cols = [c.astype(dtype) for c in cols]
    return jnp.concatenate(cols, axis=1)


def _rows_to_tiles(ref, value, first=0):
    n_sub = value.shape[1] // LANES
    for s in range(n_sub):
        ref[pl.ds(first + s, value.shape[0], stride=n_sub), :] = value[:, s * LANES:(s + 1) * LANES]


def _matmul_kernel(x_ref, w_ref, o_ref):
    x = _rows_from_tiles(x_ref, w_ref.shape[0] // LANES, dtype=BF16)
    o_ref[...] = jnp.dot(x, w_ref[...], preferred_element_type=F32).astype(o_ref.dtype)


def _x_block(x, d, m, tm):
    if x.shape == (m, d):
        return pl.BlockSpec((tm, d), lambda i: (i, 0))
    return pl.BlockSpec((tm * (d // LANES), LANES), lambda i: (i, 0))


def _matmul(x, m, w, out_dtype):
    k, n = w.shape
    tm = min(TOKEN_TILE, m)
    return pl.pallas_call(
        _matmul_kernel,
        grid=(m // tm,),
        in_specs=[_x_block(x, k, m, tm),
                  pl.BlockSpec((k, n), lambda i: (0, 0))],
        out_specs=pl.BlockSpec((tm, n), lambda i: (i, 0)),
        out_shape=jax.ShapeDtypeStruct((m, n), out_dtype),
        compiler_params=_params(1),
        name="dense_proj",
    )(x, w)


def _attn_kernel(q_ref, k_ref, v_ref, bself_ref, badj_ref, bfar_ref, o_ref,
                 kmean_ref, vt_ref, pen_ref, m_ref, l_ref, acc_ref, sa_ref, sb_ref, *, nblk, scale):
    blk = MOBA_BLOCK
    hg = bself_ref.shape[0]
    dh = q_ref.shape[1] // hg
    i = pl.program_id(1)

    def head(ref, h):
        return ref.at[:, h * dh:(h + 1) * dh]

    @pl.when(i == 0)
    def _():
        for h in range(hg):
            for j in range(nblk):
                kj = head(k_ref, h)[j * blk:(j + 1) * blk, :].astype(F32)
                kmean_ref[h, j:j + 1, :] = jnp.sum(kj, axis=0, keepdims=True) * (1.0 / blk)
                vt_ref[h, j] = head(v_ref, h)[j * blk:(j + 1) * blk, :].astype(F32).T.astype(BF16)

    qs = []
    for h in range(hg):
        q = head(q_ref, h)[...]

        km = kmean_ref[h]
        km_hi = km.astype(BF16)
        km_lo = (km - km_hi.astype(F32)).astype(BF16)
        gate = (lax.dot_general(km_hi, q, _NT, preferred_element_type=F32)
                + lax.dot_general(km_lo, q, _NT, preferred_element_type=F32))
        jj = lax.broadcasted_iota(I32, gate.shape, 0)
        past = jj < i
        g = jnp.where(past, gate, -jnp.inf)
        sel = jnp.zeros(gate.shape, jnp.bool_)
        for _ in range(min(MOBA_TOPK, nblk)):
            mx = jnp.max(g, axis=0, keepdims=True)
            jm = jnp.min(jnp.where(g == mx, jj, nblk), axis=0, keepdims=True)
            pick = jj == jm
            sel = jnp.logical_or(sel, pick)
            g = jnp.where(pick, -jnp.inf, g)
        pen_ref[h] = jnp.where(jnp.logical_and(sel, past), 0.0, -jnp.inf)
        qs.append((q.astype(F32) * (scale * LOG2_E)).astype(BF16))

    def scores(h, j):
        kb = head(k_ref, h)[pl.ds(pl.multiple_of(j * blk, blk), blk), :]
        return lax.dot_general(kb, qs[h], _NT, preferred_element_type=F32)

    def fold(x, op):
        return op(x.reshape(blk // 8, 8, blk), axis=0)

    def adj_bias(h):
        return badj_ref[h] + pen_ref[h, pl.ds(jnp.maximum(i - 1, 0), 1), :]

    def far_bias(h, j):
        return bfar_ref[h, 0:1, 0:1] + pen_ref[h, pl.ds(j, 1), :]

    n_far = jnp.maximum(i - 1, 0)


    prev_blk = jnp.maximum(i - 1, 0)
    ss = [scores(h, i) + bself_ref[h] for h in range(hg)]
    for h in range(hg):
        sb_ref[h] = scores(h, prev_blk)
    for h in range(hg):
        sa_ref[h] = scores(h, 0)

    for h in range(hg):
        m0 = jnp.max(fold(ss[h], jnp.max), axis=0, keepdims=True)
        p = jnp.exp2(ss[h] - m0)
        m_ref[h] = m0
        l_ref[h] = fold(p, jnp.sum)
        acc_ref[h] = jnp.dot(vt_ref[h, i], p.astype(BF16), preferred_element_type=F32)

    def update(h, j, s, s_max, bias):
        if bias is not None:
            s_max = s_max + bias
        m_old = m_ref[h]
        m_new = jnp.maximum(m_old, jnp.max(s_max, axis=0, keepdims=True))
        alpha = jnp.exp2(m_old - m_new)
        p = jnp.exp2(s - m_new if bias is None else s + (bias - m_new))
        m_ref[h] = m_new
        l_ref[h] = alpha * l_ref[h] + fold(p, jnp.sum)
        acc_ref[h] = alpha * acc_ref[h] + jnp.dot(
            vt_ref[h, j], p.astype(BF16), preferred_element_type=F32)

    for h in range(hg):
        s = sb_ref[h] + adj_bias(h)
        update(h, prev_blk, s, fold(s, jnp.max), None)

    def stage(j, cur_ref, nxt_ref):
        nxt = jnp.minimum(j + 1, n_far - 1)
        for h in range(hg):
            nxt_ref[h] = scores(h, nxt)
        for h in range(hg):
            s = cur_ref[h]
            update(h, j, s, fold(s, jnp.max), far_bias(h, j))

    def far_pair(p, carry):
        stage(2 * p, sa_ref, sb_ref)

        @pl.when(2 * p + 1 < n_far)
        def _():
            stage(2 * p + 1, sb_ref, sa_ref)

        return carry

    lax.fori_loop(0, (n_far + 1) // 2, far_pair, 0)

    for h in range(hg):
        o = acc_ref[h] / jnp.sum(l_ref[h], axis=0, keepdims=True)
        head(o_ref, h)[...] = o.T.astype(o_ref.dtype)


def _t5_bucket_np(dist):
    max_exact = NUM_BUCKETS // 2
    n = dist.astype(np.float32)
    large = max_exact + (np.log(np.maximum(n, np.float32(1.0)) / np.float32(max_exact))
                         / np.float32(math.log(REL_MAX_DISTANCE / max_exact))
                         * np.float32(NUM_BUCKETS - max_exact)).astype(np.int32)
    large = np.minimum(large, NUM_BUCKETS - 1)
    return np.where(dist < max_exact, dist, large)


def _t5_bucket(dist):
    max_exact = NUM_BUCKETS // 2
    n = dist.astype(F32)
    large = max_exact + (jnp.log(jnp.maximum(n, 1.0) / max_exact)
                         / math.log(REL_MAX_DISTANCE / max_exact)
                         * (NUM_BUCKETS - max_exact)).astype(I32)
    large = jnp.minimum(large, NUM_BUCKETS - 1)
    return jnp.where(dist < max_exact, dist, large)


def _attn_biases(rel_bias_table, t):
    blk = MOBA_BLOCK
    far_buckets = _t5_bucket_np(np.arange(blk + 1, max(t, 2 * blk) + 1, dtype=np.int32))
    assert (far_buckets == NUM_BUCKETS - 1).all()
    bbd = rel_bias_table.astype(F32)[_t5_bucket(jnp.arange(2 * blk + 1, dtype=I32))].T * LOG2_E
    n_heads = bbd.shape[0]

    def toeplitz(w):
        rep = jnp.tile(w, (1, blk))[:, :blk * (2 * blk - 1)].reshape(n_heads, blk, 2 * blk - 1)
        return rep[:, :, blk - 1:]

    w_self = jnp.concatenate([jnp.full((n_heads, blk - 1), -jnp.inf, F32), bbd[:, :blk + 1]], axis=1)
    w_adj = jnp.concatenate([bbd[:, 1:2 * blk], bbd[:, :1]], axis=1)
    bself = toeplitz(w_self)
    badj = toeplitz(w_adj)
    bfar = jnp.broadcast_to(bbd[:, 2 * blk][:, None, None], (n_heads, 8, 128))
    return bself, badj, bfar


def _moba_attention(qkv, biases):
    t, d3 = qkv.shape
    d = d3 // 3
    dh = d // N_HEADS
    blk = MOBA_BLOCK
    nblk = t // blk
    bself, badj, bfar = biases
    hg = HEAD_GROUP
    n_grp = N_HEADS // hg
    kernel = functools.partial(_attn_kernel, nblk=nblk, scale=dh ** -0.5)
    return pl.pallas_call(
        kernel,
        grid=(n_grp, nblk),
        in_specs=[pl.BlockSpec((blk, hg * dh), lambda g, i: (i, g)),
                  pl.BlockSpec((t, hg * dh), lambda g, i: (0, n_grp + g)),
                  pl.BlockSpec((t, hg * dh), lambda g, i: (0, 2 * n_grp + g)),
                  pl.BlockSpec((hg, blk, blk), lambda g, i: (g, 0, 0)),
                  pl.BlockSpec((hg, blk, blk), lambda g, i: (g, 0, 0)),
                  pl.BlockSpec((hg, 8, LANES), lambda g, i: (g, 0, 0))],
        out_specs=pl.BlockSpec((blk, hg * dh), lambda g, i: (i, g)),
        out_shape=jax.ShapeDtypeStruct((t, d), BF16),
        scratch_shapes=[pltpu.VMEM((hg, nblk, dh), F32),
                        pltpu.VMEM((hg, nblk, dh, blk), BF16),
                        pltpu.VMEM((hg, nblk, blk), F32),
                        pltpu.VMEM((hg, 1, blk), F32),
                        pltpu.VMEM((hg, 8, blk), F32),
                        pltpu.VMEM((hg, dh, blk), F32),
                        pltpu.VMEM((hg, blk, blk), F32),
                        pltpu.VMEM((hg, blk, blk), F32)],
        compiler_params=_params(2),
        name="moba_attention",
    )(qkv, qkv, qkv, bself, badj, bfar)


def _proj_ln_router_kernel(a_ref, w_ref, x_ref, g_ref, b_ref, wrh_ref, wrl_ref, br_ref,
                           xn_ref, ids_ref, gates_ref, rank_ref, counts_ref, base_ref,
                           *, alpha, n_exp):
    t = pl.program_id(0)
    tm = a_ref.shape[0]

    @pl.when(t == 0)
    def _():
        base_ref[...] = jnp.zeros_like(base_ref)

    mix = jnp.dot(a_ref[...].astype(BF16), w_ref[...], preferred_element_type=F32)
    x_res = _rows_from_tiles(x_ref, w_ref.shape[1] // LANES)
    xn = _layer_norm(alpha * x_res + mix, g_ref[...], b_ref[...])
    _rows_to_tiles(xn_ref, xn)

    x_hi = xn.astype(BF16)
    x_lo = (xn - x_hi.astype(F32)).astype(BF16)
    wr_hi = wrh_ref[...]
    logits = (lax.dot_general(wr_hi, x_hi, _NT, preferred_element_type=F32)
              + lax.dot_general(wr_hi, x_lo, _NT, preferred_element_type=F32)
              + lax.dot_general(wrl_ref[...], x_hi, _NT, preferred_element_type=F32)
              + br_ref[...])

    e_iota = lax.broadcasted_iota(I32, logits.shape, 0)
    work = logits
    tops, picks = [], []
    for k in range(EXPERT_TOPK):
        mx = jnp.max(work, axis=0, keepdims=True)
        em = jnp.min(jnp.where(work == mx, e_iota, n_exp), axis=0, keepdims=True)
        pick = e_iota == em
        ids_ref[k:k + 1, :] = em
        tops.append(mx)
        picks.append(pick)
        work = jnp.where(pick, -jnp.inf, work)

    ex = [jnp.exp(tk - tops[0]) for tk in tops]
    den = ex[0]
    for e in ex[1:]:
        den = den + e
    for k in range(EXPERT_TOPK):
        gates_ref[k:k + 1, :] = ex[k] / den

    onehot = jnp.zeros(logits.shape, F32)
    for pick in picks:
        onehot = onehot + pick.astype(F32)
    earlier = (lax.broadcasted_iota(I32, (tm, tm), 0)
               < lax.broadcasted_iota(I32, (tm, tm), 1)).astype(BF16)
    pos = jnp.dot(onehot.astype(BF16), earlier, preferred_element_type=F32) + base_ref[:, 0:1]
    for k in range(EXPERT_TOPK):
        rank_ref[k:k + 1, :] = jnp.sum(jnp.where(picks[k], pos, 0.0), axis=0,
                                       keepdims=True).astype(I32)
    total = base_ref[...] + jnp.sum(onehot, axis=1, keepdims=True)
    base_ref[...] = total
    counts_ref[...] = total


def _proj_ln_router(a, w, x_tiles, ln_g, ln_b, wr_hi, wr_lo, br, alpha):
    t, d = a.shape
    n_exp = wr_hi.shape[0]
    tm = min(TOKEN_TILE, t)
    kernel = functools.partial(_proj_ln_router_kernel, alpha=alpha, n_exp=n_exp)
    row = lambda i: (i, 0)
    fixed = lambda i: (0, 0)
    col = lambda i: (0, i)
    return pl.pallas_call(
        kernel,
        grid=(t // tm,),
        in_specs=[pl.BlockSpec((tm, d), row),
                  pl.BlockSpec((d, d), fixed),
                  _x_block(x_tiles, d, t, tm),
                  pl.BlockSpec((1, d), fixed),
                  pl.BlockSpec((1, d), fixed),
                  pl.BlockSpec((n_exp, d), fixed),
                  pl.BlockSpec((n_exp, d), fixed),
                  pl.BlockSpec((n_exp, 1), fixed)],
        out_specs=[pl.BlockSpec((tm * (d // LANES), LANES), row),
                   pl.BlockSpec((EXPERT_TOPK, tm), col),
                   pl.BlockSpec((EXPERT_TOPK, tm), col),
                   pl.BlockSpec((EXPERT_TOPK, tm), col),
                   pl.BlockSpec((n_exp, 128), fixed)],
        out_shape=[jax.ShapeDtypeStruct((t * (d // LANES), LANES), F32),
                   jax.ShapeDtypeStruct((EXPERT_TOPK, t), I32),
                   jax.ShapeDtypeStruct((EXPERT_TOPK, t), F32),
                   jax.ShapeDtypeStruct((EXPERT_TOPK, t), I32),
                   jax.ShapeDtypeStruct((n_exp, 128), F32)],
        scratch_shapes=[pltpu.VMEM((n_exp, 128), F32)],
        compiler_params=_params(1),
        name="proj_ln_router",
    )(a, w, x_tiles, ln_g, ln_b, wr_hi, wr_lo, br)


def _moe_kernel(dest_ref, gate_ref, start_ref, count_ref, nch_ref, last_ref,
                x2_ref, wg_ref, wu_ref, wd_ref, bg_ref, bu_ref, bd_ref, lng_ref, lnb_ref,
                out_ref, src_ref, xg_ref, yf_ref, *, n_tok, n_exp, tok_bits, ln_alpha):
    e = pl.program_id(0)
    n_sub = wg_ref.shape[0] // LANES
    tm = xg_ref.shape[0] // n_sub
    spare = COMBINE_BATCH

    def tile(ref, idx):
        return ref.at[pl.ds(pl.multiple_of(idx * n_sub, n_sub), n_sub), :]

    tok_mask = (1 << tok_bits) - 1

    def gather_row(base, r):
        tok = jnp.minimum(src_ref[base + r] & tok_mask, n_tok - 1)
        tile(xg_ref, r)[...] = tile(x2_ref, tok)[...]

    def combine_batch(base, r0):
        codes = [src_ref[base + r0 + k] for k in range(spare)]
        toks = [c & tok_mask for c in codes]
        gs = [gate_ref[c] for c in codes]
        olds = [tile(out_ref, toks[k])[...] for k in range(spare)]
        for k in range(spare):
            tile(out_ref, toks[k])[...] = olds[k] + gs[k] * tile(yf_ref, r0 + k)[...]

    @pl.when(e == 0)
    def _():
        out_ref[...] = jnp.zeros_like(out_ref)
        yf_ref[...] = jnp.zeros_like(yf_ref)

        def fill(r, carry):
            src_ref[r] = n_tok + r % spare + (EXPERT_TOPK << tok_bits)
            return carry

        for ee in range(n_exp):
            lax.fori_loop(start_ref[ee] + count_ref[ee], start_ref[ee] + nch_ref[ee] * tm, fill, 0)

        def scatter(a, carry):
            for k in range(EXPERT_TOPK):
                src_ref[dest_ref[k * n_tok + a]] = a + (k << tok_bits)
            return carry

        lax.fori_loop(0, n_tok, scatter, 0, unroll=8)

        def first_rows(r, carry):
            gather_row(0, r)
            return carry

        lax.fori_loop(0, tm, first_rows, 0, unroll=8)

    def visit(base, rows):
        x = jnp.concatenate(
            [xg_ref[pl.ds(s, rows, stride=n_sub), :].astype(BF16) for s in range(n_sub)], axis=1)

        nxt = jnp.minimum(base + tm, last_ref[0])
        for r in range(tm):
            gather_row(nxt, r)
        prev = jnp.maximum(base - tm, 0)
        for r0 in range(0, tm, spare):
            combine_batch(prev, r0)

        g = jnp.dot(x, wg_ref[...], preferred_element_type=F32) + bg_ref[...]
        u = jnp.dot(x, wu_ref[...], preferred_element_type=F32) + bu_ref[...]
        g = jnp.minimum(g, SWIGLU_LIMIT)
        u = jnp.clip(u, -SWIGLU_LIMIT, SWIGLU_LIMIT)
        h = g * _sigmoid(SWIGLU_ALPHA * g) * (u + 1.0)
        y = jnp.dot(h.astype(BF16), wd_ref[...], preferred_element_type=F32) + bd_ref[...]
        for s in range(n_sub):
            yf_ref[pl.ds(s, rows, stride=n_sub), :] = y[:, s * LANES:(s + 1) * LANES]

    def chunk(ch, carry):
        base = start_ref[e] + ch * tm
        n_real = count_ref[e] - ch * tm

        @pl.when(n_real > tm // 2)
        def _():
            visit(base, tm)

        @pl.when(n_real <= tm // 2)
        def _():
            visit(base, tm // 2)

        return carry

    lax.fori_loop(0, nch_ref[e], chunk, 0)

    @pl.when(e == n_exp - 1)
    def _():
        def last_rows(b, carry):
            combine_batch(last_ref[0], b * spare)
            return carry

        lax.fori_loop(0, tm // spare, last_rows, 0)

        ln_rows = min(EXPERT_ROWS, n_tok)

        def normalize(b, carry):
            first = pl.multiple_of(b * (ln_rows * n_sub), ln_rows * n_sub)
            z = (ln_alpha * _rows_from_tiles(x2_ref, n_sub, first, ln_rows)
                 + _rows_from_tiles(out_ref, n_sub, first, ln_rows))
            _rows_to_tiles(out_ref, _layer_norm(z, lng_ref[...], lnb_ref[...]), first)
            return carry

        lax.fori_loop(0, n_tok // ln_rows, normalize, 0)


def _moe_experts(x_tiles, t, d, tok_bits, dest, gates, start, count, nch, last, layer,
                 wg, wu, wd, bg, bu, bd, ln_g, ln_b, alpha):
    n_sub = d // LANES
    tm = EXPERT_ROWS
    n_exp, dff = wg.shape[1], wg.shape[3]
    n_rows = t * EXPERT_TOPK + n_exp * tm

    def w_map(e, *_):
        return (layer, e, 0, 0)

    out_rows = (t + COMBINE_BATCH) * n_sub
    grid_spec = pltpu.PrefetchScalarGridSpec(
        num_scalar_prefetch=6,
        grid=(n_exp,),
        in_specs=[pl.BlockSpec((t * n_sub, LANES), lambda e, *_: (0, 0)),
                  pl.BlockSpec((None, None, d, dff), w_map),
                  pl.BlockSpec((None, None, d, dff), w_map),
                  pl.BlockSpec((None, None, dff, d), w_map),
                  pl.BlockSpec((None, None, 1, dff), w_map),
                  pl.BlockSpec((None, None, 1, dff), w_map),
                  pl.BlockSpec((None, None, 1, d), w_map),
                  pl.BlockSpec((1, d), lambda e, *_: (0, 0)),
                  pl.BlockSpec((1, d), lambda e, *_: (0, 0))],
        out_specs=pl.BlockSpec((out_rows, LANES), lambda e, *_: (0, 0)),
        scratch_shapes=[pltpu.SMEM((n_rows,), I32),
                        pltpu.VMEM((tm * n_sub, LANES), F32),
                        pltpu.VMEM((tm * n_sub, LANES), F32)],
    )
    return pl.pallas_call(
        functools.partial(_moe_kernel, n_tok=t, n_exp=n_exp, tok_bits=tok_bits, ln_alpha=alpha),
        grid_spec=grid_spec,
        out_shape=jax.ShapeDtypeStruct((out_rows, LANES), F32),
        compiler_params=_params(1),
        name="moe_experts",
    )(dest, gates, start, count, nch, last, x_tiles, wg, wu, wd, bg, bu, bd, ln_g, ln_b)


def _lru_in_kernel(x_ref, w_ref, y_ref, r_ref):
    d = y_ref.shape[1]
    x = _rows_from_tiles(x_ref, w_ref.shape[0] // LANES, dtype=BF16)
    yz = jnp.dot(x, w_ref[...], preferred_element_type=F32)
    y = yz[:, :d]
    y_ref[...] = 0.5 * y * (1.0 + jnp.tanh(math.sqrt(2.0 / math.pi) * (y + 0.044715 * (y * y * y))))
    r_ref[...] = yz[:, d:]


def _lru_in(x_tiles, t, w_in):
    d = w_in.shape[0]
    n = w_in.shape[1] // 2
    tm = min(TOKEN_TILE, t)
    return pl.pallas_call(
        _lru_in_kernel,
        grid=(t // tm,),
        in_specs=[_x_block(x_tiles, d, t, tm),
                  pl.BlockSpec((d, 2 * n), lambda i: (0, 0))],
        out_specs=[pl.BlockSpec((tm, n), lambda i: (i, 0)),
                   pl.BlockSpec((tm, n), lambda i: (i, 0))],
        out_shape=[jax.ShapeDtypeStruct((t, n), F32), jax.ShapeDtypeStruct((t, n), F32)],
        compiler_params=_params(1),
        name="lru_in",
    )(x_tiles, w_in)


def _lru_core_kernel(r_ref, y_ref, cw_ref, cb_ref, wgx_ref, wga_ref, bgx_ref, bga_ref, ap_ref,
                     hy_ref, rprev_ref, carry_ref, a_ref, u_ref):
    t = pl.program_id(0)
    tt, d = r_ref.shape
    n_blk, bw, _ = wgx_ref.shape
    cwid = cw_ref.shape[0]

    @pl.when(t == 0)
    def _():
        rprev_ref[...] = jnp.zeros_like(rprev_ref)
        carry_ref[...] = jnp.zeros_like(carry_ref)

    r = r_ref[...]
    rext = jnp.concatenate([rprev_ref[...], r], axis=0)
    rc = jnp.broadcast_to(cb_ref[...], (tt, d))
    for w in range(cwid):
        off = 8 - (cwid - 1) + w
        rc = rc + cw_ref[w:w + 1, :] * rext[off:off + tt, :]
    rprev_ref[...] = r[tt - 8:, :]

    rcb = rc.astype(BF16)
    gx = jnp.concatenate(
        [jnp.dot(rcb[:, n * bw:(n + 1) * bw], wgx_ref[n], preferred_element_type=F32)
         for n in range(n_blk)], axis=1)
    ga = jnp.concatenate(
        [jnp.dot(rcb[:, n * bw:(n + 1) * bw], wga_ref[n], preferred_element_type=F32)
         for n in range(n_blk)], axis=1)
    gate_x = _sigmoid(gx + bgx_ref[...])
    gate_a = _sigmoid(ga + bga_ref[...])
    z = -ap_ref[...]
    softplus = jnp.maximum(z, 0.0) + jnp.log(1.0 + jnp.exp(-jnp.abs(z)))
    log_a = -LRU_C * gate_a * softplus
    a_ref[...] = jnp.exp(log_a)
    u_ref[...] = jnp.sqrt(1.0 - jnp.exp(2.0 * log_a)) * (gate_x * rc)

    row = lax.broadcasted_iota(I32, (SCAN_ROWS, d), 0)

    def slab(s, carry):
        off = pl.multiple_of(s * SCAN_ROWS, SCAN_ROWS)
        a = a_ref[pl.ds(off, SCAN_ROWS), :]
        b = u_ref[pl.ds(off, SCAN_ROWS), :]
        step = 1
        while step < SCAN_ROWS:
            a_sh = pltpu.roll(a, step, 0)
            b_sh = pltpu.roll(b, step, 0)
            live = row >= step
            b = jnp.where(live, a * b_sh + b, b)
            a = jnp.where(live, a * a_sh, a)
            step *= 2
        h = b + a * carry
        u_ref[pl.ds(off, SCAN_ROWS), :] = h
        return h[SCAN_ROWS - 1:SCAN_ROWS, :]

    carry_ref[...] = lax.fori_loop(0, tt // SCAN_ROWS, slab, carry_ref[...])
    hy_ref[...] = (u_ref[...] * y_ref[...]).astype(hy_ref.dtype)


def _lru_core(r, y, conv_w, conv_b, wgx, wga, bgx, bga, a_param):
    t, d = r.shape
    tt = min(TOKEN_TILE, t)
    row = lambda i: (i, 0)
    fixed2 = lambda i: (0, 0)
    fixed3 = lambda i: (0, 0, 0)
    return pl.pallas_call(
        _lru_core_kernel,
        grid=(t // tt,),
        in_specs=[pl.BlockSpec((tt, d), row),
                  pl.BlockSpec((tt, d), row),
                  pl.BlockSpec(conv_w.shape, fixed2),
                  pl.BlockSpec((1, d), fixed2),
                  pl.BlockSpec(wgx.shape, fixed3),
                  pl.BlockSpec(wga.shape, fixed3),
                  pl.BlockSpec((1, d), fixed2),
                  pl.BlockSpec((1, d), fixed2),
                  pl.BlockSpec((1, d), fixed2)],
        out_specs=pl.BlockSpec((tt, d), row),
        out_shape=jax.ShapeDtypeStruct((t, d), BF16),
        scratch_shapes=[pltpu.VMEM((8, d), F32),
                        pltpu.VMEM((1, d), F32),
                        pltpu.VMEM((tt, d), F32),
                        pltpu.VMEM((tt, d), F32)],
        compiler_params=_params(1),
        name="lru_core",
    )(r, y, conv_w, conv_b, wgx, wga, bgx, bga, a_param)


def _moe(xn_tiles, t, d, ids, gates, rank, counts128, layer, wg, wu, wd, bg, bu, bd,
         ln_g, ln_b, alpha):
    tm = EXPERT_ROWS
    n_exp = wg.shape[1]
    counts = counts128[:, 0].astype(I32)
    nch = (counts + tm - 1) // tm
    start = (jnp.cumsum(nch) - nch).astype(I32) * tm
    experts = jnp.arange(n_exp, dtype=I32)
    dest = rank + jnp.sum(jnp.where(ids[:, :, None] == experts, start, 0), axis=-1)
    last = ((jnp.sum(nch) - 1) * tm).astype(I32).reshape(1)
    tok_bits = (t + COMBINE_BATCH - 1).bit_length()
    gates_ext = jnp.pad(gates, ((0, 1), (0, (1 << tok_bits) - t))).reshape(-1)
    return _moe_experts(xn_tiles, t, d, tok_bits, dest.reshape(-1).astype(I32), gates_ext,
                        start, counts, nch.astype(I32), last, layer, wg, wu, wd, bg, bu, bd,
                        ln_g, ln_b, alpha)


def kernel(x, rel_bias_table, w_qkv, w_attn_out, w_lru_in, lru_conv_w, lru_conv_b, w_lru_gate_x, b_lru_gate_x, w_lru_gate_a, b_lru_gate_a, lru_a_param, w_lru_out, w_router, b_router, w_exp_gate, b_exp_gate, w_exp_up, b_exp_up, w_exp_down, b_exp_down, ln_mix_g, ln_mix_b, ln_ffn_g, ln_ffn_b):
    bsz, t, d = x.shape
    depth = w_router.shape[0]
    n_mixers = 2
    alpha = (2 * depth) ** 0.25
    assert t % MOBA_BLOCK == 0 and t % TOKEN_TILE == 0

    biases = _attn_biases(rel_bias_table, t)
    w_qkv_b = w_qkv.astype(BF16)
    w_attn_out_b = w_attn_out.astype(BF16)
    w_lru_in_b = w_lru_in.astype(BF16)
    w_lru_out_b = w_lru_out.astype(BF16)
    wgx_b = w_lru_gate_x.astype(BF16)
    wga_b = w_lru_gate_a.astype(BF16)
    wg_b = w_exp_gate.astype(BF16)
    wu_b = w_exp_up.astype(BF16)
    wd_b = w_exp_down.astype(BF16)
    wr_t = jnp.swapaxes(w_router, 1, 2)
    wr_hi = wr_t.astype(BF16)
    wr_lo = (wr_t - wr_hi.astype(F32)).astype(BF16)

    bg4 = b_exp_gate[:, :, None, :]
    bu4 = b_exp_up[:, :, None, :]
    bd4 = b_exp_down[:, :, None, :]

    outs = []
    for b in range(bsz):
        xs = x[b]
        for i in range(depth):
            j = i // n_mixers
            if i % n_mixers == 0:
                qkv = _matmul(xs, t, w_qkv_b[j], BF16)
                a = _moba_attention(qkv, biases)
                w_out = w_attn_out_b[j]
            else:
                y, r = _lru_in(xs, t, w_lru_in_b[j])
                a = _lru_core(r, y, lru_conv_w[j], lru_conv_b[j][None], wgx_b[j], wga_b[j],
                              b_lru_gate_x[j][None], b_lru_gate_a[j][None], lru_a_param[j][None])
                w_out = w_lru_out_b[j]
            xn, ids, gates, rank, counts = _proj_ln_router(
                a, w_out, xs, ln_mix_g[i][None], ln_mix_b[i][None],
                wr_hi[i], wr_lo[i], b_router[i][:, None], alpha)
            xs = _moe(xn, t, d, ids, gates, rank, counts, i, wg_b, wu_b, wd_b, bg4, bu4, bd4,
                      ln_ffn_g[i][None], ln_ffn_b[i][None], alpha)
        outs.append(xs.reshape(t + COMBINE_BATCH, d)[:t])
    return jnp.stack(outs, axis=0)
```

```python
import functools
import math

import numpy as np
import jax
import jax.numpy as jnp
from jax import lax
from jax.experimental import pallas as pl
from jax.experimental.pallas import tpu as pltpu

F32 = jnp.float32
BF16 = jnp.bfloat16
I32 = jnp.int32

N_HEADS = 8
MOBA_BLOCK = 256
MOBA_TOPK = 3
NUM_BUCKETS = 32
REL_MAX_DISTANCE = 128
LRU_C = 8.0
EXPERT_TOPK = 4
SWIGLU_LIMIT = 7.0
SWIGLU_ALPHA = 1.702
LN_EPS = 1e-5
LOG2_E = 1.4426950408889634

EXPERT_ROWS = 256
TOKEN_TILE = 512
LANES = 128
SUBLANES = 8
SCAN_ROWS = SUBLANES
HEAD_GROUP = 8
COMBINE_BATCH = 8
V7X_VMEM_LIMIT = 56 * 1024 * 1024

_NT = (((1,), (1,)), ((), ()))


def _params(n_axes, vmem=V7X_VMEM_LIMIT):
    return pltpu.CompilerParams(
        dimension_semantics=("arbitrary",) * n_axes, vmem_limit_bytes=vmem)


def _sigmoid(x):
    return 0.5 * jnp.tanh(0.5 * x) + 0.5


def _layer_norm(z, g, b):
    mu = jnp.mean(z, axis=-1, keepdims=True)
    zc = z - mu
    var = jnp.mean(zc * zc, axis=-1, keepdims=True)
    return zc * lax.rsqrt(var + LN_EPS) * g + b


def _rows_from_tiles(ref, n_sub, first=0, rows=None, dtype=None):
    if ref.shape[1] == n_sub * LANES:
        return ref[...] if dtype is None else ref[...].astype(dtype)
    rows = ref.shape[0] // n_sub if rows is None else rows
    cols = [ref[pl.ds(first + s, rows, stride=n_sub), :] for s in range(n_sub)]
    if dtype is not None:
        cols = [c.astype(dtype) for c in cols]
    return jnp.concatenate(cols, axis=1)


def _rows_to_tiles(ref, value, first=0):
    n_sub = value.shape[1] // LANES
    for s in range(n_sub):
        ref[pl.ds(first + s, value.shape[0], stride=n_sub), :] = value[:, s * LANES:(s + 1) * LANES]


def _matmul_kernel(x_ref, w_ref, o_ref):
    x = _rows_from_tiles(x_ref, w_ref.shape[0] // LANES, dtype=BF16)
    o_ref[...] = jnp.dot(x, w_ref[...], preferred_element_type=F32).astype(o_ref.dtype)


def _x_block(x, d, m, tm):
    if x.shape == (m, d):
        return pl.BlockSpec((tm, d), lambda i: (i, 0))
    return pl.BlockSpec((tm * (d // LANES), LANES), lambda i: (i, 0))


def _matmul(x, m, w, out_dtype):
    k, n = w.shape
    tm = min(TOKEN_TILE, m)
    return pl.pallas_call(
        _matmul_kernel,
        grid=(m // tm,),
        in_specs=[_x_block(x, k, m, tm),
                  pl.BlockSpec((k, n), lambda i: (0, 0))],
        out_specs=pl.BlockSpec((tm, n), lambda i: (i, 0)),
        out_shape=jax.ShapeDtypeStruct((m, n), out_dtype),
        compiler_params=_params(1),
        name="dense_proj",
    )(x, w)


def _attn_kernel(q_ref, k_ref, v_ref, bself_ref, badj_ref, bfar_ref, o_ref,
                 kmean_ref, vt_ref, pen_ref, m_ref, l_ref, acc_ref, sa_ref, sb_ref, *, nblk, scale):
    blk = MOBA_BLOCK
    hg = bself_ref.shape[0]
    dh = q_ref.shape[1] // hg
    i = pl.program_id(1)

    def head(ref, h):
        return ref.at[:, h * dh:(h + 1) * dh]

    @pl.when(i == 0)
    def _():
        for h in range(hg):
            for j in range(nblk):
                kj = head(k_ref, h)[j * blk:(j + 1) * blk, :].astype(F32)
                kmean_ref[h, j:j + 1, :] = jnp.sum(kj, axis=0, keepdims=True) * (1.0 / blk)
                vt_ref[h, j] = head(v_ref, h)[j * blk:(j + 1) * blk, :].astype(F32).T.astype(BF16)

    qs = []
    for h in range(hg):
        q = head(q_ref, h)[...]

        km = kmean_ref[h]
        km_hi = km.astype(BF16)
        km_lo = (km - km_hi.astype(F32)).astype(BF16)
        gate = (lax.dot_general(km_hi, q, _NT, preferred_element_type=F32)
                + lax.dot_general(km_lo, q, _NT, preferred_element_type=F32))
        jj = lax.broadcasted_iota(I32, gate.shape, 0)
        past = jj < i
        g = jnp.where(past, gate, -jnp.inf)
        sel = jnp.zeros(gate.shape, jnp.bool_)
        for _ in range(min(MOBA_TOPK, nblk)):
            mx = jnp.max(g, axis=0, keepdims=True)
            jm = jnp.min(jnp.where(g == mx, jj, nblk), axis=0, keepdims=True)
            pick = jj == jm
            sel = jnp.logical_or(sel, pick)
            g = jnp.where(pick, -jnp.inf, g)
        pen_ref[h] = jnp.where(jnp.logical_and(sel, past), 0.0, -jnp.inf)
        qs.append((q.astype(F32) * (scale * LOG2_E)).astype(BF16))

    def scores(h, j):
        kb = head(k_ref, h)[pl.ds(pl.multiple_of(j * blk, blk), blk), :]
        return lax.dot_general(kb, qs[h], _NT, preferred_element_type=F32)

    def fold(x, op):
        return op(x.reshape(blk // SUBLANES, SUBLANES, blk), axis=0)

    def adj_bias(h):
        return badj_ref[h] + pen_ref[h, pl.ds(jnp.maximum(i - 1, 0), 1), :]

    def far_bias(h, j):
        return bfar_ref[h, 0:1, 0:1] + pen_ref[h, pl.ds(j, 1), :]

    n_far = jnp.maximum(i - 1, 0)


    prev_blk = jnp.maximum(i - 1, 0)
    ss = [scores(h, i) + bself_ref[h] for h in range(hg)]
    for h in range(hg):
        sb_ref[h] = scores(h, prev_blk)
    for h in range(hg):
        sa_ref[h] = scores(h, 0)

    for h in range(hg):
        m0 = jnp.max(fold(ss[h], jnp.max), axis=0, keepdims=True)
        p = jnp.exp2(ss[h] - m0)
        m_ref[h] = m0
        l_ref[h] = fold(p, jnp.sum)
        acc_ref[h] = jnp.dot(vt_ref[h, i], p.astype(BF16), preferred_element_type=F32)

    def update(h, j, s, s_max, bias):
        if bias is not None:
            s_max = s_max + bias
        m_old = m_ref[h]
        m_new = jnp.maximum(m_old, jnp.max(s_max, axis=0, keepdims=True))
        alpha = jnp.exp2(m_old - m_new)
        p = jnp.exp2(s - m_new if bias is None else s + (bias - m_new))
        m_ref[h] = m_new
        l_ref[h] = alpha * l_ref[h] + fold(p, jnp.sum)
        acc_ref[h] = alpha * acc_ref[h] + jnp.dot(
            vt_ref[h, j], p.astype(BF16), preferred_element_type=F32)

    for h in range(hg):
        s = sb_ref[h] + adj_bias(h)
        update(h, prev_blk, s, fold(s, jnp.max), None)

    def stage(j, cur_ref, nxt_ref):
        nxt = jnp.minimum(j + 1, n_far - 1)
        for h in range(hg):
            nxt_ref[h] = scores(h, nxt)
        for h in range(hg):
            s = cur_ref[h]
            update(h, j, s, fold(s, jnp.max), far_bias(h, j))

    def far_pair(p, carry):
        stage(2 * p, sa_ref, sb_ref)

        @pl.when(2 * p + 1 < n_far)
        def _():
            stage(2 * p + 1, sb_ref, sa_ref)

        return carry

    lax.fori_loop(0, (n_far + 1) // 2, far_pair, 0)

    for h in range(hg):
        o = acc_ref[h] / jnp.sum(l_ref[h], axis=0, keepdims=True)
        head(o_ref, h)[...] = o.T.astype(o_ref.dtype)


def _t5_bucket_np(dist):
    max_exact = NUM_BUCKETS // 2
    n = dist.astype(np.float32)
    large = max_exact + (np.log(np.maximum(n, np.float32(1.0)) / np.float32(max_exact))
                         / np.float32(math.log(REL_MAX_DISTANCE / max_exact))
                         * np.float32(NUM_BUCKETS - max_exact)).astype(np.int32)
    large = np.minimum(large, NUM_BUCKETS - 1)
    return np.where(dist < max_exact, dist, large)


def _t5_bucket(dist):
    max_exact = NUM_BUCKETS // 2
    n = dist.astype(F32)
    large = max_exact + (jnp.log(jnp.maximum(n, 1.0) / max_exact)
                         / math.log(REL_MAX_DISTANCE / max_exact)
                         * (NUM_BUCKETS - max_exact)).astype(I32)
    large = jnp.minimum(large, NUM_BUCKETS - 1)
    return jnp.where(dist < max_exact, dist, large)


def _attn_biases(rel_bias_table, t):
    blk = MOBA_BLOCK
    far_buckets = _t5_bucket_np(np.arange(blk + 1, max(t, 2 * blk) + 1, dtype=np.int32))
    assert (far_buckets == NUM_BUCKETS - 1).all()
    bbd = rel_bias_table.astype(F32)[_t5_bucket(jnp.arange(2 * blk + 1, dtype=I32))].T * LOG2_E
    n_heads = bbd.shape[0]

    def toeplitz(w):
        rep = jnp.tile(w, (1, blk))[:, :blk * (2 * blk - 1)].reshape(n_heads, blk, 2 * blk - 1)
        return rep[:, :, blk - 1:]

    w_self = jnp.concatenate([jnp.full((n_heads, blk - 1), -jnp.inf, F32), bbd[:, :blk + 1]], axis=1)
    w_adj = jnp.concatenate([bbd[:, 1:2 * blk], bbd[:, :1]], axis=1)
    bself = toeplitz(w_self)
    badj = toeplitz(w_adj)
    bfar = jnp.broadcast_to(bbd[:, 2 * blk][:, None, None], (n_heads, SUBLANES, LANES))
    return bself, badj, bfar


def _moba_attention(qkv, biases):
    t, d3 = qkv.shape
    d = d3 // 3
    dh = d // N_HEADS
    blk = MOBA_BLOCK
    nblk = t // blk
    bself, badj, bfar = biases
    hg = HEAD_GROUP
    n_grp = N_HEADS // hg
    kernel = functools.partial(_attn_kernel, nblk=nblk, scale=dh ** -0.5)
    return pl.pallas_call(
        kernel,
        grid=(n_grp, nblk),
        in_specs=[pl.BlockSpec((blk, hg * dh), lambda g, i: (i, g)),
                  pl.BlockSpec((t, hg * dh), lambda g, i: (0, n_grp + g)),
                  pl.BlockSpec((t, hg * dh), lambda g, i: (0, 2 * n_grp + g)),
                  pl.BlockSpec((hg, blk, blk), lambda g, i: (g, 0, 0)),
                  pl.BlockSpec((hg, blk, blk), lambda g, i: (g, 0, 0)),
                  pl.BlockSpec((hg, SUBLANES, LANES), lambda g, i: (g, 0, 0))],
        out_specs=pl.BlockSpec((blk, hg * dh), lambda g, i: (i, g)),
        out_shape=jax.ShapeDtypeStruct((t, d), BF16),
        scratch_shapes=[pltpu.VMEM((hg, nblk, dh), F32),
                        pltpu.VMEM((hg, nblk, dh, blk), BF16),
                        pltpu.VMEM((hg, nblk, blk), F32),
                        pltpu.VMEM((hg, 1, blk), F32),
                        pltpu.VMEM((hg, SUBLANES, blk), F32),
                        pltpu.VMEM((hg, dh, blk), F32),
                        pltpu.VMEM((hg, blk, blk), F32),
                        pltpu.VMEM((hg, blk, blk), F32)],
        compiler_params=_params(2),
        name="moba_attention",
    )(qkv, qkv, qkv, bself, badj, bfar)


def _proj_ln_router_kernel(a_ref, w_ref, x_ref, g_ref, b_ref, wrh_ref, wrl_ref, br_ref,
                           xn_ref, ids_ref, gates_ref, rank_ref, counts_ref, base_ref,
                           *, alpha, n_exp):
    t = pl.program_id(0)
    tm = a_ref.shape[0]

    @pl.when(t == 0)
    def _():
        base_ref[...] = jnp.zeros_like(base_ref)

    mix = jnp.dot(a_ref[...].astype(BF16), w_ref[...], preferred_element_type=F32)
    x_res = _rows_from_tiles(x_ref, w_ref.shape[1] // LANES)
    xn = _layer_norm(alpha * x_res + mix, g_ref[...], b_ref[...])
    _rows_to_tiles(xn_ref, xn)

    x_hi = xn.astype(BF16)
    x_lo = (xn - x_hi.astype(F32)).astype(BF16)
    wr_hi = wrh_ref[...]
    logits = (lax.dot_general(wr_hi, x_hi, _NT, preferred_element_type=F32)
              + lax.dot_general(wr_hi, x_lo, _NT, preferred_element_type=F32)
              + lax.dot_general(wrl_ref[...], x_hi, _NT, preferred_element_type=F32)
              + br_ref[...])

    e_iota = lax.broadcasted_iota(I32, logits.shape, 0)
    work = logits
    tops, picks = [], []
    for k in range(EXPERT_TOPK):
        mx = jnp.max(work, axis=0, keepdims=True)
        em = jnp.min(jnp.where(work == mx, e_iota, n_exp), axis=0, keepdims=True)
        pick = e_iota == em
        ids_ref[k:k + 1, :] = em
        tops.append(mx)
        picks.append(pick)
        work = jnp.where(pick, -jnp.inf, work)

    ex = [jnp.exp(tk - tops[0]) for tk in tops]
    den = ex[0]
    for e in ex[1:]:
        den = den + e
    for k in range(EXPERT_TOPK):
        gates_ref[k:k + 1, :] = ex[k] / den

    onehot = jnp.zeros(logits.shape, F32)
    for pick in picks:
        onehot = onehot + pick.astype(F32)
    earlier = (lax.broadcasted_iota(I32, (tm, tm), 0)
               < lax.broadcasted_iota(I32, (tm, tm), 1)).astype(BF16)
    pos = jnp.dot(onehot.astype(BF16), earlier, preferred_element_type=F32) + base_ref[:, 0:1]
    for k in range(EXPERT_TOPK):
        rank_ref[k:k + 1, :] = jnp.sum(jnp.where(picks[k], pos, 0.0), axis=0,
                                       keepdims=True).astype(I32)
    total = base_ref[...] + jnp.sum(onehot, axis=1, keepdims=True)
    base_ref[...] = total
    counts_ref[...] = total


def _proj_ln_router(a, w, x_tiles, ln_g, ln_b, wr_hi, wr_lo, br, alpha):
    t, d = a.shape
    n_exp = wr_hi.shape[0]
    tm = min(TOKEN_TILE, t)
    kernel = functools.partial(_proj_ln_router_kernel, alpha=alpha, n_exp=n_exp)
    row = lambda i: (i, 0)
    fixed = lambda i: (0, 0)
    col = lambda i: (0, i)
    return pl.pallas_call(
        kernel,
        grid=(t // tm,),
        in_specs=[pl.BlockSpec((tm, d), row),
                  pl.BlockSpec((d, d), fixed),
                  _x_block(x_tiles, d, t, tm),
                  pl.BlockSpec((1, d), fixed),
                  pl.BlockSpec((1, d), fixed),
                  pl.BlockSpec((n_exp, d), fixed),
                  pl.BlockSpec((n_exp, d), fixed),
                  pl.BlockSpec((n_exp, 1), fixed)],
        out_specs=[pl.BlockSpec((tm * (d // LANES), LANES), row),
                   pl.BlockSpec((EXPERT_TOPK, tm), col),
                   pl.BlockSpec((EXPERT_TOPK, tm), col),
                   pl.BlockSpec((EXPERT_TOPK, tm), col),
                   pl.BlockSpec((n_exp, 128), fixed)],
        out_shape=[jax.ShapeDtypeStruct((t * (d // LANES), LANES), F32),
                   jax.ShapeDtypeStruct((EXPERT_TOPK, t), I32),
                   jax.ShapeDtypeStruct((EXPERT_TOPK, t), F32),
                   jax.ShapeDtypeStruct((EXPERT_TOPK, t), I32),
                   jax.ShapeDtypeStruct((n_exp, 128), F32)],
        scratch_shapes=[pltpu.VMEM((n_exp, 128), F32)],
        compiler_params=_params(1),
        name="proj_ln_router",
    )(a, w, x_tiles, ln_g, ln_b, wr_hi, wr_lo, br)


def _moe_kernel(dest_ref, gate_ref, start_ref, count_ref, nch_ref, last_ref,
                x2_ref, wg_ref, wu_ref, wd_ref, bg_ref, bu_ref, bd_ref, lng_ref, lnb_ref,
                out_ref, src_ref, xg_ref, yf_ref, *, n_tok, n_exp, tok_bits, ln_alpha):
    e = pl.program_id(0)
    n_sub = wg_ref.shape[0] // LANES
    tm = xg_ref.shape[0] // n_sub
    spare = COMBINE_BATCH

    def tile(ref, idx):
        return ref.at[pl.ds(pl.multiple_of(idx * n_sub, n_sub), n_sub), :]

    tok_mask = (1 << tok_bits) - 1

    def gather_row(base, r):
        tok = jnp.minimum(src_ref[base + r] & tok_mask, n_tok - 1)
        tile(xg_ref, r)[...] = tile(x2_ref, tok)[...]

    def combine_batch(base, r0):
        codes = [src_ref[base + r0 + k] for k in range(spare)]
        toks = [c & tok_mask for c in codes]
        gs = [gate_ref[c] for c in codes]
        olds = [tile(out_ref, toks[k])[...] for k in range(spare)]
        for k in range(spare):
            tile(out_ref, toks[k])[...] = olds[k] + gs[k] * tile(yf_ref, r0 + k)[...]

    @pl.when(e == 0)
    def _():
        out_ref[...] = jnp.zeros_like(out_ref)
        yf_ref[...] = jnp.zeros_like(yf_ref)

        def fill(r, carry):
            src_ref[r] = n_tok + r % spare + (EXPERT_TOPK << tok_bits)
            return carry

        for ee in range(n_exp):
            lax.fori_loop(start_ref[ee] + count_ref[ee], start_ref[ee] + nch_ref[ee] * tm, fill, 0)

        def scatter(a, carry):
            for k in range(EXPERT_TOPK):
                src_ref[dest_ref[k * n_tok + a]] = a + (k << tok_bits)
            return carry

        lax.fori_loop(0, n_tok, scatter, 0, unroll=8)

        def first_rows(r, carry):
            gather_row(0, r)
            return carry

        lax.fori_loop(0, tm, first_rows, 0, unroll=8)

    def visit(base, rows):
        x = jnp.concatenate(
            [xg_ref[pl.ds(s, rows, stride=n_sub), :].astype(BF16) for s in range(n_sub)], axis=1)

        nxt = jnp.minimum(base + tm, last_ref[0])
        for r in range(tm):
            gather_row(nxt, r)
        prev = jnp.maximum(base - tm, 0)
        for r0 in range(0, tm, spare):
            combine_batch(prev, r0)

        g = jnp.dot(x, wg_ref[...], preferred_element_type=F32) + bg_ref[...]
        u = jnp.dot(x, wu_ref[...], preferred_element_type=F32) + bu_ref[...]
        g = jnp.minimum(g, SWIGLU_LIMIT)
        u = jnp.clip(u, -SWIGLU_LIMIT, SWIGLU_LIMIT)
        h = g * _sigmoid(SWIGLU_ALPHA * g) * (u + 1.0)
        y = jnp.dot(h.astype(BF16), wd_ref[...], preferred_element_type=F32) + bd_ref[...]
        for s in range(n_sub):
            yf_ref[pl.ds(s, rows, stride=n_sub), :] = y[:, s * LANES:(s + 1) * LANES]

    def chunk(ch, carry):
        base = start_ref[e] + ch * tm
        n_real = count_ref[e] - ch * tm

        @pl.when(n_real > tm // 2)
        def _():
            visit(base, tm)

        @pl.when(n_real <= tm // 2)
        def _():
            visit(base, tm // 2)

        return carry

    lax.fori_loop(0, nch_ref[e], chunk, 0)

    @pl.when(e == n_exp - 1)
    def _():
        def last_rows(b, carry):
            combine_batch(last_ref[0], b * spare)
            return carry

        lax.fori_loop(0, tm // spare, last_rows, 0)

        ln_rows = min(EXPERT_ROWS, n_tok)

        def normalize(b, carry):
            first = pl.multiple_of(b * (ln_rows * n_sub), ln_rows * n_sub)
            z = (ln_alpha * _rows_from_tiles(x2_ref, n_sub, first, ln_rows)
                 + _rows_from_tiles(out_ref, n_sub, first, ln_rows))
            _rows_to_tiles(out_ref, _layer_norm(z, lng_ref[...], lnb_ref[...]), first)
            return carry

        lax.fori_loop(0, n_tok // ln_rows, normalize, 0)


def _moe_experts(x_tiles, t, d, tok_bits, dest, gates, start, count, nch, last, layer,
                 wg, wu, wd, bg, bu, bd, ln_g, ln_b, alpha):
    n_sub = d // LANES
    tm = EXPERT_ROWS
    n_exp, dff = wg.shape[1], wg.shape[3]
    n_rows = t * EXPERT_TOPK + n_exp * tm

    def w_map(e, *_):
        return (layer, e, 0, 0)

    out_rows = (t + COMBINE_BATCH) * n_sub
    grid_spec = pltpu.PrefetchScalarGridSpec(
        num_scalar_prefetch=6,
        grid=(n_exp,),
        in_specs=[pl.BlockSpec((t * n_sub, LANES), lambda e, *_: (0, 0)),
                  pl.BlockSpec((None, None, d, dff), w_map),
                  pl.BlockSpec((None, None, d, dff), w_map),
                  pl.BlockSpec((None, None, dff, d), w_map),
                  pl.BlockSpec((None, None, 1, dff), w_map),
                  pl.BlockSpec((None, None, 1, dff), w_map),
                  pl.BlockSpec((None, None, 1, d), w_map),
                  pl.BlockSpec((1, d), lambda e, *_: (0, 0)),
                  pl.BlockSpec((1, d), lambda e, *_: (0, 0))],
        out_specs=pl.BlockSpec((out_rows, LANES), lambda e, *_: (0, 0)),
        scratch_shapes=[pltpu.SMEM((n_rows,), I32),
                        pltpu.VMEM((tm * n_sub, LANES), F32),
                        pltpu.VMEM((tm * n_sub, LANES), F32)],
    )
    return pl.pallas_call(
        functools.partial(_moe_kernel, n_tok=t, n_exp=n_exp, tok_bits=tok_bits, ln_alpha=alpha),
        grid_spec=grid_spec,
        out_shape=jax.ShapeDtypeStruct((out_rows, LANES), F32),
        compiler_params=_params(1),
        name="moe_experts",
    )(dest, gates, start, count, nch, last, x_tiles, wg, wu, wd, bg, bu, bd, ln_g, ln_b)


def _lru_in_kernel(x_ref, w_ref, y_ref, r_ref):
    d = y_ref.shape[1]
    x = _rows_from_tiles(x_ref, w_ref.shape[0] // LANES, dtype=BF16)
    yz = jnp.dot(x, w_ref[...], preferred_element_type=F32)
    y = yz[:, :d]
    y_ref[...] = 0.5 * y * (1.0 + jnp.tanh(math.sqrt(2.0 / math.pi) * (y + 0.044715 * (y * y * y))))
    r_ref[...] = yz[:, d:]


def _lru_in(x_tiles, t, w_in):
    d = w_in.shape[0]
    n = w_in.shape[1] // 2
    tm = min(TOKEN_TILE, t)
    return pl.pallas_call(
        _lru_in_kernel,
        grid=(t // tm,),
        in_specs=[_x_block(x_tiles, d, t, tm),
                  pl.BlockSpec((d, 2 * n), lambda i: (0, 0))],
        out_specs=[pl.BlockSpec((tm, n), lambda i: (i, 0)),
                   pl.BlockSpec((tm, n), lambda i: (i, 0))],
        out_shape=[jax.ShapeDtypeStruct((t, n), F32), jax.ShapeDtypeStruct((t, n), F32)],
        compiler_params=_params(1),
        name="lru_in",
    )(x_tiles, w_in)


def _lru_core_kernel(r_ref, y_ref, cw_ref, cb_ref, wgx_ref, wga_ref, bgx_ref, bga_ref, ap_ref,
                     hy_ref, rprev_ref, carry_ref, a_ref, u_ref):
    t = pl.program_id(0)
    tt, d = r_ref.shape
    n_blk, bw, _ = wgx_ref.shape
    cwid = cw_ref.shape[0]

    @pl.when(t == 0)
    def _():
        rprev_ref[...] = jnp.zeros_like(rprev_ref)
        carry_ref[...] = jnp.zeros_like(carry_ref)

    r = r_ref[...]
    tail = rprev_ref.shape[0]
    rext = jnp.concatenate([rprev_ref[...], r], axis=0)
    rc = jnp.broadcast_to(cb_ref[...], (tt, d))
    for w in range(cwid):
        off = tail - (cwid - 1) + w
        rc = rc + cw_ref[w:w + 1, :] * rext[off:off + tt, :]
    rprev_ref[...] = r[tt - tail:, :]

    rcb = rc.astype(BF16)
    gx = jnp.concatenate(
        [jnp.dot(rcb[:, n * bw:(n + 1) * bw], wgx_ref[n], preferred_element_type=F32)
         for n in range(n_blk)], axis=1)
    ga = jnp.concatenate(
        [jnp.dot(rcb[:, n * bw:(n + 1) * bw], wga_ref[n], preferred_element_type=F32)
         for n in range(n_blk)], axis=1)
    gate_x = _sigmoid(gx + bgx_ref[...])
    gate_a = _sigmoid(ga + bga_ref[...])
    z = -ap_ref[...]
    softplus = jnp.maximum(z, 0.0) + jnp.log(1.0 + jnp.exp(-jnp.abs(z)))
    log_a = -LRU_C * gate_a * softplus
    a_ref[...] = jnp.exp(log_a)
    u_ref[...] = jnp.sqrt(1.0 - jnp.exp(2.0 * log_a)) * (gate_x * rc)

    row = lax.broadcasted_iota(I32, (SCAN_ROWS, d), 0)

    def slab(s, carry):
        off = pl.multiple_of(s * SCAN_ROWS, SCAN_ROWS)
        a = a_ref[pl.ds(off, SCAN_ROWS), :]
        b = u_ref[pl.ds(off, SCAN_ROWS), :]
        step = 1
        while step < SCAN_ROWS:
            a_sh = pltpu.roll(a, step, 0)
            b_sh = pltpu.roll(b, step, 0)
            live = row >= step
            b = jnp.where(live, a * b_sh + b, b)
            a = jnp.where(live, a * a_sh, a)
            step *= 2
        h = b + a * carry
        u_ref[pl.ds(off, SCAN_ROWS), :] = h
        return h[SCAN_ROWS - 1:SCAN_ROWS, :]

    carry_ref[...] = lax.fori_loop(0, tt // SCAN_ROWS, slab, carry_ref[...])
    hy_ref[...] = (u_ref[...] * y_ref[...]).astype(hy_ref.dtype)


def _lru_core(r, y, conv_w, conv_b, wgx, wga, bgx, bga, a_param):
    t, d = r.shape
    tt = min(TOKEN_TILE, t)
    row = lambda i: (i, 0)
    fixed2 = lambda i: (0, 0)
    fixed3 = lambda i: (0, 0, 0)
    return pl.pallas_call(
        _lru_core_kernel,
        grid=(t // tt,),
        in_specs=[pl.BlockSpec((tt, d), row),
                  pl.BlockSpec((tt, d), row),
                  pl.BlockSpec(conv_w.shape, fixed2),
                  pl.BlockSpec((1, d), fixed2),
                  pl.BlockSpec(wgx.shape, fixed3),
                  pl.BlockSpec(wga.shape, fixed3),
                  pl.BlockSpec((1, d), fixed2),
                  pl.BlockSpec((1, d), fixed2),
                  pl.BlockSpec((1, d), fixed2)],
        out_specs=pl.BlockSpec((tt, d), row),
        out_shape=jax.ShapeDtypeStruct((t, d), BF16),
        scratch_shapes=[pltpu.VMEM((SUBLANES, d), F32),
                        pltpu.VMEM((1, d), F32),
                        pltpu.VMEM((tt, d), F32),
                        pltpu.VMEM((tt, d), F32)],
        compiler_params=_params(1),
        name="lru_core",
    )(r, y, conv_w, conv_b, wgx, wga, bgx, bga, a_param)


def _moe(xn_tiles, t, d, ids, gates, rank, counts128, layer, wg, wu, wd, bg, bu, bd,
         ln_g, ln_b, alpha):
    tm = EXPERT_ROWS
    n_exp = wg.shape[1]
    counts = counts128[:, 0].astype(I32)
    nch = (counts + tm - 1) // tm
    start = (jnp.cumsum(nch) - nch).astype(I32) * tm
    experts = jnp.arange(n_exp, dtype=I32)
    dest = rank + jnp.sum(jnp.where(ids[:, :, None] == experts, start, 0), axis=-1)
    last = ((jnp.sum(nch) - 1) * tm).astype(I32).reshape(1)
    tok_bits = (t + COMBINE_BATCH - 1).bit_length()
    gates_ext = jnp.pad(gates, ((0, 1), (0, (1 << tok_bits) - t))).reshape(-1)
    return _moe_experts(xn_tiles, t, d, tok_bits, dest.reshape(-1).astype(I32), gates_ext,
                        start, counts, nch.astype(I32), last, layer, wg, wu, wd, bg, bu, bd,
                        ln_g, ln_b, alpha)


def kernel(x, rel_bias_table, w_qkv, w_attn_out, w_lru_in, lru_conv_w, lru_conv_b, w_lru_gate_x, b_lru_gate_x, w_lru_gate_a, b_lru_gate_a, lru_a_param, w_lru_out, w_router, b_router, w_exp_gate, b_exp_gate, w_exp_up, b_exp_up, w_exp_down, b_exp_down, ln_mix_g, ln_mix_b, ln_ffn_g, ln_ffn_b):
    bsz, t, d = x.shape
    depth = w_router.shape[0]
    n_mixers = 2
    alpha = (2 * depth) ** 0.25
    assert t % MOBA_BLOCK == 0 and t % TOKEN_TILE == 0

    biases = _attn_biases(rel_bias_table, t)
    w_qkv_b = w_qkv.astype(BF16)
    w_attn_out_b = w_attn_out.astype(BF16)
    w_lru_in_b = w_lru_in.astype(BF16)
    w_lru_out_b = w_lru_out.astype(BF16)
    wgx_b = w_lru_gate_x.astype(BF16)
    wga_b = w_lru_gate_a.astype(BF16)
    wg_b = w_exp_gate.astype(BF16)
    wu_b = w_exp_up.astype(BF16)
    wd_b = w_exp_down.astype(BF16)
    wr_t = jnp.swapaxes(w_router, 1, 2)
    wr_hi = wr_t.astype(BF16)
    wr_lo = (wr_t - wr_hi.astype(F32)).astype(BF16)

    bg4 = b_exp_gate[:, :, None, :]
    bu4 = b_exp_up[:, :, None, :]
    bd4 = b_exp_down[:, :, None, :]

    outs = []
    for b in range(bsz):
        xs = x[b]
        for i in range(depth):
            j = i // n_mixers
            if i % n_mixers == 0:
                qkv = _matmul(xs, t, w_qkv_b[j], BF16)
                a = _moba_attention(qkv, biases)
                w_out = w_attn_out_b[j]
            else:
                y, r = _lru_in(xs, t, w_lru_in_b[j])
                a = _lru_core(r, y, lru_conv_w[j], lru_conv_b[j][None], wgx_b[j], wga_b[j],
                              b_lru_gate_x[j][None], b_lru_gate_a[j][None], lru_a_param[j][None])
                w_out = w_lru_out_b[j]
            xn, ids, gates, rank, counts = _proj_ln_router(
                a, w_out, xs, ln_mix_g[i][None], ln_mix_b[i][None],
                wr_hi[i], wr_lo[i], b_router[i][:, None], alpha)
            xs = _moe(xn, t, d, ids, gates, rank, counts, i, wg_b, wu_b, wd_b, bg4, bu4, bd4,
                      ln_ffn_g[i][None], ln_ffn_b[i][None], alpha)
        outs.append(xs.reshape(t + COMBINE_BATCH, d)[:t])
    return jnp.stack(outs, axis=0)
```

```python
import functools
import math

import numpy as np
import jax
import jax.numpy as jnp
from jax import lax
from jax.experimental import pallas as pl
from jax.experimental.pallas import tpu as pltpu

F32 = jnp.float32
BF16 = jnp.bfloat16
I32 = jnp.int32

N_HEADS = 8
MOBA_BLOCK = 256
MOBA_TOPK = 3
NUM_BUCKETS = 32
REL_MAX_DISTANCE = 128
LRU_C = 8.0
EXPERT_TOPK = 4
SWIGLU_LIMIT = 7.0
SWIGLU_ALPHA = 1.702
LN_EPS = 1e-5
LOG2_E = 1.4426950408889634

EXPERT_ROWS = 256
TOKEN_TILE = 512
LANES = 128
SUBLANES = 8
SCAN_ROWS = SUBLANES
HEAD_GROUP = 8
COMBINE_BATCH = 8
V7X_VMEM_LIMIT = 56 * 1024 * 1024

_NT = (((1,), (1,)), ((), ()))


def _params(n_axes, vmem=V7X_VMEM_LIMIT):
    return pltpu.CompilerParams(
        dimension_semantics=("arbitrary",) * n_axes, vmem_limit_bytes=vmem)


def _sigmoid(x):
    return 0.5 * jnp.tanh(0.5 * x) + 0.5


def _layer_norm(z, g, b):
    mu = jnp.mean(z, axis=-1, keepdims=True)
    zc = z - mu
    var = jnp.mean(zc * zc, axis=-1, keepdims=True)
    return zc * lax.rsqrt(var + LN_EPS) * g + b


def _rows_from_tiles(ref, n_sub, first=0, rows=None, dtype=None):
    if ref.shape[1] == n_sub * LANES:
        return ref[...] if dtype is None else ref[...].astype(dtype)
    rows = ref.shape[0] // n_sub if rows is None else rows
    cols = [ref[pl.ds(first + s, rows, stride=n_sub), :] for s in range(n_sub)]
    if dtype is not None:
        cols = [c.astype(dtype) for c in cols]
    return jnp.concatenate(cols, axis=1)


def _rows_to_tiles(ref, value, first=0):
    n_sub = value.shape[1] // LANES
    for s in range(n_sub):
        ref[pl.ds(first + s, value.shape[0], stride=n_sub), :] = value[:, s * LANES:(s + 1) * LANES]


def _matmul_kernel(x_ref, w_ref, o_ref):
    x = _rows_from_tiles(x_ref, w_ref.shape[0] // LANES, dtype=BF16)
    o_ref[...] = jnp.dot(x, w_ref[...], preferred_element_type=F32).astype(o_ref.dtype)


def _x_block(x, seq, d, m, tm):
    if x.ndim == 3:
        return pl.BlockSpec((None, tm, d), lambda i: (seq, i, 0))
    return pl.BlockSpec((tm * (d // LANES), LANES), lambda i: (i, 0))


def _layer_block(w, layer):
    return pl.BlockSpec((None,) + w.shape[1:], lambda i: (layer, 0, 0))


def _matmul(x, seq, m, w, layer, out_dtype):
    k, n = w.shape[1:]
    tm = min(TOKEN_TILE, m)
    return pl.pallas_call(
        _matmul_kernel,
        grid=(m // tm,),
        in_specs=[_x_block(x, seq, k, m, tm), _layer_block(w, layer)],
        out_specs=pl.BlockSpec((tm, n), lambda i: (i, 0)),
        out_shape=jax.ShapeDtypeStruct((m, n), out_dtype),
        compiler_params=_params(1),
        name="dense_proj",
    )(x, w)


def _attn_kernel(q_ref, k_ref, v_ref, bself_ref, badj_ref, bfar_ref, o_ref,
                 kmean_ref, vt_ref, pen_ref, m_ref, l_ref, acc_ref, sa_ref, sb_ref, *, nblk, scale):
    blk = MOBA_BLOCK
    hg = bself_ref.shape[0]
    dh = q_ref.shape[1] // hg
    i = pl.program_id(1)

    def head(ref, h):
        return ref.at[:, h * dh:(h + 1) * dh]

    @pl.when(i == 0)
    def _():
        for h in range(hg):
            for j in range(nblk):
                kj = head(k_ref, h)[j * blk:(j + 1) * blk, :].astype(F32)
                kmean_ref[h, j:j + 1, :] = jnp.sum(kj, axis=0, keepdims=True) * (1.0 / blk)
                vt_ref[h, j] = head(v_ref, h)[j * blk:(j + 1) * blk, :].astype(F32).T.astype(BF16)

    qs = []
    for h in range(hg):
        q = head(q_ref, h)[...]

        km = kmean_ref[h]
        km_hi = km.astype(BF16)
        km_lo = (km - km_hi.astype(F32)).astype(BF16)
        gate = (lax.dot_general(km_hi, q, _NT, preferred_element_type=F32)
                + lax.dot_general(km_lo, q, _NT, preferred_element_type=F32))
        jj = lax.broadcasted_iota(I32, gate.shape, 0)
        past = jj < i
        g = jnp.where(past, gate, -jnp.inf)
        sel = jnp.zeros(gate.shape, jnp.bool_)
        for _ in range(min(MOBA_TOPK, nblk)):
            mx = jnp.max(g, axis=0, keepdims=True)
            jm = jnp.min(jnp.where(g == mx, jj, nblk), axis=0, keepdims=True)
            pick = jj == jm
            sel = jnp.logical_or(sel, pick)
            g = jnp.where(pick, -jnp.inf, g)
        pen_ref[h] = jnp.where(jnp.logical_and(sel, past), 0.0, -jnp.inf)
        qs.append((q.astype(F32) * (scale * LOG2_E)).astype(BF16))

    def scores(h, j):
        kb = head(k_ref, h)[pl.ds(pl.multiple_of(j * blk, blk), blk), :]
        return lax.dot_general(kb, qs[h], _NT, preferred_element_type=F32)

    def fold(x, op):
        return op(x.reshape(blk // SUBLANES, SUBLANES, blk), axis=0)

    def adj_bias(h):
        return badj_ref[h] + pen_ref[h, pl.ds(jnp.maximum(i - 1, 0), 1), :]

    def far_bias(h, j):
        return bfar_ref[h, 0:1, 0:1] + pen_ref[h, pl.ds(j, 1), :]

    n_far = jnp.maximum(i - 1, 0)


    prev_blk = jnp.maximum(i - 1, 0)
    ss = [scores(h, i) + bself_ref[h] for h in range(hg)]
    for h in range(hg):
        sb_ref[h] = scores(h, prev_blk)
    for h in range(hg):
        sa_ref[h] = scores(h, 0)

    for h in range(hg):
        m0 = jnp.max(fold(ss[h], jnp.max), axis=0, keepdims=True)
        p = jnp.exp2(ss[h] - m0)
        m_ref[h] = m0
        l_ref[h] = fold(p, jnp.sum)
        acc_ref[h] = jnp.dot(vt_ref[h, i], p.astype(BF16), preferred_element_type=F32)

    def update(h, j, s, s_max, bias):
        if bias is not None:
            s_max = s_max + bias
        m_old = m_ref[h]
        m_new = jnp.maximum(m_old, jnp.max(s_max, axis=0, keepdims=True))
        alpha = jnp.exp2(m_old - m_new)
        p = jnp.exp2(s - m_new if bias is None else s + (bias - m_new))
        m_ref[h] = m_new
        l_ref[h] = alpha * l_ref[h] + fold(p, jnp.sum)
        acc_ref[h] = alpha * acc_ref[h] + jnp.dot(
            vt_ref[h, j], p.astype(BF16), preferred_element_type=F32)

    for h in range(hg):
        s = sb_ref[h] + adj_bias(h)
        update(h, prev_blk, s, fold(s, jnp.max), None)

    def stage(j, cur_ref, nxt_ref):
        nxt = jnp.minimum(j + 1, n_far - 1)
        for h in range(hg):
            nxt_ref[h] = scores(h, nxt)
        for h in range(hg):
            s = cur_ref[h]
            update(h, j, s, fold(s, jnp.max), far_bias(h, j))

    def far_pair(p, carry):
        stage(2 * p, sa_ref, sb_ref)

        @pl.when(2 * p + 1 < n_far)
        def _():
            stage(2 * p + 1, sb_ref, sa_ref)

        return carry

    lax.fori_loop(0, (n_far + 1) // 2, far_pair, 0)

    for h in range(hg):
        o = acc_ref[h] / jnp.sum(l_ref[h], axis=0, keepdims=True)
        head(o_ref, h)[...] = o.T.astype(o_ref.dtype)


def _t5_bucket_np(dist):
    max_exact = NUM_BUCKETS // 2
    n = dist.astype(np.float32)
    large = max_exact + (np.log(np.maximum(n, np.float32(1.0)) / np.float32(max_exact))
                         / np.float32(math.log(REL_MAX_DISTANCE / max_exact))
                         * np.float32(NUM_BUCKETS - max_exact)).astype(np.int32)
    large = np.minimum(large, NUM_BUCKETS - 1)
    return np.where(dist < max_exact, dist, large)


def _t5_bucket(dist):
    max_exact = NUM_BUCKETS // 2
    n = dist.astype(F32)
    large = max_exact + (jnp.log(jnp.maximum(n, 1.0) / max_exact)
                         / math.log(REL_MAX_DISTANCE / max_exact)
                         * (NUM_BUCKETS - max_exact)).astype(I32)
    large = jnp.minimum(large, NUM_BUCKETS - 1)
    return jnp.where(dist < max_exact, dist, large)


def _attn_biases(rel_bias_table, t):
    blk = MOBA_BLOCK
    far_buckets = _t5_bucket_np(np.arange(blk + 1, max(t, 2 * blk) + 1, dtype=np.int32))
    assert (far_buckets == NUM_BUCKETS - 1).all()
    bbd = rel_bias_table.astype(F32)[_t5_bucket(jnp.arange(2 * blk + 1, dtype=I32))].T * LOG2_E
    n_heads = bbd.shape[0]

    def toeplitz(w):
        rep = jnp.tile(w, (1, blk))[:, :blk * (2 * blk - 1)].reshape(n_heads, blk, 2 * blk - 1)
        return rep[:, :, blk - 1:]

    w_self = jnp.concatenate([jnp.full((n_heads, blk - 1), -jnp.inf, F32), bbd[:, :blk + 1]], axis=1)
    w_adj = jnp.concatenate([bbd[:, 1:2 * blk], bbd[:, :1]], axis=1)
    bself = toeplitz(w_self)
    badj = toeplitz(w_adj)
    bfar = jnp.broadcast_to(bbd[:, 2 * blk][:, None, None], (n_heads, SUBLANES, LANES))
    return bself, badj, bfar


def _moba_attention(qkv, biases):
    t, d3 = qkv.shape
    d = d3 // 3
    dh = d // N_HEADS
    blk = MOBA_BLOCK
    nblk = t // blk
    bself, badj, bfar = biases
    hg = HEAD_GROUP
    n_grp = N_HEADS // hg
    kernel = functools.partial(_attn_kernel, nblk=nblk, scale=dh ** -0.5)
    return pl.pallas_call(
        kernel,
        grid=(n_grp, nblk),
        in_specs=[pl.BlockSpec((blk, hg * dh), lambda g, i: (i, g)),
                  pl.BlockSpec((t, hg * dh), lambda g, i: (0, n_grp + g)),
                  pl.BlockSpec((t, hg * dh), lambda g, i: (0, 2 * n_grp + g)),
                  pl.BlockSpec((hg, blk, blk), lambda g, i: (g, 0, 0)),
                  pl.BlockSpec((hg, blk, blk), lambda g, i: (g, 0, 0)),
                  pl.BlockSpec((hg, SUBLANES, LANES), lambda g, i: (g, 0, 0))],
        out_specs=pl.BlockSpec((blk, hg * dh), lambda g, i: (i, g)),
        out_shape=jax.ShapeDtypeStruct((t, d), BF16),
        scratch_shapes=[pltpu.VMEM((hg, nblk, dh), F32),
                        pltpu.VMEM((hg, nblk, dh, blk), BF16),
                        pltpu.VMEM((hg, nblk, blk), F32),
                        pltpu.VMEM((hg, 1, blk), F32),
                        pltpu.VMEM((hg, SUBLANES, blk), F32),
                        pltpu.VMEM((hg, dh, blk), F32),
                        pltpu.VMEM((hg, blk, blk), F32),
                        pltpu.VMEM((hg, blk, blk), F32)],
        compiler_params=_params(2),
        name="moba_attention",
    )(qkv, qkv, qkv, bself, badj, bfar)


def _proj_ln_router_kernel(a_ref, w_ref, x_ref, g_ref, b_ref, wrh_ref, wrl_ref, br_ref,
                           xn_ref, ids_ref, gates_ref, rank_ref, counts_ref, base_ref,
                           *, alpha, n_exp):
    t = pl.program_id(0)
    tm = a_ref.shape[0]

    @pl.when(t == 0)
    def _():
        base_ref[...] = jnp.zeros_like(base_ref)

    mix = jnp.dot(a_ref[...].astype(BF16), w_ref[...], preferred_element_type=F32)
    x_res = _rows_from_tiles(x_ref, w_ref.shape[1] // LANES)
    xn = _layer_norm(alpha * x_res + mix, g_ref[...], b_ref[...])
    _rows_to_tiles(xn_ref, xn)

    x_hi = xn.astype(BF16)
    x_lo = (xn - x_hi.astype(F32)).astype(BF16)
    wr_hi = wrh_ref[...]
    logits = (lax.dot_general(wr_hi, x_hi, _NT, preferred_element_type=F32)
              + lax.dot_general(wr_hi, x_lo, _NT, preferred_element_type=F32)
              + lax.dot_general(wrl_ref[...], x_hi, _NT, preferred_element_type=F32)
              + br_ref[...])

    e_iota = lax.broadcasted_iota(I32, logits.shape, 0)
    work = logits
    tops, picks = [], []
    for k in range(EXPERT_TOPK):
        mx = jnp.max(work, axis=0, keepdims=True)
        em = jnp.min(jnp.where(work == mx, e_iota, n_exp), axis=0, keepdims=True)
        pick = e_iota == em
        ids_ref[k:k + 1, :] = em
        tops.append(mx)
        picks.append(pick)
        work = jnp.where(pick, -jnp.inf, work)

    ex = [jnp.exp(tk - tops[0]) for tk in tops]
    den = ex[0]
    for e in ex[1:]:
        den = den + e
    for k in range(EXPERT_TOPK):
        gates_ref[k:k + 1, :] = ex[k] / den

    onehot = jnp.zeros(logits.shape, F32)
    for pick in picks:
        onehot = onehot + pick.astype(F32)
    earlier = (lax.broadcasted_iota(I32, (tm, tm), 0)
               < lax.broadcasted_iota(I32, (tm, tm), 1)).astype(BF16)
    pos = jnp.dot(onehot.astype(BF16), earlier, preferred_element_type=F32) + base_ref[:, 0:1]
    for k in range(EXPERT_TOPK):
        rank_ref[k:k + 1, :] = jnp.sum(jnp.where(picks[k], pos, 0.0), axis=0,
                                       keepdims=True).astype(I32)
    total = base_ref[...] + jnp.sum(onehot, axis=1, keepdims=True)
    base_ref[...] = total
    counts_ref[...] = total


def _proj_ln_router(a, w, layer, x, seq, ln_g, ln_b, wr_hi, wr_lo, br, alpha):
    t, d = a.shape
    n_exp = wr_hi.shape[0]
    tm = min(TOKEN_TILE, t)
    kernel = functools.partial(_proj_ln_router_kernel, alpha=alpha, n_exp=n_exp)
    row = lambda i: (i, 0)
    fixed = lambda i: (0, 0)
    col = lambda i: (0, i)
    return pl.pallas_call(
        kernel,
        grid=(t // tm,),
        in_specs=[pl.BlockSpec((tm, d), row),
                  _layer_block(w, layer),
                  _x_block(x, seq, d, t, tm),
                  pl.BlockSpec((1, d), fixed),
                  pl.BlockSpec((1, d), fixed),
                  pl.BlockSpec((n_exp, d), fixed),
                  pl.BlockSpec((n_exp, d), fixed),
                  pl.BlockSpec((n_exp, 1), fixed)],
        out_specs=[pl.BlockSpec((tm * (d // LANES), LANES), row),
                   pl.BlockSpec((EXPERT_TOPK, tm), col),
                   pl.BlockSpec((EXPERT_TOPK, tm), col),
                   pl.BlockSpec((EXPERT_TOPK, tm), col),
                   pl.BlockSpec((n_exp, 128), fixed)],
        out_shape=[jax.ShapeDtypeStruct((t * (d // LANES), LANES), F32),
                   jax.ShapeDtypeStruct((EXPERT_TOPK, t), I32),
                   jax.ShapeDtypeStruct((EXPERT_TOPK, t), F32),
                   jax.ShapeDtypeStruct((EXPERT_TOPK, t), I32),
                   jax.ShapeDtypeStruct((n_exp, 128), F32)],
        scratch_shapes=[pltpu.VMEM((n_exp, 128), F32)],
        compiler_params=_params(1),
        name="proj_ln_router",
    )(a, w, x, ln_g, ln_b, wr_hi, wr_lo, br)


def _moe_kernel(dest_ref, gate_ref, start_ref, count_ref, nch_ref, last_ref,
                x2_ref, wg_ref, wu_ref, wd_ref, bg_ref, bu_ref, bd_ref, lng_ref, lnb_ref,
                out_ref, src_ref, xg_ref, yf_ref, *, n_tok, n_exp, tok_bits, ln_alpha):
    e = pl.program_id(0)
    n_sub = wg_ref.shape[0] // LANES
    tm = xg_ref.shape[0] // n_sub
    spare = COMBINE_BATCH

    def tile(ref, idx):
        return ref.at[pl.ds(pl.multiple_of(idx * n_sub, n_sub), n_sub), :]

    tok_mask = (1 << tok_bits) - 1

    def gather_row(base, r):
        tok = jnp.minimum(src_ref[base + r] & tok_mask, n_tok - 1)
        tile(xg_ref, r)[...] = tile(x2_ref, tok)[...]

    def combine_batch(base, r0):
        codes = [src_ref[base + r0 + k] for k in range(spare)]
        toks = [c & tok_mask for c in codes]
        gs = [gate_ref[c] for c in codes]
        olds = [tile(out_ref, toks[k])[...] for k in range(spare)]
        for k in range(spare):
            tile(out_ref, toks[k])[...] = olds[k] + gs[k] * tile(yf_ref, r0 + k)[...]

    @pl.when(e == 0)
    def _():
        out_ref[...] = jnp.zeros_like(out_ref)
        yf_ref[...] = jnp.zeros_like(yf_ref)

        def fill(r, carry):
            src_ref[r] = n_tok + r % spare + (EXPERT_TOPK << tok_bits)
            return carry

        for ee in range(n_exp):
            lax.fori_loop(start_ref[ee] + count_ref[ee], start_ref[ee] + nch_ref[ee] * tm, fill, 0)

        def scatter(a, carry):
            for k in range(EXPERT_TOPK):
                src_ref[dest_ref[k * n_tok + a]] = a + (k << tok_bits)
            return carry

        lax.fori_loop(0, n_tok, scatter, 0, unroll=8)

        def first_rows(r, carry):
            gather_row(0, r)
            return carry

        lax.fori_loop(0, tm, first_rows, 0, unroll=8)

    def visit(base, rows):
        x = jnp.concatenate(
            [xg_ref[pl.ds(s, rows, stride=n_sub), :].astype(BF16) for s in range(n_sub)], axis=1)

        nxt = jnp.minimum(base + tm, last_ref[0])
        for r in range(tm):
            gather_row(nxt, r)
        prev = jnp.maximum(base - tm, 0)
        for r0 in range(0, tm, spare):
            combine_batch(prev, r0)

        g = jnp.dot(x, wg_ref[...], preferred_element_type=F32) + bg_ref[...]
        u = jnp.dot(x, wu_ref[...], preferred_element_type=F32) + bu_ref[...]
        g = jnp.minimum(g, SWIGLU_LIMIT)
        u = jnp.clip(u, -SWIGLU_LIMIT, SWIGLU_LIMIT)
        h = g * _sigmoid(SWIGLU_ALPHA * g) * (u + 1.0)
        y = jnp.dot(h.astype(BF16), wd_ref[...], preferred_element_type=F32) + bd_ref[...]
        for s in range(n_sub):
            yf_ref[pl.ds(s, rows, stride=n_sub), :] = y[:, s * LANES:(s + 1) * LANES]

    def chunk(ch, carry):
        base = start_ref[e] + ch * tm
        n_real = count_ref[e] - ch * tm

        @pl.when(n_real > tm // 2)
        def _():
            visit(base, tm)

        @pl.when(n_real <= tm // 2)
        def _():
            visit(base, tm // 2)

        return carry

    lax.fori_loop(0, nch_ref[e], chunk, 0)

    @pl.when(e == n_exp - 1)
    def _():
        def last_rows(b, carry):
            combine_batch(last_ref[0], b * spare)
            return carry

        lax.fori_loop(0, tm // spare, last_rows, 0)

        ln_rows = min(EXPERT_ROWS, n_tok)

        def normalize(b, carry):
            first = pl.multiple_of(b * (ln_rows * n_sub), ln_rows * n_sub)
            z = (ln_alpha * _rows_from_tiles(x2_ref, n_sub, first, ln_rows)
                 + _rows_from_tiles(out_ref, n_sub, first, ln_rows))
            _rows_to_tiles(out_ref, _layer_norm(z, lng_ref[...], lnb_ref[...]), first)
            return carry

        lax.fori_loop(0, n_tok // ln_rows, normalize, 0)


def _moe_experts(x_tiles, t, d, tok_bits, dest, gates, start, count, nch, last, layer,
                 wg, wu, wd, bg, bu, bd, ln_g, ln_b, alpha):
    n_sub = d // LANES
    tm = EXPERT_ROWS
    n_exp, dff = wg.shape[1], wg.shape[3]
    n_rows = t * EXPERT_TOPK + n_exp * tm

    def w_map(e, *_):
        return (layer, e, 0, 0)

    out_rows = (t + COMBINE_BATCH) * n_sub
    grid_spec = pltpu.PrefetchScalarGridSpec(
        num_scalar_prefetch=6,
        grid=(n_exp,),
        in_specs=[pl.BlockSpec((t * n_sub, LANES), lambda e, *_: (0, 0)),
                  pl.BlockSpec((None, None, d, dff), w_map),
                  pl.BlockSpec((None, None, d, dff), w_map),
                  pl.BlockSpec((None, None, dff, d), w_map),
                  pl.BlockSpec((None, None, 1, dff), w_map),
                  pl.BlockSpec((None, None, 1, dff), w_map),
                  pl.BlockSpec((None, None, 1, d), w_map),
                  pl.BlockSpec((1, d), lambda e, *_: (0, 0)),
                  pl.BlockSpec((1, d), lambda e, *_: (0, 0))],
        out_specs=pl.BlockSpec((out_rows, LANES), lambda e, *_: (0, 0)),
        scratch_shapes=[pltpu.SMEM((n_rows,), I32),
                        pltpu.VMEM((tm * n_sub, LANES), F32),
                        pltpu.VMEM((tm * n_sub, LANES), F32)],
    )
    return pl.pallas_call(
        functools.partial(_moe_kernel, n_tok=t, n_exp=n_exp, tok_bits=tok_bits, ln_alpha=alpha),
        grid_spec=grid_spec,
        out_shape=jax.ShapeDtypeStruct((out_rows, LANES), F32),
        compiler_params=_params(1),
        name="moe_experts",
    )(dest, gates, start, count, nch, last, x_tiles, wg, wu, wd, bg, bu, bd, ln_g, ln_b)


def _lru_in_kernel(x_ref, w_ref, y_ref, r_ref):
    d = y_ref.shape[1]
    x = _rows_from_tiles(x_ref, w_ref.shape[0] // LANES, dtype=BF16)
    yz = jnp.dot(x, w_ref[...], preferred_element_type=F32)
    y = yz[:, :d]
    y_ref[...] = 0.5 * y * (1.0 + jnp.tanh(math.sqrt(2.0 / math.pi) * (y + 0.044715 * (y * y * y))))
    r_ref[...] = yz[:, d:]


def _lru_in(x_tiles, t, w_in, layer):
    d = w_in.shape[1]
    n = w_in.shape[2] // 2
    tm = min(TOKEN_TILE, t)
    return pl.pallas_call(
        _lru_in_kernel,
        grid=(t // tm,),
        in_specs=[_x_block(x_tiles, None, d, t, tm), _layer_block(w_in, layer)],
        out_specs=[pl.BlockSpec((tm, n), lambda i: (i, 0)),
                   pl.BlockSpec((tm, n), lambda i: (i, 0))],
        out_shape=[jax.ShapeDtypeStruct((t, n), F32), jax.ShapeDtypeStruct((t, n), F32)],
        compiler_params=_params(1),
        name="lru_in",
    )(x_tiles, w_in)


def _lru_core_kernel(r_ref, y_ref, cw_ref, cb_ref, wgx_ref, wga_ref, bgx_ref, bga_ref, ap_ref,
                     hy_ref, rprev_ref, carry_ref, a_ref, u_ref):
    t = pl.program_id(0)
    tt, d = r_ref.shape
    n_blk, bw, _ = wgx_ref.shape
    cwid = cw_ref.shape[0]

    @pl.when(t == 0)
    def _():
        rprev_ref[...] = jnp.zeros_like(rprev_ref)
        carry_ref[...] = jnp.zeros_like(carry_ref)

    r = r_ref[...]
    tail = rprev_ref.shape[0]
    rext = jnp.concatenate([rprev_ref[...], r], axis=0)
    rc = jnp.broadcast_to(cb_ref[...], (tt, d))
    for w in range(cwid):
        off = tail - (cwid - 1) + w
        rc = rc + cw_ref[w:w + 1, :] * rext[off:off + tt, :]
    rprev_ref[...] = r[tt - tail:, :]

    rcb = rc.astype(BF16)
    gx = jnp.concatenate(
        [jnp.dot(rcb[:, n * bw:(n + 1) * bw], wgx_ref[n], preferred_element_type=F32)
         for n in range(n_blk)], axis=1)
    ga = jnp.concatenate(
        [jnp.dot(rcb[:, n * bw:(n + 1) * bw], wga_ref[n], preferred_element_type=F32)
         for n in range(n_blk)], axis=1)
    gate_x = _sigmoid(gx + bgx_ref[...])
    gate_a = _sigmoid(ga + bga_ref[...])
    z = -ap_ref[...]
    softplus = jnp.maximum(z, 0.0) + jnp.log(1.0 + jnp.exp(-jnp.abs(z)))
    log_a = -LRU_C * gate_a * softplus
    a_ref[...] = jnp.exp(log_a)
    u_ref[...] = jnp.sqrt(1.0 - jnp.exp(2.0 * log_a)) * (gate_x * rc)

    row = lax.broadcasted_iota(I32, (SCAN_ROWS, d), 0)

    def slab(s, carry):
        off = pl.multiple_of(s * SCAN_ROWS, SCAN_ROWS)
        a = a_ref[pl.ds(off, SCAN_ROWS), :]
        b = u_ref[pl.ds(off, SCAN_ROWS), :]
        step = 1
        while step < SCAN_ROWS:
            a_sh = pltpu.roll(a, step, 0)
            b_sh = pltpu.roll(b, step, 0)
            live = row >= step
            b = jnp.where(live, a * b_sh + b, b)
            a = jnp.where(live, a * a_sh, a)
            step *= 2
        h = b + a * carry
        u_ref[pl.ds(off, SCAN_ROWS), :] = h
        return h[SCAN_ROWS - 1:SCAN_ROWS, :]

    carry_ref[...] = lax.fori_loop(0, tt // SCAN_ROWS, slab, carry_ref[...])
    hy_ref[...] = (u_ref[...] * y_ref[...]).astype(hy_ref.dtype)


def _lru_core(r, y, conv_w, conv_b, wgx, wga, bgx, bga, a_param):
    t, d = r.shape
    tt = min(TOKEN_TILE, t)
    row = lambda i: (i, 0)
    fixed2 = lambda i: (0, 0)
    fixed3 = lambda i: (0, 0, 0)
    return pl.pallas_call(
        _lru_core_kernel,
        grid=(t // tt,),
        in_specs=[pl.BlockSpec((tt, d), row),
                  pl.BlockSpec((tt, d), row),
                  pl.BlockSpec(conv_w.shape, fixed2),
                  pl.BlockSpec((1, d), fixed2),
                  pl.BlockSpec(wgx.shape, fixed3),
                  pl.BlockSpec(wga.shape, fixed3),
                  pl.BlockSpec((1, d), fixed2),
                  pl.BlockSpec((1, d), fixed2),
                  pl.BlockSpec((1, d), fixed2)],
        out_specs=pl.BlockSpec((tt, d), row),
        out_shape=jax.ShapeDtypeStruct((t, d), BF16),
        scratch_shapes=[pltpu.VMEM((SUBLANES, d), F32),
                        pltpu.VMEM((1, d), F32),
                        pltpu.VMEM((tt, d), F32),
                        pltpu.VMEM((tt, d), F32)],
        compiler_params=_params(1),
        name="lru_core",
    )(r, y, conv_w, conv_b, wgx, wga, bgx, bga, a_param)


def _moe(xn_tiles, t, d, ids, gates, rank, counts128, layer, wg, wu, wd, bg, bu, bd,
         ln_g, ln_b, alpha):
    tm = EXPERT_ROWS
    n_exp = wg.shape[1]
    counts = counts128[:, 0].astype(I32)
    nch = (counts + tm - 1) // tm
    start = (jnp.cumsum(nch) - nch).astype(I32) * tm
    experts = jnp.arange(n_exp, dtype=I32)
    dest = rank + jnp.sum(jnp.where(ids[:, :, None] == experts, start, 0), axis=-1)
    last = ((jnp.sum(nch) - 1) * tm).astype(I32).reshape(1)
    tok_bits = (t + COMBINE_BATCH - 1).bit_length()
    gates_ext = jnp.pad(gates, ((0, 1), (0, (1 << tok_bits) - t))).reshape(-1)
    return _moe_experts(xn_tiles, t, d, tok_bits, dest.reshape(-1).astype(I32), gates_ext,
                        start, counts, nch.astype(I32), last, layer, wg, wu, wd, bg, bu, bd,
                        ln_g, ln_b, alpha)


def kernel(x, rel_bias_table, w_qkv, w_attn_out, w_lru_in, lru_conv_w, lru_conv_b, w_lru_gate_x, b_lru_gate_x, w_lru_gate_a, b_lru_gate_a, lru_a_param, w_lru_out, w_router, b_router, w_exp_gate, b_exp_gate, w_exp_up, b_exp_up, w_exp_down, b_exp_down, ln_mix_g, ln_mix_b, ln_ffn_g, ln_ffn_b):
    bsz, t, d = x.shape
    depth = w_router.shape[0]
    n_mixers = 2
    alpha = (2 * depth) ** 0.25
    assert t % MOBA_BLOCK == 0 and t % TOKEN_TILE == 0

    biases = _attn_biases(rel_bias_table, t)
    w_qkv_b = w_qkv.astype(BF16)
    w_attn_out_b = w_attn_out.astype(BF16)
    w_lru_in_b = w_lru_in.astype(BF16)
    w_lru_out_b = w_lru_out.astype(BF16)
    wgx_b = w_lru_gate_x.astype(BF16)
    wga_b = w_lru_gate_a.astype(BF16)
    wg_b = w_exp_gate.astype(BF16)
    wu_b = w_exp_up.astype(BF16)
    wd_b = w_exp_down.astype(BF16)
    wr_t = jnp.swapaxes(w_router, 1, 2)
    wr_hi = wr_t.astype(BF16)
    wr_lo = (wr_t - wr_hi.astype(F32)).astype(BF16)

    bg4 = b_exp_gate[:, :, None, :]
    bu4 = b_exp_up[:, :, None, :]
    bd4 = b_exp_down[:, :, None, :]

    outs = []
    for b in range(bsz):
        xs = x
        for i in range(depth):
            j = i // n_mixers
            if i % n_mixers == 0:
                qkv = _matmul(xs, b, t, w_qkv_b, j, BF16)
                a = _moba_attention(qkv, biases)
                w_out = w_attn_out_b
            else:
                y, r = _lru_in(xs, t, w_lru_in_b, j)
                a = _lru_core(r, y, lru_conv_w[j], lru_conv_b[j][None], wgx_b[j], wga_b[j],
                              b_lru_gate_x[j][None], b_lru_gate_a[j][None], lru_a_param[j][None])
                w_out = w_lru_out_b
            xn, ids, gates, rank, counts = _proj_ln_router(
                a, w_out, j, xs, b, ln_mix_g[i][None], ln_mix_b[i][None],
                wr_hi[i], wr_lo[i], b_router[i][:, None], alpha)
            xs = _moe(xn, t, d, ids, gates, rank, counts, i, wg_b, wu_b, wd_b, bg4, bu4, bd4,
                      ln_ffn_g[i][None], ln_ffn_b[i][None], alpha)
        outs.append(xs.reshape(t + COMBINE_BATCH, d)[:t])
    return jnp.stack(outs, axis=0)
```

```python
import functools
import math

import numpy as np
import jax
import jax.numpy as jnp
from jax import lax
from jax.experimental import pallas as pl
from jax.experimental.pallas import tpu as pltpu

F32 = jnp.float32
BF16 = jnp.bfloat16
I32 = jnp.int32

N_HEADS = 8
MOBA_BLOCK = 256
MOBA_TOPK = 3
NUM_BUCKETS = 32
REL_MAX_DISTANCE = 128
LRU_C = 8.0
EXPERT_TOPK = 4
SWIGLU_LIMIT = 7.0
SWIGLU_ALPHA = 1.702
LN_EPS = 1e-5
LOG2_E = 1.4426950408889634

EXPERT_ROWS = 256
TOKEN_TILE = 512
LANES = 128
SUBLANES = 8
SCAN_ROWS = SUBLANES
HEAD_GROUP = 8
COMBINE_BATCH = 8
V7X_VMEM_LIMIT = 56 * 1024 * 1024

_NT = (((1,), (1,)), ((), ()))


def _params(n_axes, vmem=V7X_VMEM_LIMIT):
    return pltpu.CompilerParams(
        dimension_semantics=("arbitrary",) * n_axes, vmem_limit_bytes=vmem)


def _sigmoid(x):
    return 0.5 * jnp.tanh(0.5 * x) + 0.5


def _layer_norm(z, g, b):
    mu = jnp.mean(z, axis=-1, keepdims=True)
    zc = z - mu
    var = jnp.mean(zc * zc, axis=-1, keepdims=True)
    return zc * lax.rsqrt(var + LN_EPS) * g + b


def _rows_from_tiles(ref, n_sub, first=0, rows=None, dtype=None):
    if ref.shape[1] == n_sub * LANES:
        return ref[...] if dtype is None else ref[...].astype(dtype)
    rows = ref.shape[0] // n_sub if rows is None else rows
    cols = [ref[pl.ds(first + s, rows, stride=n_sub), :] for s in range(n_sub)]
    if dtype is not None:
        cols = [c.astype(dtype) for c in cols]
    return jnp.concatenate(cols, axis=1)


def _rows_to_tiles(ref, value, first=0):
    n_sub = value.shape[1] // LANES
    for s in range(n_sub):
        ref[pl.ds(first + s, value.shape[0], stride=n_sub), :] = value[:, s * LANES:(s + 1) * LANES]


def _matmul_kernel(x_ref, w_ref, o_ref):
    x = _rows_from_tiles(x_ref, w_ref.shape[0] // LANES, dtype=BF16)
    o_ref[...] = jnp.dot(x, w_ref[...], preferred_element_type=F32).astype(o_ref.dtype)


def _x_block(x, seq, d, m, tm):
    if x.ndim == 3:
        return pl.BlockSpec((None, tm, d), lambda i: (seq, i, 0))
    return pl.BlockSpec((tm * (d // LANES), LANES), lambda i: (i, 0))


def _layer_block(w, layer):
    return pl.BlockSpec((None,) + w.shape[1:], lambda i: (layer, 0, 0))


def _matmul(x, seq, m, w, layer, out_dtype):
    k, n = w.shape[1:]
    tm = min(TOKEN_TILE, m)
    return pl.pallas_call(
        _matmul_kernel,
        grid=(m // tm,),
        in_specs=[_x_block(x, seq, k, m, tm), _layer_block(w, layer)],
        out_specs=pl.BlockSpec((tm, n), lambda i: (i, 0)),
        out_shape=jax.ShapeDtypeStruct((m, n), out_dtype),
        compiler_params=_params(1),
        name="dense_proj",
    )(x, w)


def _attn_kernel(q_ref, k_ref, v_ref, bself_ref, badj_ref, bfar_ref, o_ref,
                 kmean_ref, vt_ref, pen_ref, m_ref, l_ref, acc_ref, sa_ref, sb_ref, *, nblk, scale):
    blk = MOBA_BLOCK
    hg = bself_ref.shape[0]
    dh = q_ref.shape[1] // hg
    i = pl.program_id(1)

    def head(ref, h):
        return ref.at[:, h * dh:(h + 1) * dh]

    @pl.when(i == 0)
    def _():
        for h in range(hg):
            for j in range(nblk):
                kj = head(k_ref, h)[j * blk:(j + 1) * blk, :].astype(F32)
                kmean_ref[h, j:j + 1, :] = jnp.sum(kj, axis=0, keepdims=True) * (1.0 / blk)
                vt_ref[h, j] = head(v_ref, h)[j * blk:(j + 1) * blk, :].astype(F32).T.astype(BF16)

    qs = []
    for h in range(hg):
        q = head(q_ref, h)[...]

        km = kmean_ref[h]
        km_hi = km.astype(BF16)
        km_lo = (km - km_hi.astype(F32)).astype(BF16)
        gate = (lax.dot_general(km_hi, q, _NT, preferred_element_type=F32)
                + lax.dot_general(km_lo, q, _NT, preferred_element_type=F32))
        jj = lax.broadcasted_iota(I32, gate.shape, 0)
        past = jj < i
        g = jnp.where(past, gate, -jnp.inf)
        sel = jnp.zeros(gate.shape, jnp.bool_)
        for _ in range(min(MOBA_TOPK, nblk)):
            mx = jnp.max(g, axis=0, keepdims=True)
            jm = jnp.min(jnp.where(g == mx, jj, nblk), axis=0, keepdims=True)
            pick = jj == jm
            sel = jnp.logical_or(sel, pick)
            g = jnp.where(pick, -jnp.inf, g)
        pen_ref[h] = jnp.where(jnp.logical_and(sel, past), 0.0, -jnp.inf)
        qs.append((q.astype(F32) * (scale * LOG2_E)).astype(BF16))

    def scores(h, j):
        kb = head(k_ref, h)[pl.ds(pl.multiple_of(j * blk, blk), blk), :]
        return lax.dot_general(kb, qs[h], _NT, preferred_element_type=F32)

    def fold(x, op):
        return op(x.reshape(blk // SUBLANES, SUBLANES, blk), axis=0)

    def adj_bias(h):
        return badj_ref[h] + pen_ref[h, pl.ds(jnp.maximum(i - 1, 0), 1), :]

    def far_bias(h, j):
        return bfar_ref[h, 0:1, 0:1] + pen_ref[h, pl.ds(j, 1), :]

    n_far = jnp.maximum(i - 1, 0)


    prev_blk = jnp.maximum(i - 1, 0)
    ss = [scores(h, i) + bself_ref[h] for h in range(hg)]
    for h in range(hg):
        sb_ref[h] = scores(h, prev_blk)
    for h in range(hg):
        sa_ref[h] = scores(h, 0)

    for h in range(hg):
        m0 = jnp.max(fold(ss[h], jnp.max), axis=0, keepdims=True)
        p = jnp.exp2(ss[h] - m0)
        m_ref[h] = m0
        l_ref[h] = fold(p, jnp.sum)
        acc_ref[h] = jnp.dot(vt_ref[h, i], p.astype(BF16), preferred_element_type=F32)

    def update(h, j, s, s_max, bias):
        if bias is not None:
            s_max = s_max + bias
        m_old = m_ref[h]
        m_new = jnp.maximum(m_old, jnp.max(s_max, axis=0, keepdims=True))
        alpha = jnp.exp2(m_old - m_new)
        p = jnp.exp2(s - m_new if bias is None else s + (bias - m_new))
        m_ref[h] = m_new
        l_ref[h] = alpha * l_ref[h] + fold(p, jnp.sum)
        acc_ref[h] = alpha * acc_ref[h] + jnp.dot(
            vt_ref[h, j], p.astype(BF16), preferred_element_type=F32)

    for h in range(hg):
        s = sb_ref[h] + adj_bias(h)
        update(h, prev_blk, s, fold(s, jnp.max), None)

    def stage(j, cur_ref, nxt_ref):
        nxt = jnp.minimum(j + 1, n_far - 1)
        for h in range(hg):
            nxt_ref[h] = scores(h, nxt)
        for h in range(hg):
            s = cur_ref[h]
            update(h, j, s, fold(s, jnp.max), far_bias(h, j))

    def far_pair(p, carry):
        stage(2 * p, sa_ref, sb_ref)

        @pl.when(2 * p + 1 < n_far)
        def _():
            stage(2 * p + 1, sb_ref, sa_ref)

        return carry

    lax.fori_loop(0, (n_far + 1) // 2, far_pair, 0)

    for h in range(hg):
        o = acc_ref[h] / jnp.sum(l_ref[h], axis=0, keepdims=True)
        head(o_ref, h)[...] = o.T.astype(o_ref.dtype)


def _t5_bucket_np(dist):
    max_exact = NUM_BUCKETS // 2
    n = dist.astype(np.float32)
    large = max_exact + (np.log(np.maximum(n, np.float32(1.0)) / np.float32(max_exact))
                         / np.float32(math.log(REL_MAX_DISTANCE / max_exact))
                         * np.float32(NUM_BUCKETS - max_exact)).astype(np.int32)
    large = np.minimum(large, NUM_BUCKETS - 1)
    return np.where(dist < max_exact, dist, large)


def _t5_bucket(dist):
    max_exact = NUM_BUCKETS // 2
    n = dist.astype(F32)
    large = max_exact + (jnp.log(jnp.maximum(n, 1.0) / max_exact)
                         / math.log(REL_MAX_DISTANCE / max_exact)
                         * (NUM_BUCKETS - max_exact)).astype(I32)
    large = jnp.minimum(large, NUM_BUCKETS - 1)
    return jnp.where(dist < max_exact, dist, large)


def _attn_biases(rel_bias_table, t):
    blk = MOBA_BLOCK
    far_buckets = _t5_bucket_np(np.arange(blk + 1, max(t, 2 * blk) + 1, dtype=np.int32))
    assert (far_buckets == NUM_BUCKETS - 1).all()
    bbd = rel_bias_table.astype(F32)[_t5_bucket(jnp.arange(2 * blk + 1, dtype=I32))].T * LOG2_E
    n_heads = bbd.shape[0]

    def toeplitz(w):
        rep = jnp.tile(w, (1, blk))[:, :blk * (2 * blk - 1)].reshape(n_heads, blk, 2 * blk - 1)
        return rep[:, :, blk - 1:]

    w_self = jnp.concatenate([jnp.full((n_heads, blk - 1), -jnp.inf, F32), bbd[:, :blk + 1]], axis=1)
    w_adj = jnp.concatenate([bbd[:, 1:2 * blk], bbd[:, :1]], axis=1)
    bself = toeplitz(w_self)
    badj = toeplitz(w_adj)
    bfar = jnp.broadcast_to(bbd[:, 2 * blk][:, None, None], (n_heads, SUBLANES, LANES))
    return bself, badj, bfar


def _moba_attention(qkv, biases):
    t, d3 = qkv.shape
    d = d3 // 3
    dh = d // N_HEADS
    blk = MOBA_BLOCK
    nblk = t // blk
    bself, badj, bfar = biases
    hg = HEAD_GROUP
    n_grp = N_HEADS // hg
    kernel = functools.partial(_attn_kernel, nblk=nblk, scale=dh ** -0.5)
    return pl.pallas_call(
        kernel,
        grid=(n_grp, nblk),
        in_specs=[pl.BlockSpec((blk, hg * dh), lambda g, i: (i, g)),
                  pl.BlockSpec((t, hg * dh), lambda g, i: (0, n_grp + g)),
                  pl.BlockSpec((t, hg * dh), lambda g, i: (0, 2 * n_grp + g)),
                  pl.BlockSpec((hg, blk, blk), lambda g, i: (g, 0, 0)),
                  pl.BlockSpec((hg, blk, blk), lambda g, i: (g, 0, 0)),
                  pl.BlockSpec((hg, SUBLANES, LANES), lambda g, i: (g, 0, 0))],
        out_specs=pl.BlockSpec((blk, hg * dh), lambda g, i: (i, g)),
        out_shape=jax.ShapeDtypeStruct((t, d), BF16),
        scratch_shapes=[pltpu.VMEM((hg, nblk, dh), F32),
                        pltpu.VMEM((hg, nblk, dh, blk), BF16),
                        pltpu.VMEM((hg, nblk, blk), F32),
                        pltpu.VMEM((hg, 1, blk), F32),
                        pltpu.VMEM((hg, SUBLANES, blk), F32),
                        pltpu.VMEM((hg, dh, blk), F32),
                        pltpu.VMEM((hg, blk, blk), F32),
                        pltpu.VMEM((hg, blk, blk), F32)],
        compiler_params=_params(2),
        name="moba_attention",
    )(qkv, qkv, qkv, bself, badj, bfar)


def _proj_ln_router_kernel(a_ref, w_ref, x_ref, g_ref, b_ref, wrh_ref, wrl_ref, br_ref,
                           earlier_ref, xn_ref, ids_ref, gates_ref, rank_ref, counts_ref, base_ref,
                           *, alpha, n_exp):
    t = pl.program_id(0)
    tm = a_ref.shape[0]

    @pl.when(t == 0)
    def _():
        base_ref[...] = jnp.zeros_like(base_ref)

    mix = jnp.dot(a_ref[...].astype(BF16), w_ref[...], preferred_element_type=F32)
    x_res = _rows_from_tiles(x_ref, w_ref.shape[1] // LANES)
    xn = _layer_norm(alpha * x_res + mix, g_ref[...], b_ref[...])
    _rows_to_tiles(xn_ref, xn)

    x_hi = xn.astype(BF16)
    x_lo = (xn - x_hi.astype(F32)).astype(BF16)
    wr_hi = wrh_ref[...]
    logits = (lax.dot_general(wr_hi, x_hi, _NT, preferred_element_type=F32)
              + lax.dot_general(wr_hi, x_lo, _NT, preferred_element_type=F32)
              + lax.dot_general(wrl_ref[...], x_hi, _NT, preferred_element_type=F32)
              + br_ref[...])

    e_iota = lax.broadcasted_iota(I32, logits.shape, 0)
    work = logits
    tops, picks = [], []
    for k in range(EXPERT_TOPK):
        mx = jnp.max(work, axis=0, keepdims=True)
        em = jnp.min(jnp.where(work == mx, e_iota, n_exp), axis=0, keepdims=True)
        pick = e_iota == em
        ids_ref[k:k + 1, :] = em
        tops.append(mx)
        picks.append(pick)
        work = jnp.where(pick, -jnp.inf, work)

    ex = [jnp.exp(tk - tops[0]) for tk in tops]
    den = ex[0]
    for e in ex[1:]:
        den = den + e
    for k in range(EXPERT_TOPK):
        gates_ref[k:k + 1, :] = ex[k] / den

    onehot = jnp.zeros(logits.shape, F32)
    for pick in picks:
        onehot = onehot + pick.astype(F32)
    pos = (jnp.dot(onehot.astype(BF16), earlier_ref[...], preferred_element_type=F32)
           + base_ref[:, 0:1])
    for k in range(EXPERT_TOPK):
        rank_ref[k:k + 1, :] = jnp.sum(jnp.where(picks[k], pos, 0.0), axis=0,
                                       keepdims=True).astype(I32)
    total = base_ref[...] + jnp.sum(onehot, axis=1, keepdims=True)
    base_ref[...] = total
    counts_ref[...] = total


def _proj_ln_router(a, w, layer, x, seq, ln_g, ln_b, wr_hi, wr_lo, br, alpha):
    t, d = a.shape
    n_exp = wr_hi.shape[0]
    tm = min(TOKEN_TILE, t)
    kernel = functools.partial(_proj_ln_router_kernel, alpha=alpha, n_exp=n_exp)
    order = jnp.arange(tm, dtype=I32)
    earlier = (order[:, None] < order[None, :]).astype(BF16)
    row = lambda i: (i, 0)
    fixed = lambda i: (0, 0)
    col = lambda i: (0, i)
    return pl.pallas_call(
        kernel,
        grid=(t // tm,),
        in_specs=[pl.BlockSpec((tm, d), row),
                  _layer_block(w, layer),
                  _x_block(x, seq, d, t, tm),
                  pl.BlockSpec((1, d), fixed),
                  pl.BlockSpec((1, d), fixed),
                  pl.BlockSpec((n_exp, d), fixed),
                  pl.BlockSpec((n_exp, d), fixed),
                  pl.BlockSpec((n_exp, 1), fixed),
                  pl.BlockSpec((tm, tm), fixed)],
        out_specs=[pl.BlockSpec((tm * (d // LANES), LANES), row),
                   pl.BlockSpec((EXPERT_TOPK, tm), col),
                   pl.BlockSpec((EXPERT_TOPK, tm), col),
                   pl.BlockSpec((EXPERT_TOPK, tm), col),
                   pl.BlockSpec((n_exp, 128), fixed)],
        out_shape=[jax.ShapeDtypeStruct((t * (d // LANES), LANES), F32),
                   jax.ShapeDtypeStruct((EXPERT_TOPK, t), I32),
                   jax.ShapeDtypeStruct((EXPERT_TOPK, t), F32),
                   jax.ShapeDtypeStruct((EXPERT_TOPK, t), I32),
                   jax.ShapeDtypeStruct((n_exp, 128), F32)],
        scratch_shapes=[pltpu.VMEM((n_exp, 128), F32)],
        compiler_params=_params(1),
        name="proj_ln_router",
    )(a, w, x, ln_g, ln_b, wr_hi, wr_lo, br, earlier)


def _moe_kernel(dest_ref, gate_ref, start_ref, count_ref, nch_ref, last_ref,
                x2_ref, wg_ref, wu_ref, wd_ref, bg_ref, bu_ref, bd_ref, lng_ref, lnb_ref,
                out_ref, src_ref, xg_ref, yf_ref, *, n_tok, n_exp, tok_bits, ln_alpha):
    e = pl.program_id(0)
    n_sub = wg_ref.shape[0] // LANES
    tm = xg_ref.shape[0] // n_sub
    spare = COMBINE_BATCH

    def tile(ref, idx):
        return ref.at[pl.ds(pl.multiple_of(idx * n_sub, n_sub), n_sub), :]

    tok_mask = (1 << tok_bits) - 1

    def gather_row(base, r):
        tok = jnp.minimum(src_ref[base + r] & tok_mask, n_tok - 1)
        tile(xg_ref, r)[...] = tile(x2_ref, tok)[...]

    def combine_batch(base, r0):
        codes = [src_ref[base + r0 + k] for k in range(spare)]
        toks = [c & tok_mask for c in codes]
        gs = [gate_ref[c] for c in codes]
        olds = [tile(out_ref, toks[k])[...] for k in range(spare)]
        for k in range(spare):
            tile(out_ref, toks[k])[...] = olds[k] + gs[k] * tile(yf_ref, r0 + k)[...]

    @pl.when(e == 0)
    def _():
        out_ref[...] = jnp.zeros_like(out_ref)
        yf_ref[...] = jnp.zeros_like(yf_ref)

        def fill(r, carry):
            src_ref[r] = n_tok + r % spare + (EXPERT_TOPK << tok_bits)
            return carry

        for ee in range(n_exp):
            lax.fori_loop(start_ref[ee] + count_ref[ee], start_ref[ee] + nch_ref[ee] * tm, fill, 0)

        def scatter(a, carry):
            for k in range(EXPERT_TOPK):
                src_ref[dest_ref[k * n_tok + a]] = a + (k << tok_bits)
            return carry

        lax.fori_loop(0, n_tok, scatter, 0, unroll=8)

        def first_rows(r, carry):
            gather_row(0, r)
            return carry

        lax.fori_loop(0, tm, first_rows, 0, unroll=8)

    def visit(base, rows):
        x = jnp.concatenate(
            [xg_ref[pl.ds(s, rows, stride=n_sub), :].astype(BF16) for s in range(n_sub)], axis=1)

        nxt = jnp.minimum(base + tm, last_ref[0])
        for r in range(tm):
            gather_row(nxt, r)
        prev = jnp.maximum(base - tm, 0)
        for r0 in range(0, tm, spare):
            combine_batch(prev, r0)

        g = jnp.dot(x, wg_ref[...], preferred_element_type=F32) + bg_ref[...]
        u = jnp.dot(x, wu_ref[...], preferred_element_type=F32) + bu_ref[...]
        g = jnp.minimum(g, SWIGLU_LIMIT)
        u = jnp.clip(u, -SWIGLU_LIMIT, SWIGLU_LIMIT)
        h = g * _sigmoid(SWIGLU_ALPHA * g) * (u + 1.0)
        y = jnp.dot(h.astype(BF16), wd_ref[...], preferred_element_type=F32) + bd_ref[...]
        for s in range(n_sub):
            yf_ref[pl.ds(s, rows, stride=n_sub), :] = y[:, s * LANES:(s + 1) * LANES]

    def chunk(ch, carry):
        base = start_ref[e] + ch * tm
        n_real = count_ref[e] - ch * tm

        @pl.when(n_real > tm // 2)
        def _():
            visit(base, tm)

        @pl.when(n_real <= tm // 2)
        def _():
            visit(base, tm // 2)

        return carry

    lax.fori_loop(0, nch_ref[e], chunk, 0)

    @pl.when(e == n_exp - 1)
    def _():
        def last_rows(b, carry):
            combine_batch(last_ref[0], b * spare)
            return carry

        lax.fori_loop(0, tm // spare, last_rows, 0)

        ln_rows = min(EXPERT_ROWS, n_tok)

        def normalize(b, carry):
            first = pl.multiple_of(b * (ln_rows * n_sub), ln_rows * n_sub)
            z = (ln_alpha * _rows_from_tiles(x2_ref, n_sub, first, ln_rows)
                 + _rows_from_tiles(out_ref, n_sub, first, ln_rows))
            _rows_to_tiles(out_ref, _layer_norm(z, lng_ref[...], lnb_ref[...]), first)
            return carry

        lax.fori_loop(0, n_tok // ln_rows, normalize, 0)


def _moe_experts(x_tiles, t, d, tok_bits, dest, gates, start, count, nch, last, layer,
                 wg, wu, wd, bg, bu, bd, ln_g, ln_b, alpha):
    n_sub = d // LANES
    tm = EXPERT_ROWS
    n_exp, dff = wg.shape[1], wg.shape[3]
    n_rows = t * EXPERT_TOPK + n_exp * tm

    def w_map(e, *_):
        return (layer, e, 0, 0)

    out_rows = (t + COMBINE_BATCH) * n_sub
    grid_spec = pltpu.PrefetchScalarGridSpec(
        num_scalar_prefetch=6,
        grid=(n_exp,),
        in_specs=[pl.BlockSpec((t * n_sub, LANES), lambda e, *_: (0, 0)),
                  pl.BlockSpec((None, None, d, dff), w_map),
                  pl.BlockSpec((None, None, d, dff), w_map),
                  pl.BlockSpec((None, None, dff, d), w_map),
                  pl.BlockSpec((None, None, 1, dff), w_map),
                  pl.BlockSpec((None, None, 1, dff), w_map),
                  pl.BlockSpec((None, None, 1, d), w_map),
                  pl.BlockSpec((1, d), lambda e, *_: (0, 0)),
                  pl.BlockSpec((1, d), lambda e, *_: (0, 0))],
        out_specs=pl.BlockSpec((out_rows, LANES), lambda e, *_: (0, 0)),
        scratch_shapes=[pltpu.SMEM((n_rows,), I32),
                        pltpu.VMEM((tm * n_sub, LANES), F32),
                        pltpu.VMEM((tm * n_sub, LANES), F32)],
    )
    return pl.pallas_call(
        functools.partial(_moe_kernel, n_tok=t, n_exp=n_exp, tok_bits=tok_bits, ln_alpha=alpha),
        grid_spec=grid_spec,
        out_shape=jax.ShapeDtypeStruct((out_rows, LANES), F32),
        compiler_params=_params(1),
        name="moe_experts",
    )(dest, gates, start, count, nch, last, x_tiles, wg, wu, wd, bg, bu, bd, ln_g, ln_b)


def _lru_in_kernel(x_ref, w_ref, y_ref, r_ref):
    d = y_ref.shape[1]
    x = _rows_from_tiles(x_ref, w_ref.shape[0] // LANES, dtype=BF16)
    yz = jnp.dot(x, w_ref[...], preferred_element_type=F32)
    y = yz[:, :d]
    y_ref[...] = 0.5 * y * (1.0 + jnp.tanh(math.sqrt(2.0 / math.pi) * (y + 0.044715 * (y * y * y))))
    r_ref[...] = yz[:, d:]


def _lru_in(x_tiles, t, w_in, layer):
    d = w_in.shape[1]
    n = w_in.shape[2] // 2
    tm = min(TOKEN_TILE, t)
    return pl.pallas_call(
        _lru_in_kernel,
        grid=(t // tm,),
        in_specs=[_x_block(x_tiles, None, d, t, tm), _layer_block(w_in, layer)],
        out_specs=[pl.BlockSpec((tm, n), lambda i: (i, 0)),
                   pl.BlockSpec((tm, n), lambda i: (i, 0))],
        out_shape=[jax.ShapeDtypeStruct((t, n), F32), jax.ShapeDtypeStruct((t, n), F32)],
        compiler_params=_params(1),
        name="lru_in",
    )(x_tiles, w_in)


def _lru_core_kernel(r_ref, y_ref, cw_ref, cb_ref, wgx_ref, wga_ref, bgx_ref, bga_ref, ap_ref,
                     hy_ref, rprev_ref, carry_ref, a_ref, u_ref):
    t = pl.program_id(0)
    tt, d = r_ref.shape
    n_blk, bw, _ = wgx_ref.shape
    cwid = cw_ref.shape[0]

    @pl.when(t == 0)
    def _():
        rprev_ref[...] = jnp.zeros_like(rprev_ref)
        carry_ref[...] = jnp.zeros_like(carry_ref)

    r = r_ref[...]
    tail = rprev_ref.shape[0]
    rext = jnp.concatenate([rprev_ref[...], r], axis=0)
    rc = jnp.broadcast_to(cb_ref[...], (tt, d))
    for w in range(cwid):
        off = tail - (cwid - 1) + w
        rc = rc + cw_ref[w:w + 1, :] * rext[off:off + tt, :]
    rprev_ref[...] = r[tt - tail:, :]

    rcb = rc.astype(BF16)
    gx = jnp.concatenate(
        [jnp.dot(rcb[:, n * bw:(n + 1) * bw], wgx_ref[n], preferred_element_type=F32)
         for n in range(n_blk)], axis=1)
    ga = jnp.concatenate(
        [jnp.dot(rcb[:, n * bw:(n + 1) * bw], wga_ref[n], preferred_element_type=F32)
         for n in range(n_blk)], axis=1)
    gate_x = _sigmoid(gx + bgx_ref[...])
    gate_a = _sigmoid(ga + bga_ref[...])
    z = -ap_ref[...]
    softplus = jnp.maximum(z, 0.0) + jnp.log(1.0 + jnp.exp(-jnp.abs(z)))
    log_a = -LRU_C * gate_a * softplus
    a_ref[...] = jnp.exp(log_a)
    u_ref[...] = jnp.sqrt(1.0 - jnp.exp(2.0 * log_a)) * (gate_x * rc)

    row = lax.broadcasted_iota(I32, (SCAN_ROWS, d), 0)

    def slab(s, carry):
        off = pl.multiple_of(s * SCAN_ROWS, SCAN_ROWS)
        a = a_ref[pl.ds(off, SCAN_ROWS), :]
        b = u_ref[pl.ds(off, SCAN_ROWS), :]
        step = 1
        while step < SCAN_ROWS:
            a_sh = pltpu.roll(a, step, 0)
            b_sh = pltpu.roll(b, step, 0)
            live = row >= step
            b = jnp.where(live, a * b_sh + b, b)
            a = jnp.where(live, a * a_sh, a)
            step *= 2
        h = b + a * carry
        u_ref[pl.ds(off, SCAN_ROWS), :] = h
        return h[SCAN_ROWS - 1:SCAN_ROWS, :]

    carry_ref[...] = lax.fori_loop(0, tt // SCAN_ROWS, slab, carry_ref[...])
    hy_ref[...] = (u_ref[...] * y_ref[...]).astype(hy_ref.dtype)


def _lru_core(r, y, conv_w, conv_b, wgx, wga, bgx, bga, a_param):
    t, d = r.shape
    tt = min(TOKEN_TILE, t)
    row = lambda i: (i, 0)
    fixed2 = lambda i: (0, 0)
    fixed3 = lambda i: (0, 0, 0)
    return pl.pallas_call(
        _lru_core_kernel,
        grid=(t // tt,),
        in_specs=[pl.BlockSpec((tt, d), row),
                  pl.BlockSpec((tt, d), row),
                  pl.BlockSpec(conv_w.shape, fixed2),
                  pl.BlockSpec((1, d), fixed2),
                  pl.BlockSpec(wgx.shape, fixed3),
                  pl.BlockSpec(wga.shape, fixed3),
                  pl.BlockSpec((1, d), fixed2),
                  pl.BlockSpec((1, d), fixed2),
                  pl.BlockSpec((1, d), fixed2)],
        out_specs=pl.BlockSpec((tt, d), row),
        out_shape=jax.ShapeDtypeStruct((t, d), BF16),
        scratch_shapes=[pltpu.VMEM((SUBLANES, d), F32),
                        pltpu.VMEM((1, d), F32),
                        pltpu.VMEM((tt, d), F32),
                        pltpu.VMEM((tt, d), F32)],
        compiler_params=_params(1),
        name="lru_core",
    )(r, y, conv_w, conv_b, wgx, wga, bgx, bga, a_param)


def _moe(xn_tiles, t, d, ids, gates, rank, counts128, layer, wg, wu, wd, bg, bu, bd,
         ln_g, ln_b, alpha):
    tm = EXPERT_ROWS
    n_exp = wg.shape[1]
    counts = counts128[:, 0].astype(I32)
    nch = (counts + tm - 1) // tm
    start = (jnp.cumsum(nch) - nch).astype(I32) * tm
    experts = jnp.arange(n_exp, dtype=I32)
    dest = rank + jnp.sum(jnp.where(ids[:, :, None] == experts, start, 0), axis=-1)
    last = ((jnp.sum(nch) - 1) * tm).astype(I32).reshape(1)
    tok_bits = (t + COMBINE_BATCH - 1).bit_length()
    gates_ext = jnp.pad(gates, ((0, 1), (0, (1 << tok_bits) - t))).reshape(-1)
    return _moe_experts(xn_tiles, t, d, tok_bits, dest.reshape(-1).astype(I32), gates_ext,
                        start, counts, nch.astype(I32), last, layer, wg, wu, wd, bg, bu, bd,
                        ln_g, ln_b, alpha)


def kernel(x, rel_bias_table, w_qkv, w_attn_out, w_lru_in, lru_conv_w, lru_conv_b, w_lru_gate_x, b_lru_gate_x, w_lru_gate_a, b_lru_gate_a, lru_a_param, w_lru_out, w_router, b_router, w_exp_gate, b_exp_gate, w_exp_up, b_exp_up, w_exp_down, b_exp_down, ln_mix_g, ln_mix_b, ln_ffn_g, ln_ffn_b):
    bsz, t, d = x.shape
    depth = w_router.shape[0]
    n_mixers = 2
    alpha = (2 * depth) ** 0.25
    assert t % MOBA_BLOCK == 0 and t % TOKEN_TILE == 0

    biases = _attn_biases(rel_bias_table, t)
    w_qkv_b = w_qkv.astype(BF16)
    w_attn_out_b = w_attn_out.astype(BF16)
    w_lru_in_b = w_lru_in.astype(BF16)
    w_lru_out_b = w_lru_out.astype(BF16)
    wgx_b = w_lru_gate_x.astype(BF16)
    wga_b = w_lru_gate_a.astype(BF16)
    wg_b = w_exp_gate.astype(BF16)
    wu_b = w_exp_up.astype(BF16)
    wd_b = w_exp_down.astype(BF16)
    wr_t = jnp.swapaxes(w_router, 1, 2)
    wr_hi = wr_t.astype(BF16)
    wr_lo = (wr_t - wr_hi.astype(F32)).astype(BF16)

    bg4 = b_exp_gate[:, :, None, :]
    bu4 = b_exp_up[:, :, None, :]
    bd4 = b_exp_down[:, :, None, :]

    outs = []
    for b in range(bsz):
        xs = x
        for i in range(depth):
            j = i // n_mixers
            if i % n_mixers == 0:
                qkv = _matmul(xs, b, t, w_qkv_b, j, BF16)
                a = _moba_attention(qkv, biases)
                w_out = w_attn_out_b
            else:
                y, r = _lru_in(xs, t, w_lru_in_b, j)
                a = _lru_core(r, y, lru_conv_w[j], lru_conv_b[j][None], wgx_b[j], wga_b[j],
                              b_lru_gate_x[j][None], b_lru_gate_a[j][None], lru_a_param[j][None])
                w_out = w_lru_out_b
            xn, ids, gates, rank, counts = _proj_ln_router(
                a, w_out, j, xs, b, ln_mix_g[i][None], ln_mix_b[i][None],
                wr_hi[i], wr_lo[i], b_router[i][:, None], alpha)
            xs = _moe(xn, t, d, ids, gates, rank, counts, i, wg_b, wu_b, wd_b, bg4, bu4, bd4,
                      ln_ffn_g[i][None], ln_ffn_b[i][None], alpha)
        outs.append(xs.reshape(t + COMBINE_BATCH, d)[:t])
    return jnp.stack(outs, axis=0)
```
